```python
import math, functools
import jax, jax.numpy as jnp
from jax import lax
import numpy as np

D_MODEL = 2048
BATCH = 1
SEQ = 8192
DEPTH = 2
DEC_BATCH = 32
DEC_SEQ = 8
PAST_LEN = 8192
PAGE_SIZE = 128

HEAD_DIM = 128
D_ATTN = D_MODEL // 2
N_HEADS = D_ATTN // HEAD_DIM
D_SSM = D_MODEL // 4
SSM_GROUP = 16
SSM_GROUPS = D_SSM // SSM_GROUP
SSM_STATE = 64
D_POOL = D_MODEL - D_ATTN - D_SSM
POOL_WINDOWS = (2, 4, 8, 16)
POOL_GROUP = D_POOL // len(POOL_WINDOWS)
POOL_BUF = max(POOL_WINDOWS) - 1
D_MIX = D_ATTN + D_SSM + D_POOL
IN_SPLITS = (D_ATTN, 2 * D_ATTN, 3 * D_ATTN, 3 * D_ATTN + N_HEADS, 3 * D_ATTN + N_HEADS + D_SSM)
D_IN = 3 * D_ATTN + N_HEADS + D_SSM + D_POOL
N_EXPERT_GROUPS = 4
EXPERTS_PER_GROUP = 4
N_EXPERTS = N_EXPERT_GROUPS * EXPERTS_PER_GROUP
TOP_K = 2
D_EXPERT = 512
Q_BLOCK = 128
EPS = 1e-6
F_BIAS_INIT = 2.0

kernel_name = 'hybrid_fox_s5_pool_hmoe_step'


def rmsnorm(x, g):
    xf = x.astype(jnp.float32)
    y = xf * lax.rsqrt(jnp.mean(xf * xf, axis=-1, keepdims=True) + EPS)
    return (y * g.astype(jnp.float32)).astype(x.dtype)


def fox_prompt(q, k, v, logf):
    B, T, H, Dh = q.shape
    c = jnp.cumsum(logf, axis=1)
    cT = jnp.transpose(c, (0, 2, 1))
    scale = HEAD_DIM ** -0.5
    key_pos = jnp.arange(T)

    def block(i):
        s0 = i * Q_BLOCK
        qb = lax.dynamic_slice_in_dim(q, s0, Q_BLOCK, axis=1)
        cb = lax.dynamic_slice_in_dim(cT, s0, Q_BLOCK, axis=2)
        logits = jnp.einsum('bqhd,bkhd->bhqk', qb, k, preferred_element_type=jnp.float32) * scale
        logits = logits + cb[..., None] - cT[:, :, None, :]
        qpos = s0 + jnp.arange(Q_BLOCK)
        mask = key_pos[None, :] <= qpos[:, None]
        p = jax.nn.softmax(jnp.where(mask, logits, -jnp.inf), axis=-1)
        return jnp.einsum('bhqk,bkhd->bqhd', p.astype(v.dtype), v)

    out = lax.map(block, jnp.arange(T // Q_BLOCK))
    return jnp.transpose(out, (1, 0, 2, 3, 4)).reshape(B, T, H * Dh)


def fox_sample(q, k, v, logf, k_past, v_past, logf_past):
    Bd, Tn, H, Dh = q.shape
    P = k_past.shape[1]
    scale = HEAD_DIM ** -0.5
    c_past = jnp.cumsum(logf_past.astype(jnp.float32), axis=1)
    c_new = c_past[:, -1:, :] + jnp.cumsum(logf, axis=1)
    cpT = jnp.transpose(c_past, (0, 2, 1))
    cnT = jnp.transpose(c_new, (0, 2, 1))
    s_past = jnp.einsum('bqhd,bkhd->bhqk', q, k_past, preferred_element_type=jnp.float32) * scale
    s_past = s_past + cnT[..., None] - cpT[:, :, None, :]
    s_new = jnp.einsum('bqhd,bkhd->bhqk', q, k, preferred_element_type=jnp.float32) * scale
    s_new = s_new + cnT[..., None] - cnT[:, :, None, :]
    causal = jnp.tril(jnp.ones((Tn, Tn), dtype=bool))
    s_new = jnp.where(causal, s_new, -jnp.inf)
    p = jax.nn.softmax(jnp.concatenate([s_past, s_new], axis=-1), axis=-1).astype(v.dtype)
    out = jnp.einsum('bhqk,bkhd->bqhd', p[..., :P], v_past) + jnp.einsum('bhqk,bkhd->bqhd', p[..., P:], v)
    return out.reshape(Bd, Tn, H * Dh)


def _complex_affine_combine(e1, e2):
    a1r, a1i, b1r, b1i = e1
    a2r, a2i, b2r, b2i = e2
    return (a1r * a2r - a1i * a2i,
            a1r * a2i + a1i * a2r,
            a2r * b1r - a2i * b1i + b2r,
            a2r * b1i + a2i * b1r + b2i)


def s5_mix(u, h0_re, h0_im, a_re, a_im, log_dt, b_re, b_im, c_re, c_im, d, w_glu):
    f32 = jnp.float32
    B, T, _ = u.shape
    a_re = a_re.astype(f32)
    a_im = a_im.astype(f32)
    dt = jnp.exp(log_dt.astype(f32))[:, None]
    mag = jnp.exp(dt * a_re)
    ab_re = mag * jnp.cos(dt * a_im)
    ab_im = mag * jnp.sin(dt * a_im)
    den = a_re * a_re + a_im * a_im
    nr = ab_re - 1.0
    coef_re = (nr * a_re + ab_im * a_im) / den
    coef_im = (ab_im * a_re - nr * a_im) / den
    b_re = b_re.astype(f32)
    b_im = b_im.astype(f32)
    bb_re = coef_re[..., None] * b_re - coef_im[..., None] * b_im
    bb_im = coef_re[..., None] * b_im + coef_im[..., None] * b_re
    uf = u.astype(f32).reshape(B, T, SSM_GROUPS, SSM_GROUP)
    bu_re = jnp.einsum('gnp,btgp->btgn', bb_re, uf)
    bu_im = jnp.einsum('gnp,btgp->btgn', bb_im, uf)
    h0_re = h0_re.astype(f32)
    h0_im = h0_im.astype(f32)
    bu_re = bu_re.at[:, 0].add(ab_re * h0_re - ab_im * h0_im)
    bu_im = bu_im.at[:, 0].add(ab_re * h0_im + ab_im * h0_re)
    a_re_t = jnp.broadcast_to(ab_re, bu_re.shape)
    a_im_t = jnp.broadcast_to(ab_im, bu_im.shape)
    _, _, h_re, h_im = lax.associative_scan(_complex_affine_combine, (a_re_t, a_im_t, bu_re, bu_im), axis=1)
    y = (jnp.einsum('gpn,btgn->btgp', c_re.astype(f32), h_re)
         - jnp.einsum('gpn,btgn->btgp', c_im.astype(f32), h_im))
    y = y.reshape(B, T, D_SSM) + d.astype(f32) * uf.reshape(B, T, D_SSM)
    g = jax.nn.gelu(y)
    out = g * jax.nn.sigmoid(g @ w_glu.astype(f32))
    return out.astype(u.dtype), h_re[:, -1], h_im[:, -1]


def pool_mix(p_in, buf, pos0, pool_w, pool_scale):
    f32 = jnp.float32
    B, T, _ = p_in.shape
    ext = jnp.concatenate([buf.astype(p_in.dtype), p_in], axis=1)
    extf = ext.astype(f32)
    cs = jnp.concatenate([jnp.zeros((B, 1, D_POOL), f32), jnp.cumsum(extf, axis=1)], axis=1)
    end = cs[:, POOL_BUF + 1:POOL_BUF + 1 + T]
    pos = pos0 + jnp.arange(T)
    xt = extf[:, POOL_BUF:]
    outs = []
    for gi, w in enumerate(POOL_WINDOWS):
        sl = slice(gi * POOL_GROUP, (gi + 1) * POOL_GROUP)
        start = cs[:, POOL_BUF + 1 - w:POOL_BUF + 1 - w + T, sl]
        cnt = jnp.minimum(w, pos + 1).astype(f32)[None, :, None]
        outs.append((end[..., sl] - start) / cnt - xt[..., sl])
    pooled = jnp.stack(outs, axis=2)
    mixed = jnp.einsum('btgc,gcd->btgd', pooled, pool_w.astype(f32)).reshape(B, T, D_POOL)
    out = mixed * pool_scale.astype(f32)
    return out.astype(p_in.dtype), ext[:, -POOL_BUF:]


def hier_moe(x, rg_w, rg_b, re_w, re_b, w_gate, w_up, w_down):
    f32 = jnp.float32
    B, T, _ = x.shape
    g_logits = (x @ rg_w).astype(f32) + rg_b.astype(f32)
    g_prob = jax.nn.softmax(g_logits, axis=-1)
    g_sel = jnp.argmax(g_logits, axis=-1)
    g_w = jnp.max(g_prob, axis=-1)
    e_logits = ((x @ re_w).astype(f32) + re_b.astype(f32)).reshape(B, T, N_EXPERT_GROUPS, EXPERTS_PER_GROUP)
    e_sel = jnp.einsum('btge,btg->bte', e_logits, jax.nn.one_hot(g_sel, N_EXPERT_GROUPS, dtype=f32))
    top_v, top_i = lax.top_k(e_sel, TOP_K)
    top_p = jax.nn.softmax(top_v, axis=-1) * g_w[..., None]
    expert_idx = g_sel[..., None] * EXPERTS_PER_GROUP + top_i
    gates = jnp.einsum('btke,btk->bte', jax.nn.one_hot(expert_idx, N_EXPERTS, dtype=f32), top_p)
    hg = jnp.einsum('btd,edf->btef', x, w_gate)
    hu = jnp.einsum('btd,edf->btef', x, w_up)
    act = jax.nn.silu(hg) * hu * gates[..., None].astype(x.dtype)
    return jnp.einsum('btef,efd->btd', act, w_down)


def trunk_layer(x, attend, h0_re, h0_im, pool_buf, pos0,
                norm1_g, w_in, f_bias, q_gain, k_gain,
                ssm_a_re, ssm_a_im, ssm_log_dt, ssm_b_re, ssm_b_im, ssm_c_re, ssm_c_im, ssm_d, ssm_w_glu,
                pool_w, pool_scale, out_norm_g, w_out, norm2_g,
                router_group_w, router_group_b, router_expert_w, router_expert_b,
                moe_w_gate, moe_w_up, moe_w_down):
    B, T, _ = x.shape
    h = rmsnorm(x, norm1_g)
    proj = h @ w_in
    q, k, v, f_logit, u, p_in = jnp.split(proj, IN_SPLITS, axis=-1)
    q = rmsnorm(q.reshape(B, T, N_HEADS, HEAD_DIM), q_gain)
    k = rmsnorm(k.reshape(B, T, N_HEADS, HEAD_DIM), k_gain)
    v = v.reshape(B, T, N_HEADS, HEAD_DIM)
    logf = jax.nn.log_sigmoid((f_logit + f_bias).astype(jnp.float32))
    a_out = attend(q, k, v, logf)
    s_out, h_re, h_im = s5_mix(u, h0_re, h0_im, ssm_a_re, ssm_a_im, ssm_log_dt,
                               ssm_b_re, ssm_b_im, ssm_c_re, ssm_c_im, ssm_d, ssm_w_glu)
    p_out, new_buf = pool_mix(p_in, pool_buf, pos0, pool_w, pool_scale)
    mixed = jnp.concatenate([
        rmsnorm(a_out, out_norm_g[:D_ATTN]),
        rmsnorm(s_out, out_norm_g[D_ATTN:D_ATTN + D_SSM]),
        rmsnorm(p_out, out_norm_g[D_ATTN + D_SSM:])], axis=-1)
    x = (x + mixed @ w_out).astype(x.dtype)
    x = (x + hier_moe(rmsnorm(x, norm2_g), router_group_w, router_group_b, router_expert_w,
                      router_expert_b, moe_w_gate, moe_w_up, moe_w_down)).astype(x.dtype)
    return x, (k, v, logf, h_re, h_im, new_buf)


def _stack(states, i):
    return jnp.stack([s[i] for s in states], axis=0)


def setup_inputs(seed: int = 0) -> dict:
    key = jax.random.key(seed)
    ks = iter(jax.random.split(key, 48))
    f32 = jnp.float32

    def nrm(shape, scale):
        return jax.random.normal(next(ks), shape, f32) * scale

    n_pages = PAST_LEN // PAGE_SIZE
    n_used = DEC_BATCH * n_pages
    n_phys = (n_used * 5) // 4
    x_prompt = nrm((BATCH, SEQ, D_MODEL), 1.0)
    x_sample = nrm((DEC_BATCH, DEC_SEQ, D_MODEL), 1.0)
    cache_k = nrm((DEPTH, n_phys, PAGE_SIZE, N_HEADS, HEAD_DIM), 1.0)
    cache_v = nrm((DEPTH, n_phys, PAGE_SIZE, N_HEADS, HEAD_DIM), 1.0)
    cache_logf = jax.nn.log_sigmoid(F_BIAS_INIT + nrm((DEPTH, n_phys, PAGE_SIZE, N_HEADS), 1.0))
    page_table = jax.random.permutation(next(ks), n_phys)[:n_used].reshape(DEC_BATCH, n_pages).astype(jnp.int32)
    state_ssm_re = nrm((DEPTH, DEC_BATCH, SSM_GROUPS, SSM_STATE), 0.5)
    state_ssm_im = nrm((DEPTH, DEC_BATCH, SSM_GROUPS, SSM_STATE), 0.5)
    state_pool = nrm((DEPTH, DEC_BATCH, POOL_BUF, D_POOL), 1.0)

    norm1_g = 1.0 + nrm((DEPTH, D_MODEL), 0.05)
    w_in = nrm((DEPTH, D_MODEL, D_IN), D_MODEL ** -0.5)
    f_bias = F_BIAS_INIT + nrm((DEPTH, N_HEADS), 0.1)
    q_gain = 1.0 + nrm((DEPTH, HEAD_DIM), 0.05)
    k_gain = 1.0 + nrm((DEPTH, HEAD_DIM), 0.05)
    ssm_a_re = -0.5 + nrm((DEPTH, SSM_GROUPS, SSM_STATE), 0.01)
    ssm_a_im = math.pi * jnp.arange(SSM_STATE, dtype=f32) + nrm((DEPTH, SSM_GROUPS, SSM_STATE), 0.01)
    ssm_log_dt = jax.random.uniform(next(ks), (DEPTH, SSM_GROUPS), f32, math.log(1e-3), math.log(1e-1))
    ssm_b_re = nrm((DEPTH, SSM_GROUPS, SSM_STATE, SSM_GROUP), (2 * SSM_GROUP) ** -0.5)
    ssm_b_im = nrm((DEPTH, SSM_GROUPS, SSM_STATE, SSM_GROUP), (2 * SSM_GROUP) ** -0.5)
    ssm_c_re = nrm((DEPTH, SSM_GROUPS, SSM_GROUP, SSM_STATE), (2 * SSM_STATE) ** -0.5)
    ssm_c_im = nrm((DEPTH, SSM_GROUPS, SSM_GROUP, SSM_STATE), (2 * SSM_STATE) ** -0.5)
    ssm_d = nrm((DEPTH, D_SSM), 1.0)
    ssm_w_glu = nrm((DEPTH, D_SSM, D_SSM), D_SSM ** -0.5)
    pool_w = nrm((DEPTH, len(POOL_WINDOWS), POOL_GROUP, POOL_GROUP), POOL_GROUP ** -0.5)
    pool_scale = 1.0 + nrm((DEPTH, D_POOL), 0.1)
    out_norm_g = 1.0 + nrm((DEPTH, D_MIX), 0.05)
    w_out = nrm((DEPTH, D_MIX, D_MODEL), D_MIX ** -0.5)
    norm2_g = 1.0 + nrm((DEPTH, D_MODEL), 0.05)
    router_group_w = nrm((DEPTH, D_MODEL, N_EXPERT_GROUPS), D_MODEL ** -0.5)
    router_group_b = nrm((DEPTH, N_EXPERT_GROUPS), 0.01)
    router_expert_w = nrm((DEPTH, D_MODEL, N_EXPERTS), D_MODEL ** -0.5)
    router_expert_b = nrm((DEPTH, N_EXPERTS), 0.01)
    moe_w_gate = nrm((DEPTH, N_EXPERTS, D_MODEL, D_EXPERT), D_MODEL ** -0.5)
    moe_w_up = nrm((DEPTH, N_EXPERTS, D_MODEL, D_EXPERT), D_MODEL ** -0.5)
    moe_w_down = nrm((DEPTH, N_EXPERTS, D_EXPERT, D_MODEL), D_EXPERT ** -0.5)
    return {'x_prompt': x_prompt, 'x_sample': x_sample, 'cache_k': cache_k, 'cache_v': cache_v,
            'cache_logf': cache_logf, 'page_table': page_table, 'state_ssm_re': state_ssm_re,
            'state_ssm_im': state_ssm_im, 'state_pool': state_pool,
            'norm1_g': norm1_g, 'w_in': w_in, 'f_bias': f_bias, 'q_gain': q_gain, 'k_gain': k_gain,
            'ssm_a_re': ssm_a_re, 'ssm_a_im': ssm_a_im, 'ssm_log_dt': ssm_log_dt,
            'ssm_b_re': ssm_b_re, 'ssm_b_im': ssm_b_im, 'ssm_c_re': ssm_c_re, 'ssm_c_im': ssm_c_im,
            'ssm_d': ssm_d, 'ssm_w_glu': ssm_w_glu, 'pool_w': pool_w, 'pool_scale': pool_scale,
            'out_norm_g': out_norm_g, 'w_out': w_out, 'norm2_g': norm2_g,
            'router_group_w': router_group_w, 'router_group_b': router_group_b,
            'router_expert_w': router_expert_w, 'router_expert_b': router_expert_b,
            'moe_w_gate': moe_w_gate, 'moe_w_up': moe_w_up, 'moe_w_down': moe_w_down}


def reference(x_prompt, x_sample, cache_k, cache_v, cache_logf, page_table, state_ssm_re, state_ssm_im,
              state_pool, norm1_g, w_in, f_bias, q_gain, k_gain, ssm_a_re, ssm_a_im, ssm_log_dt,
              ssm_b_re, ssm_b_im, ssm_c_re, ssm_c_im, ssm_d, ssm_w_glu, pool_w, pool_scale, out_norm_g,
              w_out, norm2_g, router_group_w, router_group_b, router_expert_w, router_expert_b,
              moe_w_gate, moe_w_up, moe_w_down):
    n_prompt = x_prompt.shape[0]
    n_dec = x_sample.shape[0]
    past = page_table.shape[1] * cache_k.shape[2]
    h0p = jnp.zeros((n_prompt, SSM_GROUPS, SSM_STATE), jnp.float32)
    bufp = jnp.zeros((n_prompt, POOL_BUF, D_POOL), x_prompt.dtype)
    hp, hs = x_prompt, x_sample
    new_p, new_s = [], []
    for l in range(DEPTH):
        params = (norm1_g[l], w_in[l], f_bias[l], q_gain[l], k_gain[l],
                  ssm_a_re[l], ssm_a_im[l], ssm_log_dt[l], ssm_b_re[l], ssm_b_im[l],
                  ssm_c_re[l], ssm_c_im[l], ssm_d[l], ssm_w_glu[l],
                  pool_w[l], pool_scale[l], out_norm_g[l], w_out[l], norm2_g[l],
                  router_group_w[l], router_group_b[l], router_expert_w[l], router_expert_b[l],
                  moe_w_gate[l], moe_w_up[l], moe_w_down[l])
        hp, st_p = trunk_layer(hp, fox_prompt, h0p, h0p, bufp, 0, *params)
        k_past = cache_k[l][page_table].reshape(n_dec, past, N_HEADS, HEAD_DIM)
        v_past = cache_v[l][page_table].reshape(n_dec, past, N_HEADS, HEAD_DIM)
        lf_past = cache_logf[l][page_table].reshape(n_dec, past, N_HEADS)
        attend_s = functools.partial(fox_sample, k_past=k_past, v_past=v_past, logf_past=lf_past)
        hs, st_s = trunk_layer(hs, attend_s, state_ssm_re[l], state_ssm_im[l], state_pool[l], past, *params)
        new_p.append(st_p)
        new_s.append(st_s)
    return (hp, hs,
            _stack(new_p, 0), _stack(new_p, 1), _stack(new_p, 2), _stack(new_p, 3), _stack(new_p, 4), _stack(new_p, 5),
            _stack(new_s, 0), _stack(new_s, 1), _stack(new_s, 2), _stack(new_s, 3), _stack(new_s, 4), _stack(new_s, 5))
```

```python
import functools
import math

import jax
import jax.numpy as jnp
from jax import lax
from jax.experimental import pallas as pl
from jax.experimental.pallas import tpu as pltpu

F32 = jnp.float32
BF16 = jnp.bfloat16
HIGHEST = lax.Precision.HIGHEST
NT_DIMS = (((1,), (1,)), ((), ()))

EPS = 1e-6
HEAD_DIM = 128
N_HEADS = 8
D_ATTN = N_HEADS * HEAD_DIM
D_SSM = 512
SSM_GROUPS = 32
SSM_GROUP = 16
SSM_STATE = 64
D_STATE = SSM_GROUPS * SSM_STATE
D_POOL = 512
POOL_WINDOWS = (2, 4, 8, 16)
POOL_GROUP = 128
POOL_BUF = 15
N_EXPERT_GROUPS = 4
EXPERTS_PER_GROUP = 4
N_EXPERTS = 16
D_EXPERT = 512
PAGE_SIZE = 128
NEG_BIG = -1e30

ROW_TILE = 256
ATTN_TILE = 512
PAGES_PER_STEP = 8
SCAN_COLS = 512
VMEM_LIMIT = 56 * 1024 * 1024


def _params(sem, vmem=VMEM_LIMIT):
    return pltpu.CompilerParams(dimension_semantics=sem, vmem_limit_bytes=vmem)


def _const_spec(shape):
    nd = len(shape)
    return pl.BlockSpec(shape, lambda *_: (0,) * nd, pipeline_mode=pl.Buffered(1))


def _rms(y, g):
    return y * lax.rsqrt(jnp.mean(y * y, axis=-1, keepdims=True) + EPS) * g


def _proj_kernel(x_ref, g1_ref, wqkv_ref, wup_ref, wft_ref, fb_ref, qg_ref, kg_ref,
                 q_ref, k_ref, v_ref, kb_ref, vb_ref, lf_ref, u_ref, p_ref):
    hb = _rms(x_ref[...], g1_ref[...]).astype(BF16)
    qg = qg_ref[...]
    kg = kg_ref[...]
    for c in range(D_ATTN // 256):
        lo = c * 256
        qc = jnp.dot(hb, wqkv_ref[:, lo:lo + 256], preferred_element_type=F32)
        kc = jnp.dot(hb, wqkv_ref[:, D_ATTN + lo:D_ATTN + lo + 256], preferred_element_type=F32)
        vc = jnp.dot(hb, wqkv_ref[:, 2 * D_ATTN + lo:2 * D_ATTN + lo + 256], preferred_element_type=F32)
        for j in range(2):
            a = lo + j * HEAD_DIM
            q_ref[:, a:a + HEAD_DIM] = _rms(qc[:, j * HEAD_DIM:(j + 1) * HEAD_DIM], qg)
            kn = _rms(kc[:, j * HEAD_DIM:(j + 1) * HEAD_DIM], kg)
            k_ref[:, a:a + HEAD_DIM] = kn
            kb_ref[:, a:a + HEAD_DIM] = kn.astype(BF16)
        v_ref[:, lo:lo + 256] = vc
        vb_ref[:, lo:lo + 256] = vc.astype(BF16)
    u_ref[...] = jnp.dot(hb, wup_ref[:, :D_SSM], preferred_element_type=F32)
    p_ref[...] = jnp.dot(hb, wup_ref[:, D_SSM:], preferred_element_type=F32)
    ft = lax.dot_general(wft_ref[...], hb, NT_DIMS, preferred_element_type=F32)[:N_HEADS]
    z = ft + fb_ref[...]
    lf_ref[...] = jnp.minimum(z, 0.0) - jnp.log1p(jnp.exp(-jnp.abs(z)))


def _proj(x, g1, wqkv, wup, wft, fb, qg, kg):
    m, d = x.shape
    tm = ROW_TILE
    row = lambda w: pl.BlockSpec((tm, w), lambda i: (i, 0))
    outs = (
        jax.ShapeDtypeStruct((m, D_ATTN), F32), jax.ShapeDtypeStruct((m, D_ATTN), F32),
        jax.ShapeDtypeStruct((m, D_ATTN), F32), jax.ShapeDtypeStruct((m, D_ATTN), BF16),
        jax.ShapeDtypeStruct((m, D_ATTN), BF16), jax.ShapeDtypeStruct((N_HEADS, m), F32),
        jax.ShapeDtypeStruct((m, D_SSM), F32), jax.ShapeDtypeStruct((m, D_POOL), F32))
    return pl.pallas_call(
        _proj_kernel,
        grid=(m // tm,),
        in_specs=[row(d), _const_spec((1, d)), _const_spec(wqkv.shape), _const_spec(wup.shape),
                  _const_spec(wft.shape), _const_spec((N_HEADS, 1)), _const_spec((1, HEAD_DIM)),
                  _const_spec((1, HEAD_DIM))],
        out_specs=(row(D_ATTN), row(D_ATTN), row(D_ATTN), row(D_ATTN), row(D_ATTN),
                   pl.BlockSpec((N_HEADS, tm), lambda i: (0, i)), row(D_SSM), row(D_POOL)),
        out_shape=outs,
        compiler_params=_params(("parallel",)),
        name="proj",
    )(x, g1, wqkv, wup, wft, fb, qg, kg)


def _cumsum_kernel(lf_ref, c_ref):
    t = c_ref.shape[1]
    w = ATTN_TILE
    r = lax.broadcasted_iota(jnp.int32, (w, w), 0)
    c = lax.broadcasted_iota(jnp.int32, (w, w), 1)
    upper = jnp.where(r <= c, 1.0, 0.0).astype(F32)
    carry = jnp.zeros((N_HEADS, 1), F32)
    for s in range(t // w):
        blk = jnp.dot(lf_ref[:, s * w:(s + 1) * w], upper, precision=HIGHEST,
                      preferred_element_type=F32) + carry
        c_ref[:, s * w:(s + 1) * w] = blk
        carry = blk[:, w - 1:w]


def _prompt_cumsum(lft, t):
    return pl.pallas_call(
        _cumsum_kernel,
        grid=(1,),
        in_specs=[pl.BlockSpec((N_HEADS, t), lambda i: (0, 0))],
        out_specs=pl.BlockSpec((N_HEADS, t), lambda i: (0, 0)),
        out_shape=jax.ShapeDtypeStruct((N_HEADS, t), F32),
        compiler_params=_params(("arbitrary",)),
        name="prompt_cumsum",
    )(lft)


def _fox_prompt_kernel(q_ref, k_ref, v_ref, c_ref, o_ref):
    qi = pl.program_id(1)
    tq = q_ref.shape[0]
    q = (q_ref[...] * (HEAD_DIM ** -0.5)).astype(BF16)

    def tile(ki, carry, masked):
        m, l, acc = carry
        off = pl.multiple_of(ki * tq, tq)
        kt = k_ref[pl.ds(off, tq), :]
        vt = v_ref[pl.ds(off, tq), :]
        s = lax.dot_general(q, kt, NT_DIMS, preferred_element_type=F32) - c_ref[ki]
        if masked:
            r = lax.broadcasted_iota(jnp.int32, s.shape, 0)
            c = lax.broadcasted_iota(jnp.int32, s.shape, 1)
            s = jnp.where(c <= r, s, NEG_BIG)
        m_new = jnp.maximum(m, jnp.max(s, axis=-1, keepdims=True))
        alpha = jnp.exp(m - m_new)
        p = jnp.exp(s - m_new)
        l = alpha * l + jnp.sum(p, axis=-1, keepdims=True)
        acc = alpha * acc + jnp.dot(p.astype(BF16), vt, preferred_element_type=F32)
        return m_new, l, acc

    init = (jnp.full((tq, 1), NEG_BIG, F32), jnp.zeros((tq, 1), F32), jnp.zeros((tq, HEAD_DIM), F32))
    carry = lax.fori_loop(0, qi, lambda ki, cr: tile(ki, cr, False), init)
    _, l, acc = tile(qi, carry, True)
    o_ref[...] = acc / l


def _fox_prompt(q, kb, vb, c, t):
    tq = ATTN_TILE
    nq = t // tq
    c4 = c.reshape(N_HEADS, nq, 1, tq)
    return pl.pallas_call(
        _fox_prompt_kernel,
        grid=(N_HEADS, nq),
        in_specs=[pl.BlockSpec((tq, HEAD_DIM), lambda h, i: (i, h)),
                  pl.BlockSpec((t, HEAD_DIM), lambda h, i: (0, h)),
                  pl.BlockSpec((t, HEAD_DIM), lambda h, i: (0, h)),
                  pl.BlockSpec((None, nq, 1, tq), lambda h, i: (h, 0, 0, 0))],
        out_specs=pl.BlockSpec((tq, HEAD_DIM), lambda h, i: (i, h)),
        out_shape=jax.ShapeDtypeStruct((t, D_ATTN), F32),
        compiler_params=_params(("parallel", "parallel")),
        name="fox_prompt",
    )(q, kb, vb, c4)


def _fox_sample_kernel(pt_ref, q_ref, kn_ref, vn_ref, lfn_ref, *rest, pg):
    del pt_ref
    k_refs = rest[:pg]
    v_refs = rest[pg:2 * pg]
    lf_refs = rest[2 * pg:3 * pg]
    o_ref = rest[3 * pg]
    m_scr, l_scr, acc_scr, c_scr = rest[3 * pg + 1:]
    g = pl.program_id(1)
    tn = q_ref.shape[0]

    @pl.when(g == 0)
    def _():
        m_scr[...] = jnp.full(m_scr.shape, NEG_BIG, F32)
        l_scr[...] = jnp.zeros(l_scr.shape, F32)
        acc_scr[...] = jnp.zeros(acc_scr.shape, F32)
        c_scr[...] = jnp.zeros(c_scr.shape, F32)

    r = lax.broadcasted_iota(jnp.int32, (PAGE_SIZE, PAGE_SIZE), 0)
    c = lax.broadcasted_iota(jnp.int32, (PAGE_SIZE, PAGE_SIZE), 1)
    upper = jnp.where(r <= c, 1.0, 0.0).astype(F32)
    carry = c_scr[...]
    c_pages = []
    for j in range(pg):
        cp = jnp.dot(lf_refs[j][...], upper, precision=HIGHEST, preferred_element_type=F32) + carry
        carry = cp[:, PAGE_SIZE - 1:PAGE_SIZE]
        c_pages.append(cp)
    c_scr[...] = carry

    q = (q_ref[...] * (HEAD_DIM ** -0.5)).astype(BF16)
    for h in range(N_HEADS):
        hs = slice(h * HEAD_DIM, (h + 1) * HEAD_DIM)
        qh = q[:, hs]
        s = jnp.concatenate(
            [lax.dot_general(qh, k_refs[j][:, hs].astype(BF16), NT_DIMS, preferred_element_type=F32)
             - c_pages[j][h:h + 1, :] for j in range(pg)], axis=1)
        m_old = m_scr[h]
        m_new = jnp.maximum(m_old, jnp.max(s, axis=-1, keepdims=True))
        alpha = jnp.exp(m_old - m_new)
        p = jnp.exp(s - m_new).astype(BF16)
        l_scr[h] = alpha * l_scr[h] + jnp.sum(p.astype(F32), axis=-1, keepdims=True)
        pv = jnp.dot(p[:, :PAGE_SIZE], v_refs[0][:, hs].astype(BF16), preferred_element_type=F32)
        for j in range(1, pg):
            pv = pv + jnp.dot(p[:, j * PAGE_SIZE:(j + 1) * PAGE_SIZE], v_refs[j][:, hs].astype(BF16),
                              preferred_element_type=F32)
        acc_scr[h] = alpha * acc_scr[h] + pv
        m_scr[h] = m_new

    @pl.when(g == pl.num_programs(1) - 1)
    def _():
        lfn = lfn_ref[...]
        run = c_scr[...]
        c_new = []
        for j in range(tn):
            run = run + lfn[:, j:j + 1]
            c_new.append(run)
        qf = q_ref[...] * (HEAD_DIM ** -0.5)
        row = lax.broadcasted_iota(jnp.int32, (tn, 1), 0)
        for h in range(N_HEADS):
            hs = slice(h * HEAD_DIM, (h + 1) * HEAD_DIM)
            qh = qf[:, hs]
            kh = kn_ref[:, hs]
            vh = vn_ref[:, hs]
            cols = []
            for j in range(tn):
                sj = jnp.sum(qh * kh[j:j + 1, :], axis=-1, keepdims=True) - c_new[j][h:h + 1, :]
                cols.append(jnp.where(row >= j, sj, NEG_BIG))
            m_old = m_scr[h]
            m_fin = m_old
            for sj in cols:
                m_fin = jnp.maximum(m_fin, sj)
            alpha = jnp.exp(m_old - m_fin)
            l = alpha * l_scr[h]
            acc = alpha * acc_scr[h]
            for j, sj in enumerate(cols):
                pj = jnp.exp(sj - m_fin)
                l = l + pj
                acc = acc + pj * vh[j:j + 1, :]
            o_ref[:, hs] = acc / l


def _fox_sample(q, k, v, lfn, cache_k, cache_v, cache_lft, page_table, layer, n_phys, t_rows):
    nb, n_pages = page_table.shape
    tn = (q.shape[0] - t_rows) // nb
    pg = min(PAGES_PER_STEP, n_pages)
    row0 = t_rows // tn
    base = layer * n_phys

    def page_spec(rows, cols, j):
        return pl.BlockSpec((None, rows, cols), lambda b, g, pt: (base + pt[b, g * pg + j], 0, 0))

    new_spec = pl.BlockSpec((tn, D_ATTN), lambda b, g, pt: (row0 + b, 0))
    in_specs = [new_spec, new_spec, new_spec,
                pl.BlockSpec((None, N_HEADS, tn), lambda b, g, pt: (b, 0, 0))]
    in_specs += [page_spec(PAGE_SIZE, D_ATTN, j) for j in range(pg)]
    in_specs += [page_spec(PAGE_SIZE, D_ATTN, j) for j in range(pg)]
    in_specs += [page_spec(N_HEADS, PAGE_SIZE, j) for j in range(pg)]
    grid_spec = pltpu.PrefetchScalarGridSpec(
        num_scalar_prefetch=1,
        grid=(nb, n_pages // pg),
        in_specs=in_specs,
        out_specs=pl.BlockSpec((tn, D_ATTN), lambda b, g, pt: (b, 0)),
        scratch_shapes=[pltpu.VMEM((N_HEADS, tn, 1), F32), pltpu.VMEM((N_HEADS, tn, 1), F32),
                        pltpu.VMEM((N_HEADS, tn, HEAD_DIM), F32), pltpu.VMEM((N_HEADS, 1), F32)])
    return pl.pallas_call(
        functools.partial(_fox_sample_kernel, pg=pg),
        grid_spec=grid_spec,
        out_shape=jax.ShapeDtypeStruct((nb * tn, D_ATTN), F32),
        compiler_params=_params(("parallel", "arbitrary")),
        name="fox_sample",
    )(page_table, q, k, v, lfn, *([cache_k] * pg), *([cache_v] * pg), *([cache_lft] * pg))


def _s5_kernel(u_ref, h0re_ref, h0im_ref, are_ref, aim_ref, ldt_ref, bdre_ref, bdim_ref,
               cdre_ref, cdim_ref, d_ref, wglu_ref, *rest, nb, nt):
    o_ref, hre_out, him_out, bbre, bbim, bure, buim, hre, him, abre, abim = rest
    i = pl.program_id(0)

    @pl.when(i == 0)
    def _():
        a_re = are_ref[...]
        a_im = aim_ref[...]
        dt = jnp.exp(ldt_ref[...])
        mag = jnp.exp(dt * a_re)
        ab_re = mag * jnp.cos(dt * a_im)
        ab_im = mag * jnp.sin(dt * a_im)
        den = a_re * a_re + a_im * a_im
        nr = ab_re - 1.0
        coef_re = (nr * a_re + ab_im * a_im) / den
        coef_im = (ab_im * a_re - nr * a_im) / den
        bbre[...] = (coef_re * bdre_ref[...] - coef_im * bdim_ref[...]).astype(BF16)
        bbim[...] = (coef_re * bdim_ref[...] + coef_im * bdre_ref[...]).astype(BF16)
        abre[...] = ab_re
        abim[...] = ab_im
        hre[...] = h0re_ref[...]
        him[...] = h0im_ref[...]

    u = u_ref[...]
    if nb > 1:
        n = nb * nt
        tsh = nt.bit_length() - 1
        bsh = nb.bit_length() - 1
        r = lax.broadcasted_iota(jnp.int32, (n, n), 0)
        c = lax.broadcasted_iota(jnp.int32, (n, n), 1)
        to_time_major = jnp.where(((r & (nb - 1)) == (c >> tsh)) & ((r >> bsh) == (c & (nt - 1))), 1.0, 0.0).astype(F32)
        to_seq_major = jnp.where(((c & (nb - 1)) == (r >> tsh)) & ((c >> bsh) == (r & (nt - 1))), 1.0, 0.0).astype(F32)
        u = jnp.dot(to_time_major, u, precision=HIGHEST, preferred_element_type=F32)
    ub = u.astype(BF16)
    bure[...] = jnp.dot(ub, bbre[...], preferred_element_type=F32)
    buim[...] = jnp.dot(ub, bbim[...], preferred_element_type=F32)

    for cidx in range(D_STATE // SCAN_COLS):
        cs = slice(cidx * SCAN_COLS, (cidx + 1) * SCAN_COLS)
        ar = jnp.broadcast_to(abre[:, cs], (nb, SCAN_COLS))
        ai = jnp.broadcast_to(abim[:, cs], (nb, SCAN_COLS))

        def step(t, carry, cs=cs, ar=ar, ai=ai):
            hr, hi = carry
            rows = pl.ds(pl.multiple_of(t * nb, nb), nb) if nb > 1 else pl.ds(t, 1)
            nr = ar * hr - ai * hi + bure[rows, cs]
            ni = ar * hi + ai * hr + buim[rows, cs]
            bure[rows, cs] = nr
            buim[rows, cs] = ni
            return nr, ni

        hr, hi = lax.fori_loop(0, nt, step, (hre[:, cs], him[:, cs]), unroll=8)
        hre[:, cs] = hr
        him[:, cs] = hi

    y = (jnp.dot(bure[...].astype(BF16), cdre_ref[...], preferred_element_type=F32)
         - jnp.dot(buim[...].astype(BF16), cdim_ref[...], preferred_element_type=F32)
         + d_ref[...] * u)
    gl = 0.5 * y * (1.0 + jnp.tanh(math.sqrt(2.0 / math.pi) * (y + 0.044715 * (y * y * y))))
    gate = jax.nn.sigmoid(jnp.dot(gl.astype(BF16), wglu_ref[...], preferred_element_type=F32))
    out = gl * gate
    if nb > 1:
        out = jnp.dot(to_seq_major, out, precision=HIGHEST, preferred_element_type=F32)
    o_ref[...] = out
    hre_out[...] = hre[...]
    him_out[...] = him[...]


def _s5(u_all, row_block0, nb, nt, n_steps, h0re, h0im, consts):
    rows = nb * nt
    are, aim, ldt, bdre, bdim, cdre, cdim, dvec, wglu = consts
    in_specs = [pl.BlockSpec((rows, D_SSM), lambda i: (row_block0 + i, 0)),
                _const_spec((nb, D_STATE)), _const_spec((nb, D_STATE)),
                _const_spec((1, D_STATE)), _const_spec((1, D_STATE)), _const_spec((1, D_STATE)),
                _const_spec((D_SSM, D_STATE)), _const_spec((D_SSM, D_STATE)),
                _const_spec((D_STATE, D_SSM)), _const_spec((D_STATE, D_SSM)),
                _const_spec((1, D_SSM)), _const_spec((D_SSM, D_SSM))]
    args = [u_all, h0re, h0im, are, aim, ldt, bdre, bdim, cdre, cdim, dvec, wglu]
    state = jax.ShapeDtypeStruct((nb, D_STATE), F32)
    return pl.pallas_call(
        functools.partial(_s5_kernel, nb=nb, nt=nt),
        grid=(n_steps,),
        in_specs=in_specs,
        out_specs=(pl.BlockSpec((rows, D_SSM), lambda i: (i, 0)),
                   pl.BlockSpec((nb, D_STATE), lambda i: (0, 0)),
                   pl.BlockSpec((nb, D_STATE), lambda i: (0, 0))),
        out_shape=(jax.ShapeDtypeStruct((n_steps * rows, D_SSM), F32), state, state),
        scratch_shapes=[pltpu.VMEM((D_SSM, D_STATE), BF16), pltpu.VMEM((D_SSM, D_STATE), BF16),
                        pltpu.VMEM((rows, D_STATE), F32), pltpu.VMEM((rows, D_STATE), F32),
                        pltpu.VMEM((nb, D_STATE), F32), pltpu.VMEM((nb, D_STATE), F32),
                        pltpu.VMEM((1, D_STATE), F32), pltpu.VMEM((1, D_STATE), F32)],
        compiler_params=_params(("arbitrary",)),
        name="s5_sample" if nb > 1 else "s5_prompt",
    )(*args)


def _pool_kernel(cur_ref, *rest, seq_len, prev_rows, pos0, prompt):
    if prompt:
        prev_ref, buf_ref, pw_ref, ps_ref, o_ref = rest
    else:
        buf_ref, pw_ref, ps_ref, o_ref = rest
    i = pl.program_id(0)
    cur = cur_ref[...]
    n = cur.shape[0]
    if prompt:
        prev = jnp.where(i == 0, buf_ref[...], prev_ref[...])
    else:
        prev = buf_ref[...]
    npv = prev.shape[0]
    lsh = seq_len.bit_length() - 1
    psh = prev_rows.bit_length() - 1
    r = lax.broadcasted_iota(jnp.int32, (n, n), 0)
    c = lax.broadcasted_iota(jnp.int32, (n, n), 1)
    same = (r >> lsh) == (c >> lsh)
    d = r - c
    r1 = lax.broadcasted_iota(jnp.int32, (n, npv), 0)
    c1 = lax.broadcasted_iota(jnp.int32, (n, npv), 1)
    same1 = (r1 >> lsh) == (c1 >> psh)
    d1 = (r1 & (seq_len - 1)) + prev_rows - (c1 & (prev_rows - 1))
    local = lax.broadcasted_iota(jnp.int32, (n, 1), 0) & (seq_len - 1)
    pos = pos0 + i * n + local
    outs = []
    for gi, w in enumerate(POOL_WINDOWS):
        gs = slice(gi * POOL_GROUP, (gi + 1) * POOL_GROUP)
        band = jnp.where(same & (d >= 0) & (d < w), 1.0, 0.0).astype(F32)
        band_prev = jnp.where(same1 & (d1 < w), 1.0, 0.0).astype(F32)
        xg = cur[:, gs]
        wsum = (jnp.dot(band, xg, precision=HIGHEST, preferred_element_type=F32)
                + jnp.dot(band_prev, prev[:, gs], precision=HIGHEST, preferred_element_type=F32))
        cnt = jnp.minimum(w, pos + 1).astype(F32)
        pooled = wsum / cnt - xg
        outs.append(jnp.dot(pooled.astype(BF16), pw_ref[gi], preferred_element_type=F32))
    o_ref[...] = jnp.concatenate(outs, axis=1) * ps_ref[...]


def _pool_prompt(p_all, buf_pad, pw, ps, t):
    tt = ROW_TILE
    ratio = tt // POOL_GROUP
    return pl.pallas_call(
        functools.partial(_pool_kernel, seq_len=tt, prev_rows=POOL_GROUP, pos0=0, prompt=True),
        grid=(t // tt,),
        in_specs=[pl.BlockSpec((tt, D_POOL), lambda i: (i, 0)),
                  pl.BlockSpec((POOL_GROUP, D_POOL), lambda i: (jnp.maximum(i * ratio - 1, 0), 0)),
                  _const_spec((POOL_GROUP, D_POOL)), _const_spec(pw.shape), _const_spec((1, D_POOL))],
        out_specs=pl.BlockSpec((tt, D_POOL), lambda i: (i, 0)),
        out_shape=jax.ShapeDtypeStruct((t, D_POOL), F32),
        compiler_params=_params(("parallel",)),
        name="pool_prompt",
    )(p_all, p_all, buf_pad, pw, ps)


def _pool_sample(p_all, buf_pad, pw, ps, t, nb, tn, past):
    rows = nb * tn
    blk = t // rows
    return pl.pallas_call(
        functools.partial(_pool_kernel, seq_len=tn, prev_rows=16, pos0=past, prompt=False),
        grid=(1,),
        in_specs=[pl.BlockSpec((rows, D_POOL), lambda i: (blk, 0)),
                  _const_spec(buf_pad.shape), _const_spec(pw.shape), _const_spec((1, D_POOL))],
        out_specs=pl.BlockSpec((rows, D_POOL), lambda i: (0, 0)),
        out_shape=jax.ShapeDtypeStruct((rows, D_POOL), F32),
        compiler_params=_params(("arbitrary",)),
        name="pool_sample",
    )(p_all, buf_pad, pw, ps)


def _outproj_kernel(x_ref, ap_ref, as_ref, sp_ref, ss_ref, pp_ref, ps_ref, ong_ref, wout_ref, n2_ref,
                    wr_ref, rb_ref, xm_ref, xn_ref, ids_ref, gates_ref, *, prompt_tiles):
    g = ong_ref[...]
    sample = pl.program_id(0) >= prompt_tiles
    ma = _rms(jnp.where(sample, as_ref[...], ap_ref[...]), g[:, :D_ATTN]).astype(BF16)
    ms = _rms(jnp.where(sample, ss_ref[...], sp_ref[...]), g[:, D_ATTN:D_ATTN + D_SSM]).astype(BF16)
    mp = _rms(jnp.where(sample, ps_ref[...], pp_ref[...]), g[:, D_ATTN + D_SSM:]).astype(BF16)
    xm = (x_ref[...]
          + jnp.dot(ma, wout_ref[:D_ATTN, :], preferred_element_type=F32)
          + jnp.dot(ms, wout_ref[D_ATTN:D_ATTN + D_SSM, :], preferred_element_type=F32)
          + jnp.dot(mp, wout_ref[D_ATTN + D_SSM:, :], preferred_element_type=F32))
    xm_ref[...] = xm
    xn = _rms(xm, n2_ref[...])
    xn_ref[...] = xn
    lt = lax.dot_general(wr_ref[...], xn, NT_DIMS, precision=HIGHEST,
                         preferred_element_type=F32) + rb_ref[...]
    ng, ne = N_EXPERT_GROUPS, EXPERTS_PER_GROUP
    gl = [lt[j:j + 1, :] for j in range(ng)]
    gmax = functools.reduce(jnp.maximum, gl)
    gsel = jnp.full(gmax.shape, ng - 1, jnp.int32)
    for j in range(ng - 2, -1, -1):
        gsel = jnp.where(gl[j] == gmax, j, gsel)
    gw = 1.0 / functools.reduce(jnp.add, [jnp.exp(v - gmax) for v in gl])
    es = []
    for j in range(ne):
        v = lt[ng + (ng - 1) * ne + j:ng + (ng - 1) * ne + j + 1, :]
        for gi in range(ng - 2, -1, -1):
            v = jnp.where(gsel == gi, lt[ng + gi * ne + j:ng + gi * ne + j + 1, :], v)
        es.append(v)

    def first_argmax(vals):
        vmax = functools.reduce(jnp.maximum, vals)
        idx = jnp.full(vmax.shape, len(vals) - 1, jnp.int32)
        for j in range(len(vals) - 2, -1, -1):
            idx = jnp.where(vals[j] == vmax, j, idx)
        return vmax, idx

    v1, i1 = first_argmax(es)
    v2, i2 = first_argmax([jnp.where(i1 == j, -jnp.inf, es[j]) for j in range(ne)])
    t = jnp.exp(v2 - v1)
    p1 = 1.0 / (1.0 + t)
    p2 = t / (1.0 + t)
    row = lax.broadcasted_iota(jnp.int32, ids_ref.shape, 0)
    ids_ref[...] = jnp.where(row == 0, gsel * ne + i1, jnp.where(row == 1, gsel * ne + i2, 0))
    gates_ref[...] = jnp.where(row == 0, p1 * gw, jnp.where(row == 1, p2 * gw, 0.0))


def _outproj(x, mix_prompt, mix_sample, ong, wout, n2, wr, rb):
    m, d = x.shape
    tm = ROW_TILE
    pt = mix_prompt[0].shape[0] // tm
    row = lambda w: pl.BlockSpec((tm, w), lambda i: (i, 0))
    prow = lambda w: pl.BlockSpec((tm, w), lambda i: (jnp.minimum(i, pt - 1), 0))
    srow = lambda w: pl.BlockSpec((tm, w), lambda i: (jnp.maximum(i - pt, 0), 0))
    col = pl.BlockSpec((8, tm), lambda i: (0, i))
    return pl.pallas_call(
        functools.partial(_outproj_kernel, prompt_tiles=pt),
        grid=(m // tm,),
        in_specs=[row(d), prow(D_ATTN), srow(D_ATTN), prow(D_SSM), srow(D_SSM), prow(D_POOL), srow(D_POOL),
                  _const_spec((1, d)),
                  _const_spec(wout.shape), _const_spec((1, d)), _const_spec(wr.shape),
                  _const_spec(rb.shape)],
        out_specs=(row(d), row(d), col, col),
        out_shape=(jax.ShapeDtypeStruct((m, d), F32), jax.ShapeDtypeStruct((m, d), F32),
                   jax.ShapeDtypeStruct((8, m), jnp.int32), jax.ShapeDtypeStruct((8, m), F32)),
        compiler_params=_params(("parallel",)),
        name="outproj_router",
    )(x, mix_prompt[0], mix_sample[0], mix_prompt[1], mix_sample[1], mix_prompt[2], mix_sample[2],
      ong, wout, n2, wr, rb)


def _row_copy(src_hbm, row, dst, r, sem):
    return pltpu.make_async_copy(src_hbm.at[pl.ds(row, 1), :], dst.at[pl.ds(r, 1), :], sem)


def _gather_kernel(idx_ref, x_hbm, o_ref, sem):
    tm = o_ref.shape[0]
    base = pl.program_id(0) * tm

    def issue(r, carry):
        _row_copy(x_hbm, idx_ref[base + r], o_ref, r, sem).start()
        return carry

    lax.fori_loop(0, tm, issue, 0)

    def drain(r, carry):
        _row_copy(x_hbm, 0, o_ref, r, sem).wait()
        return carry

    lax.fori_loop(0, tm, drain, 0)


def _gather_rows(src_tok, xn, m_pad):
    tm = ROW_TILE
    d = xn.shape[1]
    grid_spec = pltpu.PrefetchScalarGridSpec(
        num_scalar_prefetch=1,
        grid=(m_pad // tm,),
        in_specs=[pl.BlockSpec(memory_space=pl.ANY)],
        out_specs=pl.BlockSpec((tm, d), lambda i, idx: (i, 0)),
        scratch_shapes=[pltpu.SemaphoreType.DMA(())])
    return pl.pallas_call(
        _gather_kernel,
        grid_spec=grid_spec,
        out_shape=jax.ShapeDtypeStruct((m_pad, d), xn.dtype),
        compiler_params=_params(("arbitrary",)),
        name="moe_gather",
    )(src_tok, xn)


def _expert_kernel(te_ref, tv_ref, xs_ref, gate_ref, wg_ref, wu_ref, wd_ref, o_ref, wgb, wub, wdb):
    t = pl.program_id(0)
    prev = te_ref[jnp.maximum(t - 1, 0)]
    changed = jnp.logical_or(t == 0, te_ref[t] != prev)

    @pl.when(changed)
    def _():
        wgb[...] = wg_ref[...].astype(BF16)
        wub[...] = wu_ref[...].astype(BF16)
        wdb[...] = wd_ref[...].astype(BF16)

    @pl.when(tv_ref[t] == 1)
    def _():
        xb = xs_ref[...].astype(BF16)
        hg = jnp.dot(xb, wgb[...], preferred_element_type=F32)
        hu = jnp.dot(xb, wub[...], preferred_element_type=F32)
        act = (hg * jax.nn.sigmoid(hg)) * hu * gate_ref[...]
        o_ref[...] = jnp.dot(act.astype(BF16), wdb[...], preferred_element_type=F32)

    @pl.when(tv_ref[t] == 0)
    def _():
        o_ref[...] = jnp.zeros(o_ref.shape, o_ref.dtype)


def _expert_ffn(tile_expert, tile_valid, xs, gate_sorted, wg, wu, wd):
    m_pad, d = xs.shape
    tm = ROW_TILE
    grid_spec = pltpu.PrefetchScalarGridSpec(
        num_scalar_prefetch=2,
        grid=(m_pad // tm,),
        in_specs=[pl.BlockSpec((tm, d), lambda t, te, tv: (t, 0)),
                  pl.BlockSpec((tm, 1), lambda t, te, tv: (t, 0)),
                  pl.BlockSpec((None, d, D_EXPERT), lambda t, te, tv: (te[t], 0, 0)),
                  pl.BlockSpec((None, d, D_EXPERT), lambda t, te, tv: (te[t], 0, 0)),
                  pl.BlockSpec((None, D_EXPERT, d), lambda t, te, tv: (te[t], 0, 0))],
        out_specs=pl.BlockSpec((tm, d), lambda t, te, tv: (t, 0)),
        scratch_shapes=[pltpu.VMEM((d, D_EXPERT), BF16), pltpu.VMEM((d, D_EXPERT), BF16),
                        pltpu.VMEM((D_EXPERT, d), BF16)])
    return pl.pallas_call(
        _expert_kernel,
        grid_spec=grid_spec,
        out_shape=jax.ShapeDtypeStruct((m_pad, d), F32),
        compiler_params=_params(("arbitrary",)),
        name="expert_ffn",
    )(tile_expert, tile_valid, xs, gate_sorted, wg, wu, wd)


def _combine_kernel(d1_ref, d2_ref, xm_ref, ys_hbm, o_ref, buf1, buf2, sem):
    tm = o_ref.shape[0]
    base = pl.program_id(0) * tm

    def issue(r, carry):
        _row_copy(ys_hbm, d1_ref[base + r], buf1, r, sem).start()
        _row_copy(ys_hbm, d2_ref[base + r], buf2, r, sem).start()
        return carry

    lax.fori_loop(0, tm, issue, 0)

    def drain(r, carry):
        _row_copy(ys_hbm, 0, buf1, r, sem).wait()
        _row_copy(ys_hbm, 0, buf2, r, sem).wait()
        return carry

    lax.fori_loop(0, tm, drain, 0)
    o_ref[...] = xm_ref[...] + buf1[...] + buf2[...]


def _combine(dest1, dest2, xm, ys):
    m, d = xm.shape
    tm = ROW_TILE
    grid_spec = pltpu.PrefetchScalarGridSpec(
        num_scalar_prefetch=2,
        grid=(m // tm,),
        in_specs=[pl.BlockSpec((tm, d), lambda i, a, b: (i, 0)), pl.BlockSpec(memory_space=pl.ANY)],
        out_specs=pl.BlockSpec((tm, d), lambda i, a, b: (i, 0)),
        scratch_shapes=[pltpu.VMEM((tm, d), F32), pltpu.VMEM((tm, d), F32),
                        pltpu.SemaphoreType.DMA(())])
    return pl.pallas_call(
        _combine_kernel,
        grid_spec=grid_spec,
        out_shape=jax.ShapeDtypeStruct((m, d), F32),
        compiler_params=_params(("arbitrary",)),
        name="moe_combine",
    )(dest1, dest2, xm, ys)


def _dispatch_plan(ids, gates, m):
    tm = ROW_TILE
    n_flat = 2 * m
    n_tiles = n_flat // tm + N_EXPERTS
    m_pad = n_tiles * tm
    e_flat = ids[:2].reshape(n_flat)
    g_flat = gates[:2].reshape(n_flat)
    onehot = (e_flat[:, None] == jnp.arange(N_EXPERTS, dtype=jnp.int32)[None, :]).astype(jnp.int32)
    csum = jnp.cumsum(onehot, axis=0)
    counts = csum[-1]
    rank = jnp.sum(csum * onehot, axis=1) - 1
    tiles_per = (counts + tm - 1) // tm
    tile_end = jnp.cumsum(tiles_per)
    tile_start = tile_end - tiles_per
    slot0 = tile_start * tm
    dest = slot0[e_flat] + rank
    row_start = jnp.cumsum(counts) - counts
    order = jnp.argsort(e_flat, stable=True).astype(jnp.int32)
    tiles = jnp.arange(n_tiles, dtype=jnp.int32)
    te_raw = jnp.sum((tiles[:, None] >= tile_end[None, :]).astype(jnp.int32), axis=1)
    tile_valid = (tiles < tile_end[-1]).astype(jnp.int32)
    last_e = jnp.sum((tile_end[-1] - 1 >= tile_end).astype(jnp.int32))
    tile_expert = jnp.where(tile_valid == 1, te_raw, last_e).astype(jnp.int32)
    slots = jnp.arange(m_pad, dtype=jnp.int32)
    s_e = tile_expert[slots // tm]
    within = slots - slot0[s_e]
    s_valid = (within < counts[s_e]) & (tile_valid[slots // tm] == 1)
    src_flat = order[jnp.clip(row_start[s_e] + within, 0, n_flat - 1)]
    src_tok = jnp.where(s_valid, src_flat % m, 0).astype(jnp.int32)
    gate_sorted = jnp.where(s_valid, g_flat[src_flat], 0.0).reshape(m_pad, 1)
    return src_tok, gate_sorted, tile_expert, tile_valid, dest[:m].astype(jnp.int32), dest[m:].astype(jnp.int32), m_pad


def _block_diag(w):
    g, r, c = w.shape
    eye = jnp.eye(g, dtype=w.dtype)
    return (eye[:, None, :, None] * w[:, :, None, :]).reshape(g * r, g * c)


def kernel(x_prompt, x_sample, cache_k, cache_v, cache_logf, page_table, state_ssm_re, state_ssm_im, state_pool, norm1_g, w_in, f_bias, q_gain, k_gain, ssm_a_re, ssm_a_im, ssm_log_dt, ssm_b_re, ssm_b_im, ssm_c_re, ssm_c_im, ssm_d, ssm_w_glu, pool_w, pool_scale, out_norm_g, w_out, norm2_g, router_group_w, router_group_b, router_expert_w, router_expert_b, moe_w_gate, moe_w_up, moe_w_down):
    n_prompt, t, d = x_prompt.shape
    nb, tn, _ = x_sample.shape
    depth = w_in.shape[0]
    n_phys = cache_k.shape[1]
    past = page_table.shape[1] * cache_k.shape[2]
    assert n_prompt == 1
    ms = nb * tn
    m = t + ms
    x = jnp.concatenate([x_prompt.reshape(t, d), x_sample.reshape(ms, d)], axis=0)

    ck = cache_k.reshape(depth * n_phys, PAGE_SIZE, D_ATTN)
    cv = cache_v.reshape(depth * n_phys, PAGE_SIZE, D_ATTN)
    clf = jnp.swapaxes(cache_logf, 2, 3).reshape(depth * n_phys, N_HEADS, PAGE_SIZE)
    zero_state = jnp.zeros((1, D_STATE), F32)
    zero_buf = jnp.zeros((POOL_GROUP, D_POOL), F32)

    outs_p, outs_s = [], []
    for l in range(depth):
        wl = w_in[l]
        wqkv = wl[:, :3 * D_ATTN].astype(BF16)
        wup = wl[:, 3 * D_ATTN + N_HEADS:].astype(BF16)
        wft = jnp.pad(wl[:, 3 * D_ATTN:3 * D_ATTN + N_HEADS].T, ((0, 8), (0, 0))).astype(BF16)
        q, k, v, kb, vb, lft, u, p_in = _proj(
            x, norm1_g[l].reshape(1, d), wqkv, wup, wft, f_bias[l].reshape(N_HEADS, 1),
            q_gain[l].reshape(1, HEAD_DIM), k_gain[l].reshape(1, HEAD_DIM))

        c_prompt = _prompt_cumsum(lft, t)
        a_p = _fox_prompt(q, kb, vb, c_prompt, t)
        lfn = lft[:, t:].reshape(N_HEADS, nb, tn).transpose(1, 0, 2)
        a_s = _fox_sample(q, k, v, lfn, ck, cv, clf, page_table, l, n_phys, t)

        consts = (ssm_a_re[l].reshape(1, D_STATE), ssm_a_im[l].reshape(1, D_STATE),
                  jnp.repeat(ssm_log_dt[l], SSM_STATE).reshape(1, D_STATE),
                  _block_diag(jnp.swapaxes(ssm_b_re[l], 1, 2)), _block_diag(jnp.swapaxes(ssm_b_im[l], 1, 2)),
                  _block_diag(jnp.swapaxes(ssm_c_re[l], 1, 2)).astype(BF16),
                  _block_diag(jnp.swapaxes(ssm_c_im[l], 1, 2)).astype(BF16),
                  ssm_d[l].reshape(1, D_SSM), ssm_w_glu[l].astype(BF16))
        s_p, hre_p, him_p = _s5(u, 0, 1, ROW_TILE, t // ROW_TILE, zero_state, zero_state, consts)
        s_s, hre_s, him_s = _s5(u, t // ms, nb, tn, 1, state_ssm_re[l].reshape(nb, D_STATE),
                                state_ssm_im[l].reshape(nb, D_STATE), consts)

        pw = pool_w[l].astype(BF16)
        ps = pool_scale[l].reshape(1, D_POOL)
        po_p = _pool_prompt(p_in, zero_buf, pw, ps, t)
        buf_s = jnp.pad(state_pool[l], ((0, 0), (1, 0), (0, 0))).reshape(nb * 16, D_POOL)
        po_s = _pool_sample(p_in, buf_s, pw, ps, t, nb, tn, past)

        wr = jnp.pad(jnp.concatenate([router_group_w[l], router_expert_w[l]], axis=1).T, ((0, 4), (0, 0)))
        rb = jnp.pad(jnp.concatenate([router_group_b[l], router_expert_b[l]]), (0, 4)).reshape(24, 1)
        xm, xn, ids, gates = _outproj(x, (a_p, s_p, po_p), (a_s, s_s, po_s), out_norm_g[l].reshape(1, d),
                                      w_out[l].astype(BF16), norm2_g[l].reshape(1, d), wr, rb)

        src_tok, gate_sorted, tile_expert, tile_valid, dest1, dest2, m_pad = _dispatch_plan(ids, gates, m)
        xs = _gather_rows(src_tok, xn, m_pad)
        ys = _expert_ffn(tile_expert, tile_valid, xs, gate_sorted, moe_w_gate[l], moe_w_up[l], moe_w_down[l])
        x = _combine(dest1, dest2, xm, ys)

        lf_rows = lft.T
        outs_p.append((k[:t].reshape(1, t, N_HEADS, HEAD_DIM), v[:t].reshape(1, t, N_HEADS, HEAD_DIM),
                       lf_rows[:t].reshape(1, t, N_HEADS),
                       hre_p.reshape(1, SSM_GROUPS, SSM_STATE), him_p.reshape(1, SSM_GROUPS, SSM_STATE),
                       p_in[t - POOL_BUF:t].reshape(1, POOL_BUF, D_POOL)))
        p_s = p_in[t:].reshape(nb, tn, D_POOL)
        outs_s.append((k[t:].reshape(nb, tn, N_HEADS, HEAD_DIM), v[t:].reshape(nb, tn, N_HEADS, HEAD_DIM),
                       lf_rows[t:].reshape(nb, tn, N_HEADS),
                       hre_s.reshape(nb, SSM_GROUPS, SSM_STATE), him_s.reshape(nb, SSM_GROUPS, SSM_STATE),
                       jnp.concatenate([state_pool[l], p_s], axis=1)[:, -POOL_BUF:]))

    stack = lambda outs, i: jnp.stack([o[i] for o in outs], axis=0)
    return (x[:t].reshape(1, t, d), x[t:].reshape(nb, tn, d),
            *[stack(outs_p, i) for i in range(6)], *[stack(outs_s, i) for i in range(6)])
```

```python
import functools
import math

import jax
import jax.numpy as jnp
from jax import lax
from jax.experimental import pallas as pl
from jax.experimental.pallas import tpu as pltpu

F32 = jnp.float32
BF16 = jnp.bfloat16
HIGHEST = lax.Precision.HIGHEST
NT_DIMS = (((1,), (1,)), ((), ()))

EPS = 1e-6
HEAD_DIM = 128
N_HEADS = 8
D_ATTN = N_HEADS * HEAD_DIM
D_SSM = 512
SSM_GROUPS = 32
SSM_GROUP = 16
SSM_STATE = 64
D_STATE = SSM_GROUPS * SSM_STATE
D_POOL = 512
POOL_WINDOWS = (2, 4, 8, 16)
POOL_GROUP = 128
POOL_BUF = 15
N_EXPERT_GROUPS = 4
EXPERTS_PER_GROUP = 4
N_EXPERTS = 16
D_EXPERT = 512
PAGE_SIZE = 128
NEG_BIG = -1e30

ROW_TILE = 256
ATTN_TILE = 512
PAGES_PER_STEP = 8
SCAN_COLS = 512
VMEM_LIMIT = 56 * 1024 * 1024


def _params(sem, vmem=VMEM_LIMIT):
    return pltpu.CompilerParams(dimension_semantics=sem, vmem_limit_bytes=vmem)


def _const_spec(shape):
    nd = len(shape)
    return pl.BlockSpec(shape, lambda *_: (0,) * nd, pipeline_mode=pl.Buffered(1))


def _rms(y, g):
    return y * lax.rsqrt(jnp.mean(y * y, axis=-1, keepdims=True) + EPS) * g


def _head_rows(h, n):
    return pl.ds(h, n, stride=N_HEADS)


def _proj_kernel(x_ref, g1_ref, wqkv_ref, wup_ref, wft_ref, fb_ref, qg_ref, kg_ref,
                 q_ref, k_ref, v_ref, kb_ref, vb_ref, lf_ref, u_ref, p_ref):
    hb = _rms(x_ref[...], g1_ref[...]).astype(BF16)
    qg = qg_ref[...]
    kg = kg_ref[...]
    for c in range(D_ATTN // 256):
        lo = c * 256
        qc = jnp.dot(hb, wqkv_ref[:, lo:lo + 256], preferred_element_type=F32)
        kc = jnp.dot(hb, wqkv_ref[:, D_ATTN + lo:D_ATTN + lo + 256], preferred_element_type=F32)
        vc = jnp.dot(hb, wqkv_ref[:, 2 * D_ATTN + lo:2 * D_ATTN + lo + 256], preferred_element_type=F32)
        for j in range(2):
            a = lo + j * HEAD_DIM
            q_ref[:, a:a + HEAD_DIM] = _rms(qc[:, j * HEAD_DIM:(j + 1) * HEAD_DIM], qg)
            kn = _rms(kc[:, j * HEAD_DIM:(j + 1) * HEAD_DIM], kg)
            k_ref[_head_rows(2 * c + j, kn.shape[0]), :] = kn
            kb_ref[:, a:a + HEAD_DIM] = kn.astype(BF16)
            v_ref[_head_rows(2 * c + j, kn.shape[0]), :] = vc[:, j * HEAD_DIM:(j + 1) * HEAD_DIM]
        vb_ref[:, lo:lo + 256] = vc.astype(BF16)
    u_ref[...] = jnp.dot(hb, wup_ref[:, :D_SSM], preferred_element_type=F32)
    p_ref[...] = jnp.dot(hb, wup_ref[:, D_SSM:], preferred_element_type=F32)
    ft = lax.dot_general(wft_ref[...], hb, NT_DIMS, preferred_element_type=F32)[:N_HEADS]
    z = ft + fb_ref[...]
    lf_ref[...] = jnp.minimum(z, 0.0) - jnp.log1p(jnp.exp(-jnp.abs(z)))


def _proj(x, g1, wqkv, wup, wft, fb, qg, kg):
    m, d = x.shape
    tm = ROW_TILE
    row = lambda w: pl.BlockSpec((tm, w), lambda i: (i, 0))
    heads = pl.BlockSpec((tm * N_HEADS, HEAD_DIM), lambda i: (i, 0))
    outs = (
        jax.ShapeDtypeStruct((m, D_ATTN), F32), jax.ShapeDtypeStruct((m * N_HEADS, HEAD_DIM), F32),
        jax.ShapeDtypeStruct((m * N_HEADS, HEAD_DIM), F32), jax.ShapeDtypeStruct((m, D_ATTN), BF16),
        jax.ShapeDtypeStruct((m, D_ATTN), BF16), jax.ShapeDtypeStruct((N_HEADS, m), F32),
        jax.ShapeDtypeStruct((m, D_SSM), F32), jax.ShapeDtypeStruct((m, D_POOL), F32))
    return pl.pallas_call(
        _proj_kernel,
        grid=(m // tm,),
        in_specs=[row(d), _const_spec((1, d)), _const_spec(wqkv.shape), _const_spec(wup.shape),
                  _const_spec(wft.shape), _const_spec((N_HEADS, 1)), _const_spec((1, HEAD_DIM)),
                  _const_spec((1, HEAD_DIM))],
        out_specs=(row(D_ATTN), heads, heads, row(D_ATTN), row(D_ATTN),
                   pl.BlockSpec((N_HEADS, tm), lambda i: (0, i)), row(D_SSM), row(D_POOL)),
        out_shape=outs,
        compiler_params=_params(("parallel",)),
        name="proj",
    )(x, g1, wqkv, wup, wft, fb, qg, kg)


def _cumsum_kernel(lf_ref, c_ref):
    t = c_ref.shape[1]
    w = ATTN_TILE
    r = lax.broadcasted_iota(jnp.int32, (w, w), 0)
    c = lax.broadcasted_iota(jnp.int32, (w, w), 1)
    upper = jnp.where(r <= c, 1.0, 0.0).astype(F32)
    carry = jnp.zeros((N_HEADS, 1), F32)
    for s in range(t // w):
        blk = jnp.dot(lf_ref[:, s * w:(s + 1) * w], upper, precision=HIGHEST,
                      preferred_element_type=F32) + carry
        c_ref[:, s * w:(s + 1) * w] = blk
        carry = blk[:, w - 1:w]


def _prompt_cumsum(lft, t):
    return pl.pallas_call(
        _cumsum_kernel,
        grid=(1,),
        in_specs=[pl.BlockSpec((N_HEADS, t), lambda i: (0, 0))],
        out_specs=pl.BlockSpec((N_HEADS, t), lambda i: (0, 0)),
        out_shape=jax.ShapeDtypeStruct((N_HEADS, t), F32),
        compiler_params=_params(("arbitrary",)),
        name="prompt_cumsum",
    )(lft)


def _fox_prompt_kernel(q_ref, k_ref, v_ref, c_ref, o_ref):
    qi = pl.program_id(1)
    tq = q_ref.shape[0]
    q = (q_ref[...] * (HEAD_DIM ** -0.5)).astype(BF16)

    def tile(ki, carry, masked):
        m, l, acc = carry
        off = pl.multiple_of(ki * tq, tq)
        kt = k_ref[pl.ds(off, tq), :]
        vt = v_ref[pl.ds(off, tq), :]
        s = lax.dot_general(q, kt, NT_DIMS, preferred_element_type=F32) - c_ref[ki]
        if masked:
            r = lax.broadcasted_iota(jnp.int32, s.shape, 0)
            c = lax.broadcasted_iota(jnp.int32, s.shape, 1)
            s = jnp.where(c <= r, s, NEG_BIG)
        m_new = jnp.maximum(m, jnp.max(s, axis=-1, keepdims=True))
        alpha = jnp.exp(m - m_new)
        p = jnp.exp(s - m_new)
        l = alpha * l + jnp.sum(p, axis=-1, keepdims=True)
        acc = alpha * acc + jnp.dot(p.astype(BF16), vt, preferred_element_type=F32)
        return m_new, l, acc

    init = (jnp.full((tq, 1), NEG_BIG, F32), jnp.zeros((tq, 1), F32), jnp.zeros((tq, HEAD_DIM), F32))
    carry = lax.fori_loop(0, qi, lambda ki, cr: tile(ki, cr, False), init)
    _, l, acc = tile(qi, carry, True)
    o_ref[...] = acc / l


def _fox_prompt(q, kb, vb, c, t):
    tq = ATTN_TILE
    nq = t // tq
    c4 = c.reshape(N_HEADS, nq, 1, tq)
    return pl.pallas_call(
        _fox_prompt_kernel,
        grid=(N_HEADS, nq),
        in_specs=[pl.BlockSpec((tq, HEAD_DIM), lambda h, i: (i, h)),
                  pl.BlockSpec((t, HEAD_DIM), lambda h, i: (0, h)),
                  pl.BlockSpec((t, HEAD_DIM), lambda h, i: (0, h)),
                  pl.BlockSpec((None, nq, 1, tq), lambda h, i: (h, 0, 0, 0))],
        out_specs=pl.BlockSpec((tq, HEAD_DIM), lambda h, i: (i, h)),
        out_shape=jax.ShapeDtypeStruct((t, D_ATTN), F32),
        compiler_params=_params(("parallel", "parallel")),
        name="fox_prompt",
    )(q, kb, vb, c4)


def _fox_sample_kernel(pt_ref, q_ref, kn_ref, vn_ref, lfn_ref, *rest, pg):
    del pt_ref
    k_refs = rest[:pg]
    v_refs = rest[pg:2 * pg]
    lf_refs = rest[2 * pg:3 * pg]
    o_ref = rest[3 * pg]
    m_scr, l_scr, acc_scr, c_scr = rest[3 * pg + 1:]
    g = pl.program_id(1)
    tn = q_ref.shape[0]
    scale = HEAD_DIM ** -0.5

    @pl.when(g == 0)
    def _():
        m_scr[...] = jnp.full(m_scr.shape, NEG_BIG, F32)
        l_scr[...] = jnp.zeros(l_scr.shape, F32)
        acc_scr[...] = jnp.zeros(acc_scr.shape, F32)
        c_scr[...] = jnp.zeros(c_scr.shape, F32)

    r = lax.broadcasted_iota(jnp.int32, (PAGE_SIZE, PAGE_SIZE), 0)
    c = lax.broadcasted_iota(jnp.int32, (PAGE_SIZE, PAGE_SIZE), 1)
    upper = jnp.where(r <= c, 1.0, 0.0).astype(BF16)
    lf = jnp.concatenate([lf_refs[j][...] for j in range(pg)], axis=0)
    hi = lf.astype(BF16).astype(F32)
    rem = lf - hi
    mid = rem.astype(BF16).astype(F32)
    pieces = jnp.concatenate([hi, mid, rem - mid], axis=0).astype(BF16)
    cs = jnp.dot(pieces, upper, preferred_element_type=F32)
    n = pg * N_HEADS
    c_in = cs[:n] + cs[n:2 * n] + cs[2 * n:]
    carry = c_scr[...]
    c_pages = []
    for j in range(pg):
        cp = c_in[j * N_HEADS:(j + 1) * N_HEADS] + carry
        carry = cp[:, PAGE_SIZE - 1:PAGE_SIZE]
        c_pages.append(cp)
    c_scr[...] = carry

    q = (q_ref[...] * scale).astype(BF16)
    s = jnp.concatenate(
        [jnp.concatenate(
            [lax.dot_general(q[:, h * HEAD_DIM:(h + 1) * HEAD_DIM], k_refs[j][_head_rows(h, PAGE_SIZE), :].astype(BF16), NT_DIMS,
                             preferred_element_type=F32) - c_pages[j][h:h + 1, :] for j in range(pg)], axis=1)
         for h in range(N_HEADS)], axis=0)
    m_old = m_scr[...]
    m_new = jnp.maximum(m_old, jnp.max(s, axis=-1, keepdims=True))
    alpha = jnp.exp(m_old - m_new)
    p = jnp.exp(s - m_new)
    l_scr[...] = alpha * l_scr[...] + jnp.sum(p, axis=-1, keepdims=True)
    pvs = []
    for h in range(N_HEADS):
        ph = p[h * tn:(h + 1) * tn].astype(BF16)
        pv = jnp.dot(ph[:, :PAGE_SIZE], v_refs[0][_head_rows(h, PAGE_SIZE), :].astype(BF16),
                     preferred_element_type=F32)
        for j in range(1, pg):
            pv = pv + jnp.dot(ph[:, j * PAGE_SIZE:(j + 1) * PAGE_SIZE],
                              v_refs[j][_head_rows(h, PAGE_SIZE), :].astype(BF16),
                              preferred_element_type=F32)
        pvs.append(pv)
    acc_scr[...] = alpha * acc_scr[...] + jnp.concatenate(pvs, axis=0)
    m_scr[...] = m_new

    @pl.when(g == pl.num_programs(1) - 1)
    def _():
        lfn = lfn_ref[...]
        run = c_scr[...]
        c_new = []
        for j in range(tn):
            run = run + lfn[:, j:j + 1]
            c_new.append(run)
        qf = q_ref[...] * scale
        row = lax.broadcasted_iota(jnp.int32, (tn, 1), 0)
        for h in range(N_HEADS):
            hs = slice(h * HEAD_DIM, (h + 1) * HEAD_DIM)
            rows = slice(h * tn, (h + 1) * tn)
            qh = qf[:, hs]
            kh = kn_ref[_head_rows(h, tn), :]
            vh = vn_ref[_head_rows(h, tn), :]
            cols = []
            for j in range(tn):
                sj = jnp.sum(qh * kh[j:j + 1, :], axis=-1, keepdims=True) - c_new[j][h:h + 1, :]
                cols.append(jnp.where(row >= j, sj, NEG_BIG))
            m_old = m_scr[rows, :]
            m_fin = m_old
            for sj in cols:
                m_fin = jnp.maximum(m_fin, sj)
            alpha = jnp.exp(m_old - m_fin)
            l = alpha * l_scr[rows, :]
            acc = alpha * acc_scr[rows, :]
            for j, sj in enumerate(cols):
                pj = jnp.exp(sj - m_fin)
                l = l + pj
                acc = acc + pj * vh[j:j + 1, :]
            o_ref[:, hs] = acc / l


def _fox_sample(q, k, v, lfn, cache_k, cache_v, cache_lft, page_table, layer, n_phys, t_rows):
    nb, n_pages = page_table.shape
    tn = (q.shape[0] - t_rows) // nb
    pg = min(PAGES_PER_STEP, n_pages)
    row0 = t_rows // tn
    base = layer * n_phys

    def kv_page(j):
        return pl.BlockSpec((None, PAGE_SIZE * N_HEADS, HEAD_DIM),
                            lambda b, g, pt: (base + pt[b, g * pg + j], 0, 0))

    def lf_page(j):
        return pl.BlockSpec((None, N_HEADS, PAGE_SIZE), lambda b, g, pt: (base + pt[b, g * pg + j], 0, 0))

    new_kv = pl.BlockSpec((tn * N_HEADS, HEAD_DIM), lambda b, g, pt: (row0 + b, 0))
    in_specs = [pl.BlockSpec((tn, D_ATTN), lambda b, g, pt: (row0 + b, 0)), new_kv, new_kv,
                pl.BlockSpec((None, N_HEADS, tn), lambda b, g, pt: (b, 0, 0))]
    in_specs += [kv_page(j) for j in range(pg)]
    in_specs += [kv_page(j) for j in range(pg)]
    in_specs += [lf_page(j) for j in range(pg)]
    grid_spec = pltpu.PrefetchScalarGridSpec(
        num_scalar_prefetch=1,
        grid=(nb, n_pages // pg),
        in_specs=in_specs,
        out_specs=pl.BlockSpec((tn, D_ATTN), lambda b, g, pt: (b, 0)),
        scratch_shapes=[pltpu.VMEM((N_HEADS * tn, 1), F32), pltpu.VMEM((N_HEADS * tn, 1), F32),
                        pltpu.VMEM((N_HEADS * tn, HEAD_DIM), F32), pltpu.VMEM((N_HEADS, 1), F32)])
    return pl.pallas_call(
        functools.partial(_fox_sample_kernel, pg=pg),
        grid_spec=grid_spec,
        out_shape=jax.ShapeDtypeStruct((nb * tn, D_ATTN), F32),
        compiler_params=_params(("parallel", "arbitrary")),
        name="fox_sample",
    )(page_table, q, k, v, lfn, *([cache_k] * pg), *([cache_v] * pg), *([cache_lft] * pg))


def _s5_kernel(u_ref, h0re_ref, h0im_ref, are_ref, aim_ref, ldt_ref, bdre_ref, bdim_ref,
               cdre_ref, cdim_ref, d_ref, wglu_ref, *rest, nb, nt):
    o_ref, hre_out, him_out, bbre, bbim, bure, buim, hre, him, abre, abim = rest
    i = pl.program_id(0)

    @pl.when(i == 0)
    def _():
        a_re = are_ref[...]
        a_im = aim_ref[...]
        dt = jnp.exp(ldt_ref[...])
        mag = jnp.exp(dt * a_re)
        ab_re = mag * jnp.cos(dt * a_im)
        ab_im = mag * jnp.sin(dt * a_im)
        den = a_re * a_re + a_im * a_im
        nr = ab_re - 1.0
        coef_re = (nr * a_re + ab_im * a_im) / den
        coef_im = (ab_im * a_re - nr * a_im) / den
        bbre[...] = (coef_re * bdre_ref[...] - coef_im * bdim_ref[...]).astype(BF16)
        bbim[...] = (coef_re * bdim_ref[...] + coef_im * bdre_ref[...]).astype(BF16)
        abre[...] = ab_re
        abim[...] = ab_im
        hre[...] = h0re_ref[...]
        him[...] = h0im_ref[...]

    u = u_ref[...]
    if nb > 1:
        n = nb * nt
        tsh = nt.bit_length() - 1
        bsh = nb.bit_length() - 1
        r = lax.broadcasted_iota(jnp.int32, (n, n), 0)
        c = lax.broadcasted_iota(jnp.int32, (n, n), 1)
        to_time_major = jnp.where(((r & (nb - 1)) == (c >> tsh)) & ((r >> bsh) == (c & (nt - 1))), 1.0, 0.0).astype(F32)
        to_seq_major = jnp.where(((c & (nb - 1)) == (r >> tsh)) & ((c >> bsh) == (r & (nt - 1))), 1.0, 0.0).astype(F32)
        u = jnp.dot(to_time_major, u, precision=HIGHEST, preferred_element_type=F32)
    ub = u.astype(BF16)
    bure[...] = jnp.dot(ub, bbre[...], preferred_element_type=F32)
    buim[...] = jnp.dot(ub, bbim[...], preferred_element_type=F32)

    for cidx in range(D_STATE // SCAN_COLS):
        cs = slice(cidx * SCAN_COLS, (cidx + 1) * SCAN_COLS)
        ar = jnp.broadcast_to(abre[:, cs], (nb, SCAN_COLS))
        ai = jnp.broadcast_to(abim[:, cs], (nb, SCAN_COLS))

        def step(t, carry, cs=cs, ar=ar, ai=ai):
            hr, hi = carry
            rows = pl.ds(pl.multiple_of(t * nb, nb), nb) if nb > 1 else pl.ds(t, 1)
            nr = ar * hr - ai * hi + bure[rows, cs]
            ni = ar * hi + ai * hr + buim[rows, cs]
            bure[rows, cs] = nr
            buim[rows, cs] = ni
            return nr, ni

        hr, hi = lax.fori_loop(0, nt, step, (hre[:, cs], him[:, cs]), unroll=8)
        hre[:, cs] = hr
        him[:, cs] = hi

    y = (jnp.dot(bure[...].astype(BF16), cdre_ref[...], preferred_element_type=F32)
         - jnp.dot(buim[...].astype(BF16), cdim_ref[...], preferred_element_type=F32)
         + d_ref[...] * u)
    gl = 0.5 * y * (1.0 + jnp.tanh(math.sqrt(2.0 / math.pi) * (y + 0.044715 * (y * y * y))))
    gate = jax.nn.sigmoid(jnp.dot(gl.astype(BF16), wglu_ref[...], preferred_element_type=F32))
    out = gl * gate
    if nb > 1:
        out = jnp.dot(to_seq_major, out, precision=HIGHEST, preferred_element_type=F32)
    o_ref[...] = out
    hre_out[...] = hre[...]
    him_out[...] = him[...]


def _s5(u_all, row_block0, nb, nt, n_steps, h0re, h0im, consts):
    rows = nb * nt
    are, aim, ldt, bdre, bdim, cdre, cdim, dvec, wglu = consts
    in_specs = [pl.BlockSpec((rows, D_SSM), lambda i: (row_block0 + i, 0)),
                _const_spec((nb, D_STATE)), _const_spec((nb, D_STATE)),
                _const_spec((1, D_STATE)), _const_spec((1, D_STATE)), _const_spec((1, D_STATE)),
                _const_spec((D_SSM, D_STATE)), _const_spec((D_SSM, D_STATE)),
                _const_spec((D_STATE, D_SSM)), _const_spec((D_STATE, D_SSM)),
                _const_spec((1, D_SSM)), _const_spec((D_SSM, D_SSM))]
    args = [u_all, h0re, h0im, are, aim, ldt, bdre, bdim, cdre, cdim, dvec, wglu]
    state = jax.ShapeDtypeStruct((nb, D_STATE), F32)
    return pl.pallas_call(
        functools.partial(_s5_kernel, nb=nb, nt=nt),
        grid=(n_steps,),
        in_specs=in_specs,
        out_specs=(pl.BlockSpec((rows, D_SSM), lambda i: (i, 0)),
                   pl.BlockSpec((nb, D_STATE), lambda i: (0, 0)),
                   pl.BlockSpec((nb, D_STATE), lambda i: (0, 0))),
        out_shape=(jax.ShapeDtypeStruct((n_steps * rows, D_SSM), F32), state, state),
        scratch_shapes=[pltpu.VMEM((D_SSM, D_STATE), BF16), pltpu.VMEM((D_SSM, D_STATE), BF16),
                        pltpu.VMEM((rows, D_STATE), F32), pltpu.VMEM((rows, D_STATE), F32),
                        pltpu.VMEM((nb, D_STATE), F32), pltpu.VMEM((nb, D_STATE), F32),
                        pltpu.VMEM((1, D_STATE), F32), pltpu.VMEM((1, D_STATE), F32)],
        compiler_params=_params(("arbitrary",)),
        name="s5_sample" if nb > 1 else "s5_prompt",
    )(*args)


def _pool_kernel(cur_ref, *rest, seq_len, prev_rows, pos0, prompt):
    if prompt:
        prev_ref, buf_ref, pw_ref, ps_ref, o_ref = rest
    else:
        buf_ref, pw_ref, ps_ref, o_ref = rest
    i = pl.program_id(0)
    cur = cur_ref[...]
    n = cur.shape[0]
    if prompt:
        prev = jnp.where(i == 0, buf_ref[...], prev_ref[...])
    else:
        prev = buf_ref[...]
    npv = prev.shape[0]
    lsh = seq_len.bit_length() - 1
    psh = prev_rows.bit_length() - 1
    r = lax.broadcasted_iota(jnp.int32, (n, n), 0)
    c = lax.broadcasted_iota(jnp.int32, (n, n), 1)
    same = (r >> lsh) == (c >> lsh)
    d = r - c
    r1 = lax.broadcasted_iota(jnp.int32, (n, npv), 0)
    c1 = lax.broadcasted_iota(jnp.int32, (n, npv), 1)
    same1 = (r1 >> lsh) == (c1 >> psh)
    d1 = (r1 & (seq_len - 1)) + prev_rows - (c1 & (prev_rows - 1))
    local = lax.broadcasted_iota(jnp.int32, (n, 1), 0) & (seq_len - 1)
    pos = pos0 + i * n + local
    outs = []
    for gi, w in enumerate(POOL_WINDOWS):
        gs = slice(gi * POOL_GROUP, (gi + 1) * POOL_GROUP)
        band = jnp.where(same & (d >= 0) & (d < w), 1.0, 0.0).astype(F32)
        band_prev = jnp.where(same1 & (d1 < w), 1.0, 0.0).astype(F32)
        xg = cur[:, gs]
        wsum = (jnp.dot(band, xg, precision=HIGHEST, preferred_element_type=F32)
                + jnp.dot(band_prev, prev[:, gs], precision=HIGHEST, preferred_element_type=F32))
        cnt = jnp.minimum(w, pos + 1).astype(F32)
        pooled = wsum / cnt - xg
        outs.append(jnp.dot(pooled.astype(BF16), pw_ref[gi], preferred_element_type=F32))
    o_ref[...] = jnp.concatenate(outs, axis=1) * ps_ref[...]


def _pool_prompt(p_all, buf_pad, pw, ps, t):
    tt = ROW_TILE
    ratio = tt // POOL_GROUP
    return pl.pallas_call(
        functools.partial(_pool_kernel, seq_len=tt, prev_rows=POOL_GROUP, pos0=0, prompt=True),
        grid=(t // tt,),
        in_specs=[pl.BlockSpec((tt, D_POOL), lambda i: (i, 0)),
                  pl.BlockSpec((POOL_GROUP, D_POOL), lambda i: (jnp.maximum(i * ratio - 1, 0), 0)),
                  _const_spec((POOL_GROUP, D_POOL)), _const_spec(pw.shape), _const_spec((1, D_POOL))],
        out_specs=pl.BlockSpec((tt, D_POOL), lambda i: (i, 0)),
        out_shape=jax.ShapeDtypeStruct((t, D_POOL), F32),
        compiler_params=_params(("parallel",)),
        name="pool_prompt",
    )(p_all, p_all, buf_pad, pw, ps)


def _pool_sample(p_all, buf_pad, pw, ps, t, nb, tn, past):
    rows = nb * tn
    blk = t // rows
    return pl.pallas_call(
        functools.partial(_pool_kernel, seq_len=tn, prev_rows=16, pos0=past, prompt=False),
        grid=(1,),
        in_specs=[pl.BlockSpec((rows, D_POOL), lambda i: (blk, 0)),
                  _const_spec(buf_pad.shape), _const_spec(pw.shape), _const_spec((1, D_POOL))],
        out_specs=pl.BlockSpec((rows, D_POOL), lambda i: (0, 0)),
        out_shape=jax.ShapeDtypeStruct((rows, D_POOL), F32),
        compiler_params=_params(("arbitrary",)),
        name="pool_sample",
    )(p_all, buf_pad, pw, ps)


def _outproj_kernel(x_ref, ap_ref, as_ref, sp_ref, ss_ref, pp_ref, ps_ref, ong_ref, wout_ref, n2_ref,
                    wr_ref, rb_ref, xm_ref, xn_ref, ids_ref, gates_ref, *, prompt_tiles):
    g = ong_ref[...]
    sample = pl.program_id(0) >= prompt_tiles
    ma = _rms(jnp.where(sample, as_ref[...], ap_ref[...]), g[:, :D_ATTN]).astype(BF16)
    ms = _rms(jnp.where(sample, ss_ref[...], sp_ref[...]), g[:, D_ATTN:D_ATTN + D_SSM]).astype(BF16)
    mp = _rms(jnp.where(sample, ps_ref[...], pp_ref[...]), g[:, D_ATTN + D_SSM:]).astype(BF16)
    xm = (x_ref[...]
          + jnp.dot(ma, wout_ref[:D_ATTN, :], preferred_element_type=F32)
          + jnp.dot(ms, wout_ref[D_ATTN:D_ATTN + D_SSM, :], preferred_element_type=F32)
          + jnp.dot(mp, wout_ref[D_ATTN + D_SSM:, :], preferred_element_type=F32))
    xm_ref[...] = xm
    xn = _rms(xm, n2_ref[...])
    xn_ref[...] = xn
    lt = lax.dot_general(wr_ref[...], xn, NT_DIMS, precision=HIGHEST,
                         preferred_element_type=F32) + rb_ref[...]
    ng, ne = N_EXPERT_GROUPS, EXPERTS_PER_GROUP
    gl = [lt[j:j + 1, :] for j in range(ng)]
    gmax = functools.reduce(jnp.maximum, gl)
    gsel = jnp.full(gmax.shape, ng - 1, jnp.int32)
    for j in range(ng - 2, -1, -1):
        gsel = jnp.where(gl[j] == gmax, j, gsel)
    gw = 1.0 / functools.reduce(jnp.add, [jnp.exp(v - gmax) for v in gl])
    es = []
    for j in range(ne):
        v = lt[ng + (ng - 1) * ne + j:ng + (ng - 1) * ne + j + 1, :]
        for gi in range(ng - 2, -1, -1):
            v = jnp.where(gsel == gi, lt[ng + gi * ne + j:ng + gi * ne + j + 1, :], v)
        es.append(v)

    def first_argmax(vals):
        vmax = functools.reduce(jnp.maximum, vals)
        idx = jnp.full(vmax.shape, len(vals) - 1, jnp.int32)
        for j in range(len(vals) - 2, -1, -1):
            idx = jnp.where(vals[j] == vmax, j, idx)
        return vmax, idx

    v1, i1 = first_argmax(es)
    v2, i2 = first_argmax([jnp.where(i1 == j, -jnp.inf, es[j]) for j in range(ne)])
    t = jnp.exp(v2 - v1)
    p1 = 1.0 / (1.0 + t)
    p2 = t / (1.0 + t)
    row = lax.broadcasted_iota(jnp.int32, ids_ref.shape, 0)
    ids_ref[...] = jnp.where(row == 0, gsel * ne + i1, jnp.where(row == 1, gsel * ne + i2, 0))
    gates_ref[...] = jnp.where(row == 0, p1 * gw, jnp.where(row == 1, p2 * gw, 0.0))


def _outproj(x, mix_prompt, mix_sample, ong, wout, n2, wr, rb):
    m, d = x.shape
    tm = ROW_TILE
    pt = mix_prompt[0].shape[0] // tm
    row = lambda w: pl.BlockSpec((tm, w), lambda i: (i, 0))
    prow = lambda w: pl.BlockSpec((tm, w), lambda i: (jnp.minimum(i, pt - 1), 0))
    srow = lambda w: pl.BlockSpec((tm, w), lambda i: (jnp.maximum(i - pt, 0), 0))
    col = pl.BlockSpec((8, tm), lambda i: (0, i))
    return pl.pallas_call(
        functools.partial(_outproj_kernel, prompt_tiles=pt),
        grid=(m // tm,),
        in_specs=[row(d), prow(D_ATTN), srow(D_ATTN), prow(D_SSM), srow(D_SSM), prow(D_POOL), srow(D_POOL),
                  _const_spec((1, d)),
                  _const_spec(wout.shape), _const_spec((1, d)), _const_spec(wr.shape),
                  _const_spec(rb.shape)],
        out_specs=(row(d), row(d), col, col),
        out_shape=(jax.ShapeDtypeStruct((m, d), F32), jax.ShapeDtypeStruct((m, d), F32),
                   jax.ShapeDtypeStruct((8, m), jnp.int32), jax.ShapeDtypeStruct((8, m), F32)),
        compiler_params=_params(("parallel",)),
        name="outproj_router",
    )(x, mix_prompt[0], mix_sample[0], mix_prompt[1], mix_sample[1], mix_prompt[2], mix_sample[2],
      ong, wout, n2, wr, rb)


def _row_copy(src_hbm, row, dst, r, sem):
    return pltpu.make_async_copy(src_hbm.at[pl.ds(row, 1), :], dst.at[pl.ds(r, 1), :], sem)


def _gather_kernel(idx_ref, x_hbm, o_ref, sem):
    tm = o_ref.shape[0]
    base = pl.program_id(0) * tm

    def issue(r, carry):
        _row_copy(x_hbm, idx_ref[base + r], o_ref, r, sem).start()
        return carry

    lax.fori_loop(0, tm, issue, 0)

    def drain(r, carry):
        _row_copy(x_hbm, 0, o_ref, r, sem).wait()
        return carry

    lax.fori_loop(0, tm, drain, 0)


def _gather_rows(src_tok, xn, m_pad):
    tm = ROW_TILE
    d = xn.shape[1]
    grid_spec = pltpu.PrefetchScalarGridSpec(
        num_scalar_prefetch=1,
        grid=(m_pad // tm,),
        in_specs=[pl.BlockSpec(memory_space=pl.ANY)],
        out_specs=pl.BlockSpec((tm, d), lambda i, idx: (i, 0)),
        scratch_shapes=[pltpu.SemaphoreType.DMA(())])
    return pl.pallas_call(
        _gather_kernel,
        grid_spec=grid_spec,
        out_shape=jax.ShapeDtypeStruct((m_pad, d), xn.dtype),
        compiler_params=_params(("arbitrary",)),
        name="moe_gather",
    )(src_tok, xn)


def _expert_kernel(te_ref, tv_ref, xs_ref, gate_ref, wg_ref, wu_ref, wd_ref, o_ref, wgb, wub, wdb):
    t = pl.program_id(0)
    prev = te_ref[jnp.maximum(t - 1, 0)]
    changed = jnp.logical_or(t == 0, te_ref[t] != prev)

    @pl.when(changed)
    def _():
        wgb[...] = wg_ref[...].astype(BF16)
        wub[...] = wu_ref[...].astype(BF16)
        wdb[...] = wd_ref[...].astype(BF16)

    @pl.when(tv_ref[t] == 1)
    def _():
        xb = xs_ref[...].astype(BF16)
        hg = jnp.dot(xb, wgb[...], preferred_element_type=F32)
        hu = jnp.dot(xb, wub[...], preferred_element_type=F32)
        act = (hg * jax.nn.sigmoid(hg)) * hu * gate_ref[...]
        o_ref[...] = jnp.dot(act.astype(BF16), wdb[...], preferred_element_type=F32)

    @pl.when(tv_ref[t] == 0)
    def _():
        o_ref[...] = jnp.zeros(o_ref.shape, o_ref.dtype)


def _expert_ffn(tile_expert, tile_valid, xs, gate_sorted, wg, wu, wd):
    m_pad, d = xs.shape
    tm = ROW_TILE
    grid_spec = pltpu.PrefetchScalarGridSpec(
        num_scalar_prefetch=2,
        grid=(m_pad // tm,),
        in_specs=[pl.BlockSpec((tm, d), lambda t, te, tv: (t, 0)),
                  pl.BlockSpec((tm, 1), lambda t, te, tv: (t, 0)),
                  pl.BlockSpec((None, d, D_EXPERT), lambda t, te, tv: (te[t], 0, 0)),
                  pl.BlockSpec((None, d, D_EXPERT), lambda t, te, tv: (te[t], 0, 0)),
                  pl.BlockSpec((None, D_EXPERT, d), lambda t, te, tv: (te[t], 0, 0))],
        out_specs=pl.BlockSpec((tm, d), lambda t, te, tv: (t, 0)),
        scratch_shapes=[pltpu.VMEM((d, D_EXPERT), BF16), pltpu.VMEM((d, D_EXPERT), BF16),
                        pltpu.VMEM((D_EXPERT, d), BF16)])
    return pl.pallas_call(
        _expert_kernel,
        grid_spec=grid_spec,
        out_shape=jax.ShapeDtypeStruct((m_pad, d), F32),
        compiler_params=_params(("arbitrary",)),
        name="expert_ffn",
    )(tile_expert, tile_valid, xs, gate_sorted, wg, wu, wd)


def _combine_kernel(d1_ref, d2_ref, xm_ref, ys_hbm, o_ref, buf1, buf2, sem):
    tm = o_ref.shape[0]
    base = pl.program_id(0) * tm

    def issue(r, carry):
        _row_copy(ys_hbm, d1_ref[base + r], buf1, r, sem).start()
        _row_copy(ys_hbm, d2_ref[base + r], buf2, r, sem).start()
        return carry

    lax.fori_loop(0, tm, issue, 0)

    def drain(r, carry):
        _row_copy(ys_hbm, 0, buf1, r, sem).wait()
        _row_copy(ys_hbm, 0, buf2, r, sem).wait()
        return carry

    lax.fori_loop(0, tm, drain, 0)
    o_ref[...] = xm_ref[...] + buf1[...] + buf2[...]


def _combine(dest1, dest2, xm, ys):
    m, d = xm.shape
    tm = ROW_TILE
    grid_spec = pltpu.PrefetchScalarGridSpec(
        num_scalar_prefetch=2,
        grid=(m // tm,),
        in_specs=[pl.BlockSpec((tm, d), lambda i, a, b: (i, 0)), pl.BlockSpec(memory_space=pl.ANY)],
        out_specs=pl.BlockSpec((tm, d), lambda i, a, b: (i, 0)),
        scratch_shapes=[pltpu.VMEM((tm, d), F32), pltpu.VMEM((tm, d), F32),
                        pltpu.SemaphoreType.DMA(())])
    return pl.pallas_call(
        _combine_kernel,
        grid_spec=grid_spec,
        out_shape=jax.ShapeDtypeStruct((m, d), F32),
        compiler_params=_params(("arbitrary",)),
        name="moe_combine",
    )(dest1, dest2, xm, ys)


def _dispatch_plan(ids, gates, m):
    tm = ROW_TILE
    n_flat = 2 * m
    n_tiles = n_flat // tm + N_EXPERTS
    m_pad = n_tiles * tm
    e_flat = ids[:2].reshape(n_flat)
    g_flat = gates[:2].reshape(n_flat)
    onehot = (e_flat[:, None] == jnp.arange(N_EXPERTS, dtype=jnp.int32)[None, :]).astype(jnp.int32)
    csum = jnp.cumsum(onehot, axis=0)
    counts = csum[-1]
    rank = jnp.sum(csum * onehot, axis=1) - 1
    tiles_per = (counts + tm - 1) // tm
    tile_end = jnp.cumsum(tiles_per)
    tile_start = tile_end - tiles_per
    slot0 = tile_start * tm
    dest = slot0[e_flat] + rank
    row_start = jnp.cumsum(counts) - counts
    order = jnp.argsort(e_flat, stable=True).astype(jnp.int32)
    tiles = jnp.arange(n_tiles, dtype=jnp.int32)
    te_raw = jnp.sum((tiles[:, None] >= tile_end[None, :]).astype(jnp.int32), axis=1)
    tile_valid = (tiles < tile_end[-1]).astype(jnp.int32)
    last_e = jnp.sum((tile_end[-1] - 1 >= tile_end).astype(jnp.int32))
    tile_expert = jnp.where(tile_valid == 1, te_raw, last_e).astype(jnp.int32)
    slots = jnp.arange(m_pad, dtype=jnp.int32)
    s_e = tile_expert[slots // tm]
    within = slots - slot0[s_e]
    s_valid = (within < counts[s_e]) & (tile_valid[slots // tm] == 1)
    src_flat = order[jnp.clip(row_start[s_e] + within, 0, n_flat - 1)]
    src_tok = jnp.where(s_valid, src_flat % m, 0).astype(jnp.int32)
    gate_sorted = jnp.where(s_valid, g_flat[src_flat], 0.0).reshape(m_pad, 1)
    return src_tok, gate_sorted, tile_expert, tile_valid, dest[:m].astype(jnp.int32), dest[m:].astype(jnp.int32), m_pad


def _block_diag(w):
    g, r, c = w.shape
    eye = jnp.eye(g, dtype=w.dtype)
    return (eye[:, None, :, None] * w[:, :, None, :]).reshape(g * r, g * c)


def kernel(x_prompt, x_sample, cache_k, cache_v, cache_logf, page_table, state_ssm_re, state_ssm_im, state_pool, norm1_g, w_in, f_bias, q_gain, k_gain, ssm_a_re, ssm_a_im, ssm_log_dt, ssm_b_re, ssm_b_im, ssm_c_re, ssm_c_im, ssm_d, ssm_w_glu, pool_w, pool_scale, out_norm_g, w_out, norm2_g, router_group_w, router_group_b, router_expert_w, router_expert_b, moe_w_gate, moe_w_up, moe_w_down):
    n_prompt, t, d = x_prompt.shape
    nb, tn, _ = x_sample.shape
    depth = w_in.shape[0]
    n_phys = cache_k.shape[1]
    past = page_table.shape[1] * cache_k.shape[2]
    assert n_prompt == 1
    ms = nb * tn
    m = t + ms
    x = jnp.concatenate([x_prompt.reshape(t, d), x_sample.reshape(ms, d)], axis=0)

    ck = cache_k.reshape(depth * n_phys, PAGE_SIZE * N_HEADS, HEAD_DIM)
    cv = cache_v.reshape(depth * n_phys, PAGE_SIZE * N_HEADS, HEAD_DIM)
    clf = jnp.swapaxes(cache_logf, 2, 3).reshape(depth * n_phys, N_HEADS, PAGE_SIZE)
    zero_state = jnp.zeros((1, D_STATE), F32)
    zero_buf = jnp.zeros((POOL_GROUP, D_POOL), F32)

    outs_p, outs_s = [], []
    for l in range(depth):
        wl = w_in[l]
        wqkv = wl[:, :3 * D_ATTN].astype(BF16)
        wup = wl[:, 3 * D_ATTN + N_HEADS:].astype(BF16)
        wft = jnp.pad(wl[:, 3 * D_ATTN:3 * D_ATTN + N_HEADS].T, ((0, 8), (0, 0))).astype(BF16)
        q, k, v, kb, vb, lft, u, p_in = _proj(
            x, norm1_g[l].reshape(1, d), wqkv, wup, wft, f_bias[l].reshape(N_HEADS, 1),
            q_gain[l].reshape(1, HEAD_DIM), k_gain[l].reshape(1, HEAD_DIM))

        c_prompt = _prompt_cumsum(lft, t)
        a_p = _fox_prompt(q, kb, vb, c_prompt, t)
        lfn = lft[:, t:].reshape(N_HEADS, nb, tn).transpose(1, 0, 2)
        a_s = _fox_sample(q, k, v, lfn, ck, cv, clf, page_table, l, n_phys, t)

        consts = (ssm_a_re[l].reshape(1, D_STATE), ssm_a_im[l].reshape(1, D_STATE),
                  jnp.repeat(ssm_log_dt[l], SSM_STATE).reshape(1, D_STATE),
                  _block_diag(jnp.swapaxes(ssm_b_re[l], 1, 2)), _block_diag(jnp.swapaxes(ssm_b_im[l], 1, 2)),
                  _block_diag(jnp.swapaxes(ssm_c_re[l], 1, 2)).astype(BF16),
                  _block_diag(jnp.swapaxes(ssm_c_im[l], 1, 2)).astype(BF16),
                  ssm_d[l].reshape(1, D_SSM), ssm_w_glu[l].astype(BF16))
        s_p, hre_p, him_p = _s5(u, 0, 1, ROW_TILE, t // ROW_TILE, zero_state, zero_state, consts)
        s_s, hre_s, him_s = _s5(u, t // ms, nb, tn, 1, state_ssm_re[l].reshape(nb, D_STATE),
                                state_ssm_im[l].reshape(nb, D_STATE), consts)

        pw = pool_w[l].astype(BF16)
        ps = pool_scale[l].reshape(1, D_POOL)
        po_p = _pool_prompt(p_in, zero_buf, pw, ps, t)
        buf_s = jnp.pad(state_pool[l], ((0, 0), (1, 0), (0, 0))).reshape(nb * 16, D_POOL)
        po_s = _pool_sample(p_in, buf_s, pw, ps, t, nb, tn, past)

        wr = jnp.pad(jnp.concatenate([router_group_w[l], router_expert_w[l]], axis=1).T, ((0, 4), (0, 0)))
        rb = jnp.pad(jnp.concatenate([router_group_b[l], router_expert_b[l]]), (0, 4)).reshape(24, 1)
        xm, xn, ids, gates = _outproj(x, (a_p, s_p, po_p), (a_s, s_s, po_s), out_norm_g[l].reshape(1, d),
                                      w_out[l].astype(BF16), norm2_g[l].reshape(1, d), wr, rb)

        src_tok, gate_sorted, tile_expert, tile_valid, dest1, dest2, m_pad = _dispatch_plan(ids, gates, m)
        xs = _gather_rows(src_tok, xn, m_pad)
        ys = _expert_ffn(tile_expert, tile_valid, xs, gate_sorted, moe_w_gate[l], moe_w_up[l], moe_w_down[l])
        x = _combine(dest1, dest2, xm, ys)

        lf_rows = lft.T
        th = t * N_HEADS
        outs_p.append((k[:th].reshape(1, t, N_HEADS, HEAD_DIM), v[:th].reshape(1, t, N_HEADS, HEAD_DIM),
                       lf_rows[:t].reshape(1, t, N_HEADS),
                       hre_p.reshape(1, SSM_GROUPS, SSM_STATE), him_p.reshape(1, SSM_GROUPS, SSM_STATE),
                       p_in[t - POOL_BUF:t].reshape(1, POOL_BUF, D_POOL)))
        p_s = p_in[t:].reshape(nb, tn, D_POOL)
        outs_s.append((k[th:].reshape(nb, tn, N_HEADS, HEAD_DIM), v[th:].reshape(nb, tn, N_HEADS, HEAD_DIM),
                       lf_rows[t:].reshape(nb, tn, N_HEADS),
                       hre_s.reshape(nb, SSM_GROUPS, SSM_STATE), him_s.reshape(nb, SSM_GROUPS, SSM_STATE),
                       jnp.concatenate([state_pool[l], p_s], axis=1)[:, -POOL_BUF:]))

    stack = lambda outs, i: jnp.stack([o[i] for o in outs], axis=0)
    return (x[:t].reshape(1, t, d), x[t:].reshape(nb, tn, d),
            *[stack(outs_p, i) for i in range(6)], *[stack(outs_s, i) for i in range(6)])
```

```python
import functools
import math

import jax
import jax.numpy as jnp
from jax import lax
from jax.experimental import pallas as pl
from jax.experimental.pallas import tpu as pltpu

F32 = jnp.float32
BF16 = jnp.bfloat16
HIGHEST = lax.Precision.HIGHEST
NT_DIMS = (((1,), (1,)), ((), ()))

EPS = 1e-6
HEAD_DIM = 128
N_HEADS = 8
D_ATTN = N_HEADS * HEAD_DIM
D_SSM = 512
SSM_GROUPS = 32
SSM_GROUP = 16
SSM_STATE = 64
D_STATE = SSM_GROUPS * SSM_STATE
D_POOL = 512
POOL_WINDOWS = (2, 4, 8, 16)
POOL_GROUP = 128
POOL_BUF = 15
N_EXPERT_GROUPS = 4
EXPERTS_PER_GROUP = 4
N_EXPERTS = 16
D_EXPERT = 512
PAGE_SIZE = 128
NEG_BIG = -1e30
SKIP_MARGIN = 30.0

ROW_TILE = 256
ROW_CHUNK = 128
ROW_CHUNKS = 2048 // ROW_CHUNK
ATTN_TILE = 512
PAGES_PER_STEP = 8
SCAN_COLS = 512
SCAN_SEGMENTS = 8
VMEM_LIMIT = 56 * 1024 * 1024


def _params(sem, vmem=VMEM_LIMIT):
    return pltpu.CompilerParams(dimension_semantics=sem, vmem_limit_bytes=vmem)


def _const_spec(shape):
    nd = len(shape)
    return pl.BlockSpec(shape, lambda *_: (0,) * nd, pipeline_mode=pl.Buffered(1))


def _rms(y, g):
    return y * lax.rsqrt(jnp.mean(y * y, axis=-1, keepdims=True) + EPS) * g


def _dot01(mat01, x, pieces):
    m = mat01.astype(BF16)
    acc = None
    rem = x
    for _ in range(pieces):
        part = rem.astype(BF16)
        rem = rem - part.astype(F32)
        d = jnp.dot(m, part, preferred_element_type=F32)
        acc = d if acc is None else acc + d
    return acc


def _chunk_rows(j, n):
    return pl.ds(j, n, stride=ROW_CHUNKS)


def _head_rows(h, n):
    return pl.ds(h, n, stride=N_HEADS)


def _proj_kernel(x_ref, g1_ref, wqkv_ref, wup_ref, wft_ref, fb_ref, qg_ref, kg_ref,
                 q_ref, k_ref, v_ref, kb_ref, vb_ref, lf_ref, u_ref, p_ref):
    hb = _rms(x_ref[...], g1_ref[...]).astype(BF16)
    qg = qg_ref[...]
    kg = kg_ref[...]
    for c in range(D_ATTN // 256):
        lo = c * 256
        qc = jnp.dot(hb, wqkv_ref[:, lo:lo + 256], preferred_element_type=F32)
        kc = jnp.dot(hb, wqkv_ref[:, D_ATTN + lo:D_ATTN + lo + 256], preferred_element_type=F32)
        vc = jnp.dot(hb, wqkv_ref[:, 2 * D_ATTN + lo:2 * D_ATTN + lo + 256], preferred_element_type=F32)
        for j in range(2):
            a = lo + j * HEAD_DIM
            q_ref[:, a:a + HEAD_DIM] = _rms(qc[:, j * HEAD_DIM:(j + 1) * HEAD_DIM], qg)
            kn = _rms(kc[:, j * HEAD_DIM:(j + 1) * HEAD_DIM], kg)
            k_ref[_head_rows(2 * c + j, kn.shape[0]), :] = kn
            kb_ref[:, a:a + HEAD_DIM] = kn.astype(BF16)
            v_ref[_head_rows(2 * c + j, kn.shape[0]), :] = vc[:, j * HEAD_DIM:(j + 1) * HEAD_DIM]
        vb_ref[:, lo:lo + 256] = vc.astype(BF16)
    u_ref[...] = jnp.dot(hb, wup_ref[:, :D_SSM], preferred_element_type=F32)
    p_ref[...] = jnp.dot(hb, wup_ref[:, D_SSM:], preferred_element_type=F32)
    ft = lax.dot_general(wft_ref[...], hb, NT_DIMS, preferred_element_type=F32)[:N_HEADS]
    z = ft + fb_ref[...]
    lf_ref[...] = jnp.minimum(z, 0.0) - jnp.log1p(jnp.exp(-jnp.abs(z)))


def _proj(x, g1, wqkv, wup, wft, fb, qg, kg):
    m, d = x.shape
    tm = ROW_TILE
    row = lambda w: pl.BlockSpec((tm, w), lambda i: (i, 0))
    heads = pl.BlockSpec((tm * N_HEADS, HEAD_DIM), lambda i: (i, 0))
    outs = (
        jax.ShapeDtypeStruct((m, D_ATTN), F32), jax.ShapeDtypeStruct((m * N_HEADS, HEAD_DIM), F32),
        jax.ShapeDtypeStruct((m * N_HEADS, HEAD_DIM), F32), jax.ShapeDtypeStruct((m, D_ATTN), BF16),
        jax.ShapeDtypeStruct((m, D_ATTN), BF16), jax.ShapeDtypeStruct((N_HEADS, m), F32),
        jax.ShapeDtypeStruct((m, D_SSM), F32), jax.ShapeDtypeStruct((m, D_POOL), F32))
    return pl.pallas_call(
        _proj_kernel,
        grid=(m // tm,),
        in_specs=[row(d), _const_spec((1, d)), _const_spec(wqkv.shape), _const_spec(wup.shape),
                  _const_spec(wft.shape), _const_spec((N_HEADS, 1)), _const_spec((1, HEAD_DIM)),
                  _const_spec((1, HEAD_DIM))],
        out_specs=(row(D_ATTN), heads, heads, row(D_ATTN), row(D_ATTN),
                   pl.BlockSpec((N_HEADS, tm), lambda i: (0, i)), row(D_SSM), row(D_POOL)),
        out_shape=outs,
        compiler_params=_params(("parallel",)),
        name="proj",
    )(x, g1, wqkv, wup, wft, fb, qg, kg)


def _cumsum_kernel(lf_ref, c_ref):
    t = c_ref.shape[1]
    w = ATTN_TILE
    r = lax.broadcasted_iota(jnp.int32, (w, w), 0)
    c = lax.broadcasted_iota(jnp.int32, (w, w), 1)
    upper = jnp.where(r <= c, 1.0, 0.0).astype(F32)
    carry = jnp.zeros((N_HEADS, 1), F32)
    for s in range(t // w):
        blk = jnp.dot(lf_ref[:, s * w:(s + 1) * w], upper, precision=HIGHEST,
                      preferred_element_type=F32) + carry
        c_ref[:, s * w:(s + 1) * w] = blk
        carry = blk[:, w - 1:w]


def _prompt_cumsum(lft, t):
    return pl.pallas_call(
        _cumsum_kernel,
        grid=(1,),
        in_specs=[pl.BlockSpec((N_HEADS, t), lambda i: (0, 0))],
        out_specs=pl.BlockSpec((N_HEADS, t), lambda i: (0, 0)),
        out_shape=jax.ShapeDtypeStruct((N_HEADS, t), F32),
        compiler_params=_params(("arbitrary",)),
        name="prompt_cumsum",
    )(lft)


def _fox_prompt_kernel(kstart_ref, q_ref, k_ref, v_ref, c_ref, o_ref):
    qi = pl.program_id(1)
    first_tile = kstart_ref[pl.program_id(0), qi]
    tq = q_ref.shape[0]
    q = (q_ref[...] * (HEAD_DIM ** -0.5)).astype(BF16)

    def tile(ki, carry, masked):
        m, l, acc = carry
        off = pl.multiple_of(ki * tq, tq)
        kt = k_ref[pl.ds(off, tq), :]
        vt = v_ref[pl.ds(off, tq), :]
        s = lax.dot_general(q, kt, NT_DIMS, preferred_element_type=F32) - c_ref[ki]
        if masked:
            r = lax.broadcasted_iota(jnp.int32, s.shape, 0)
            c = lax.broadcasted_iota(jnp.int32, s.shape, 1)
            s = jnp.where(c <= r, s, NEG_BIG)
        m_new = jnp.maximum(m, jnp.max(s, axis=-1, keepdims=True))
        alpha = jnp.exp(m - m_new)
        p = jnp.exp(s - m_new)
        l = alpha * l + jnp.sum(p, axis=-1, keepdims=True)
        acc = alpha * acc + jnp.dot(p.astype(BF16), vt, preferred_element_type=F32)
        return m_new, l, acc

    init = (jnp.full((tq, 1), NEG_BIG, F32), jnp.zeros((tq, 1), F32), jnp.zeros((tq, HEAD_DIM), F32))
    carry = lax.fori_loop(first_tile, qi, lambda ki, cr: tile(ki, cr, False), init)
    _, l, acc = tile(qi, carry, True)
    o_ref[...] = acc / l


def _negligible_key_tiles(c, qk_bound, tq):
    c_first = c[:, ::tq]
    c_last = c[:, tq - 1::tq]
    nq = c_first.shape[1]
    gap = c_first[:, :, None] - c_last[:, None, :]
    earlier = jnp.arange(nq)[None, :] < jnp.arange(nq)[:, None]
    skip = (gap < -(2.0 * qk_bound + SKIP_MARGIN)) & earlier[None]
    return jnp.sum(skip.astype(jnp.int32), axis=2)


def _fox_prompt(q, kb, vb, c, t, qk_bound):
    tq = ATTN_TILE
    nq = t // tq
    c4 = c.reshape(N_HEADS, nq, 1, tq)
    kstart = _negligible_key_tiles(c, qk_bound, tq)
    grid_spec = pltpu.PrefetchScalarGridSpec(
        num_scalar_prefetch=1,
        grid=(N_HEADS, nq),
        in_specs=[pl.BlockSpec((tq, HEAD_DIM), lambda h, i, ks: (i, h)),
                  pl.BlockSpec((t, HEAD_DIM), lambda h, i, ks: (0, h)),
                  pl.BlockSpec((t, HEAD_DIM), lambda h, i, ks: (0, h)),
                  pl.BlockSpec((None, nq, 1, tq), lambda h, i, ks: (h, 0, 0, 0))],
        out_specs=pl.BlockSpec((tq, HEAD_DIM), lambda h, i, ks: (i, h)))
    return pl.pallas_call(
        _fox_prompt_kernel,
        grid_spec=grid_spec,
        out_shape=jax.ShapeDtypeStruct((t, D_ATTN), F32),
        compiler_params=_params(("parallel", "parallel")),
        name="fox_prompt",
    )(kstart, q, kb, vb, c4)


def _fox_sample_kernel(pt_ref, q_ref, kn_ref, vn_ref, lfn_ref, *rest, pg):
    del pt_ref
    k_refs = rest[:pg]
    v_refs = rest[pg:2 * pg]
    lf_refs = rest[2 * pg:3 * pg]
    o_ref = rest[3 * pg]
    m_scr, l_scr, acc_scr, c_scr = rest[3 * pg + 1:]
    g = pl.program_id(1)
    tn = q_ref.shape[0]
    scale = HEAD_DIM ** -0.5

    @pl.when(g == 0)
    def _():
        m_scr[...] = jnp.full(m_scr.shape, NEG_BIG, F32)
        l_scr[...] = jnp.zeros(l_scr.shape, F32)
        acc_scr[...] = jnp.zeros(acc_scr.shape, F32)
        c_scr[...] = jnp.zeros(c_scr.shape, F32)

    r = lax.broadcasted_iota(jnp.int32, (PAGE_SIZE, PAGE_SIZE), 0)
    c = lax.broadcasted_iota(jnp.int32, (PAGE_SIZE, PAGE_SIZE), 1)
    upper = jnp.where(r <= c, 1.0, 0.0).astype(BF16)
    lf = jnp.concatenate([lf_refs[j][...] for j in range(pg)], axis=0)
    hi = lf.astype(BF16).astype(F32)
    rem = lf - hi
    mid = rem.astype(BF16).astype(F32)
    pieces = jnp.concatenate([hi, mid, rem - mid], axis=0).astype(BF16)
    cs = jnp.dot(pieces, upper, preferred_element_type=F32)
    n = pg * N_HEADS
    c_in = cs[:n] + cs[n:2 * n] + cs[2 * n:]
    carry = c_scr[...]
    c_pages = []
    for j in range(pg):
        cp = c_in[j * N_HEADS:(j + 1) * N_HEADS] + carry
        carry = cp[:, PAGE_SIZE - 1:PAGE_SIZE]
        c_pages.append(cp)
    c_scr[...] = carry

    q = (q_ref[...] * scale).astype(BF16)
    s = jnp.concatenate(
        [jnp.concatenate(
            [lax.dot_general(q[:, h * HEAD_DIM:(h + 1) * HEAD_DIM], k_refs[j][_head_rows(h, PAGE_SIZE), :].astype(BF16), NT_DIMS,
                             preferred_element_type=F32) - c_pages[j][h:h + 1, :] for j in range(pg)], axis=1)
         for h in range(N_HEADS)], axis=0)
    m_old = m_scr[...]
    m_new = jnp.maximum(m_old, jnp.max(s, axis=-1, keepdims=True))
    alpha = jnp.exp(m_old - m_new)
    p = jnp.exp(s - m_new)
    l_scr[...] = alpha * l_scr[...] + jnp.sum(p, axis=-1, keepdims=True)
    pvs = []
    for h in range(N_HEADS):
        ph = p[h * tn:(h + 1) * tn].astype(BF16)
        pv = jnp.dot(ph[:, :PAGE_SIZE], v_refs[0][_head_rows(h, PAGE_SIZE), :].astype(BF16),
                     preferred_element_type=F32)
        for j in range(1, pg):
            pv = pv + jnp.dot(ph[:, j * PAGE_SIZE:(j + 1) * PAGE_SIZE],
                              v_refs[j][_head_rows(h, PAGE_SIZE), :].astype(BF16),
                              preferred_element_type=F32)
        pvs.append(pv)
    acc_scr[...] = alpha * acc_scr[...] + jnp.concatenate(pvs, axis=0)
    m_scr[...] = m_new

    @pl.when(g == pl.num_programs(1) - 1)
    def _():
        lfn = lfn_ref[...]
        run = c_scr[...]
        c_new = []
        for j in range(tn):
            run = run + lfn[:, j:j + 1]
            c_new.append(run)
        qf = q_ref[...] * scale
        row = lax.broadcasted_iota(jnp.int32, (tn, 1), 0)
        for h in range(N_HEADS):
            hs = slice(h * HEAD_DIM, (h + 1) * HEAD_DIM)
            rows = slice(h * tn, (h + 1) * tn)
            qh = qf[:, hs]
            kh = kn_ref[_head_rows(h, tn), :]
            vh = vn_ref[_head_rows(h, tn), :]
            cols = []
            for j in range(tn):
                sj = jnp.sum(qh * kh[j:j + 1, :], axis=-1, keepdims=True) - c_new[j][h:h + 1, :]
                cols.append(jnp.where(row >= j, sj, NEG_BIG))
            m_old = m_scr[rows, :]
            m_fin = m_old
            for sj in cols:
                m_fin = jnp.maximum(m_fin, sj)
            alpha = jnp.exp(m_old - m_fin)
            l = alpha * l_scr[rows, :]
            acc = alpha * acc_scr[rows, :]
            for j, sj in enumerate(cols):
                pj = jnp.exp(sj - m_fin)
                l = l + pj
                acc = acc + pj * vh[j:j + 1, :]
            o_ref[:, hs] = acc / l


def _fox_sample(q, k, v, lfn, cache_k, cache_v, cache_lft, page_table, layer, n_phys, t_rows):
    nb, n_pages = page_table.shape
    tn = (q.shape[0] - t_rows) // nb
    pg = min(PAGES_PER_STEP, n_pages)
    row0 = t_rows // tn
    base = layer * n_phys

    def kv_page(j):
        return pl.BlockSpec((None, PAGE_SIZE * N_HEADS, HEAD_DIM),
                            lambda b, g, pt: (base + pt[b, g * pg + j], 0, 0))

    def lf_page(j):
        return pl.BlockSpec((None, N_HEADS, PAGE_SIZE), lambda b, g, pt: (base + pt[b, g * pg + j], 0, 0))

    new_kv = pl.BlockSpec((tn * N_HEADS, HEAD_DIM), lambda b, g, pt: (row0 + b, 0))
    in_specs = [pl.BlockSpec((tn, D_ATTN), lambda b, g, pt: (row0 + b, 0)), new_kv, new_kv,
                pl.BlockSpec((None, N_HEADS, tn), lambda b, g, pt: (b, 0, 0))]
    in_specs += [kv_page(j) for j in range(pg)]
    in_specs += [kv_page(j) for j in range(pg)]
    in_specs += [lf_page(j) for j in range(pg)]
    grid_spec = pltpu.PrefetchScalarGridSpec(
        num_scalar_prefetch=1,
        grid=(nb, n_pages // pg),
        in_specs=in_specs,
        out_specs=pl.BlockSpec((tn, D_ATTN), lambda b, g, pt: (b, 0)),
        scratch_shapes=[pltpu.VMEM((N_HEADS * tn, 1), F32), pltpu.VMEM((N_HEADS * tn, 1), F32),
                        pltpu.VMEM((N_HEADS * tn, HEAD_DIM), F32), pltpu.VMEM((N_HEADS, 1), F32)])
    return pl.pallas_call(
        functools.partial(_fox_sample_kernel, pg=pg),
        grid_spec=grid_spec,
        out_shape=jax.ShapeDtypeStruct((nb * tn, D_ATTN), F32),
        compiler_params=_params(("parallel", "arbitrary")),
        name="fox_sample",
    )(page_table, q, k, v, lfn, *([cache_k] * pg), *([cache_v] * pg), *([cache_lft] * pg))


def _s5_kernel(u_ref, h0re_ref, h0im_ref, are_ref, aim_ref, ldt_ref, bdre_ref, bdim_ref,
               cdre_ref, cdim_ref, d_ref, wglu_ref, *rest, nb, nt, chained):
    o_ref, hre_out, him_out, bbre, bbim, bure, buim, hre, him, abre, abim, pwre, pwim = rest
    i = pl.program_id(0)

    @pl.when(i == 0)
    def _():
        a_re = are_ref[...]
        a_im = aim_ref[...]
        dt = jnp.exp(ldt_ref[...])
        mag = jnp.exp(dt * a_re)
        ab_re = mag * jnp.cos(dt * a_im)
        ab_im = mag * jnp.sin(dt * a_im)
        den = a_re * a_re + a_im * a_im
        nr = ab_re - 1.0
        coef_re = (nr * a_re + ab_im * a_im) / den
        coef_im = (ab_im * a_re - nr * a_im) / den
        bbre[...] = (coef_re * bdre_ref[...] - coef_im * bdim_ref[...]).astype(BF16)
        bbim[...] = (coef_re * bdim_ref[...] + coef_im * bdre_ref[...]).astype(BF16)
        abre[...] = ab_re
        abim[...] = ab_im
        hre[...] = h0re_ref[...]
        him[...] = h0im_ref[...]
        if chained:
            def power(j, carry):
                pr, pi = carry
                pwre[pl.ds(j, 1), :] = pr
                pwim[pl.ds(j, 1), :] = pi
                return ab_re * pr - ab_im * pi, ab_re * pi + ab_im * pr

            lax.fori_loop(0, nt, power, (ab_re, ab_im))

    n = nb * nt
    tsh = nt.bit_length() - 1
    bsh = nb.bit_length() - 1
    r = lax.broadcasted_iota(jnp.int32, (n, n), 0)
    c = lax.broadcasted_iota(jnp.int32, (n, n), 1)
    to_time_major = jnp.where(((r & (nb - 1)) == (c >> tsh)) & ((r >> bsh) == (c & (nt - 1))), 1.0, 0.0).astype(F32)
    to_seq_major = jnp.where(((c & (nb - 1)) == (r >> tsh)) & ((c >> bsh) == (r & (nt - 1))), 1.0, 0.0).astype(F32)
    u = _dot01(to_time_major, u_ref[...], 3)
    ub = u.astype(BF16)
    bure[...] = jnp.dot(ub, bbre[...], preferred_element_type=F32)
    buim[...] = jnp.dot(ub, bbim[...], preferred_element_type=F32)

    for cidx in range(D_STATE // SCAN_COLS):
        cs = slice(cidx * SCAN_COLS, (cidx + 1) * SCAN_COLS)
        ar = jnp.broadcast_to(abre[:, cs], (nb, SCAN_COLS))
        ai = jnp.broadcast_to(abim[:, cs], (nb, SCAN_COLS))

        def step(t, carry, cs=cs, ar=ar, ai=ai):
            hr, hi = carry
            rows = pl.ds(pl.multiple_of(t * nb, nb), nb)
            nr = ar * hr - ai * hi + bure[rows, cs]
            ni = ar * hi + ai * hr + buim[rows, cs]
            bure[rows, cs] = nr
            buim[rows, cs] = ni
            return nr, ni

        if not chained:
            hr, hi = lax.fori_loop(0, nt, step, (hre[:, cs], him[:, cs]), unroll=8)
            hre[:, cs] = hr
            him[:, cs] = hi
            continue

        zeros = jnp.zeros((nb, SCAN_COLS), F32)
        fr, fi = lax.fori_loop(0, nt, step, (zeros, zeros), unroll=8)
        ptr = pwre[nt - 1:nt, cs]
        pti = pwim[nt - 1:nt, cs]
        cr = hre[:, cs]
        ci = him[:, cs]
        seg = lax.broadcasted_iota(jnp.int32, (nb, SCAN_COLS), 0)
        inr, ini = zeros, zeros
        for s in range(nb):
            inr = jnp.where(seg == s, cr, inr)
            ini = jnp.where(seg == s, ci, ini)
            cr, ci = fr[s:s + 1, :] + ptr * cr - pti * ci, fi[s:s + 1, :] + ptr * ci + pti * cr
        hre[:, cs] = cr
        him[:, cs] = ci

        def fix(t, carry, cs=cs, inr=inr, ini=ini):
            rows = pl.ds(pl.multiple_of(t * nb, nb), nb)
            pr = pwre[pl.ds(t, 1), cs]
            pi = pwim[pl.ds(t, 1), cs]
            bure[rows, cs] += pr * inr - pi * ini
            buim[rows, cs] += pr * ini + pi * inr
            return carry

        lax.fori_loop(0, nt, fix, 0, unroll=8)

    y = (jnp.dot(bure[...].astype(BF16), cdre_ref[...], preferred_element_type=F32)
         - jnp.dot(buim[...].astype(BF16), cdim_ref[...], preferred_element_type=F32)
         + d_ref[...] * u)
    gl = 0.5 * y * (1.0 + jnp.tanh(math.sqrt(2.0 / math.pi) * (y + 0.044715 * (y * y * y))))
    gate = jax.nn.sigmoid(jnp.dot(gl.astype(BF16), wglu_ref[...], preferred_element_type=F32))
    o_ref[...] = _dot01(to_seq_major, gl * gate, 3)
    hre_out[...] = hre[...]
    him_out[...] = him[...]


def _s5(u_all, row_block0, nb, nt, n_steps, h0re, h0im, consts, chained):
    rows = nb * nt
    ns = 1 if chained else nb
    are, aim, ldt, bdre, bdim, cdre, cdim, dvec, wglu = consts
    in_specs = [pl.BlockSpec((rows, D_SSM), lambda i: (row_block0 + i, 0)),
                _const_spec((ns, D_STATE)), _const_spec((ns, D_STATE)),
                _const_spec((1, D_STATE)), _const_spec((1, D_STATE)), _const_spec((1, D_STATE)),
                _const_spec((D_SSM, D_STATE)), _const_spec((D_SSM, D_STATE)),
                _const_spec((D_STATE, D_SSM)), _const_spec((D_STATE, D_SSM)),
                _const_spec((1, D_SSM)), _const_spec((D_SSM, D_SSM))]
    args = [u_all, h0re, h0im, are, aim, ldt, bdre, bdim, cdre, cdim, dvec, wglu]
    state = jax.ShapeDtypeStruct((ns, D_STATE), F32)
    return pl.pallas_call(
        functools.partial(_s5_kernel, nb=nb, nt=nt, chained=chained),
        grid=(n_steps,),
        in_specs=in_specs,
        out_specs=(pl.BlockSpec((rows, D_SSM), lambda i: (i, 0)),
                   pl.BlockSpec((ns, D_STATE), lambda i: (0, 0)),
                   pl.BlockSpec((ns, D_STATE), lambda i: (0, 0))),
        out_shape=(jax.ShapeDtypeStruct((n_steps * rows, D_SSM), F32), state, state),
        scratch_shapes=[pltpu.VMEM((D_SSM, D_STATE), BF16), pltpu.VMEM((D_SSM, D_STATE), BF16),
                        pltpu.VMEM((rows, D_STATE), F32), pltpu.VMEM((rows, D_STATE), F32),
                        pltpu.VMEM((ns, D_STATE), F32), pltpu.VMEM((ns, D_STATE), F32),
                        pltpu.VMEM((1, D_STATE), F32), pltpu.VMEM((1, D_STATE), F32),
                        pltpu.VMEM((nt, D_STATE), F32), pltpu.VMEM((nt, D_STATE), F32)],
        compiler_params=_params(("arbitrary",)),
        name="s5_prompt" if chained else "s5_sample",
    )(*args)


def _pool_kernel(cur_ref, *rest, seq_len, prev_rows, pos0, prompt):
    if prompt:
        prev_ref, buf_ref, pw_ref, ps_ref, o_ref = rest
    else:
        buf_ref, pw_ref, ps_ref, o_ref = rest
    i = pl.program_id(0)
    cur = cur_ref[...]
    n = cur.shape[0]
    if prompt:
        prev = jnp.where(i == 0, buf_ref[...], prev_ref[...])
    else:
        prev = buf_ref[...]
    npv = prev.shape[0]
    lsh = seq_len.bit_length() - 1
    psh = prev_rows.bit_length() - 1
    r = lax.broadcasted_iota(jnp.int32, (n, n), 0)
    c = lax.broadcasted_iota(jnp.int32, (n, n), 1)
    same = (r >> lsh) == (c >> lsh)
    d = r - c
    r1 = lax.broadcasted_iota(jnp.int32, (n, npv), 0)
    c1 = lax.broadcasted_iota(jnp.int32, (n, npv), 1)
    same1 = (r1 >> lsh) == (c1 >> psh)
    d1 = (r1 & (seq_len - 1)) + prev_rows - (c1 & (prev_rows - 1))
    local = lax.broadcasted_iota(jnp.int32, (n, 1), 0) & (seq_len - 1)
    pos = pos0 + i * n + local
    outs = []
    for gi, w in enumerate(POOL_WINDOWS):
        gs = slice(gi * POOL_GROUP, (gi + 1) * POOL_GROUP)
        band = jnp.where(same & (d >= 0) & (d < w), 1.0, 0.0).astype(F32)
        band_prev = jnp.where(same1 & (d1 < w), 1.0, 0.0).astype(F32)
        xg = cur[:, gs]
        wsum = _dot01(band, xg, 3) + _dot01(band_prev, prev[:, gs], 3)
        cnt = jnp.minimum(w, pos + 1).astype(F32)
        pooled = wsum / cnt - xg
        outs.append(jnp.dot(pooled.astype(BF16), pw_ref[gi], preferred_element_type=F32))
    o_ref[...] = jnp.concatenate(outs, axis=1) * ps_ref[...]


def _pool_prompt(p_all, buf_pad, pw, ps, t):
    tt = ROW_TILE
    ratio = tt // POOL_GROUP
    return pl.pallas_call(
        functools.partial(_pool_kernel, seq_len=tt, prev_rows=POOL_GROUP, pos0=0, prompt=True),
        grid=(t // tt,),
        in_specs=[pl.BlockSpec((tt, D_POOL), lambda i: (i, 0)),
                  pl.BlockSpec((POOL_GROUP, D_POOL), lambda i: (jnp.maximum(i * ratio - 1, 0), 0)),
                  _const_spec((POOL_GROUP, D_POOL)), _const_spec(pw.shape), _const_spec((1, D_POOL))],
        out_specs=pl.BlockSpec((tt, D_POOL), lambda i: (i, 0)),
        out_shape=jax.ShapeDtypeStruct((t, D_POOL), F32),
        compiler_params=_params(("parallel",)),
        name="pool_prompt",
    )(p_all, p_all, buf_pad, pw, ps)


def _pool_sample(p_all, buf_pad, pw, ps, t, nb, tn, past):
    rows = nb * tn
    blk = t // rows
    return pl.pallas_call(
        functools.partial(_pool_kernel, seq_len=tn, prev_rows=16, pos0=past, prompt=False),
        grid=(1,),
        in_specs=[pl.BlockSpec((rows, D_POOL), lambda i: (blk, 0)),
                  _const_spec(buf_pad.shape), _const_spec(pw.shape), _const_spec((1, D_POOL))],
        out_specs=pl.BlockSpec((rows, D_POOL), lambda i: (0, 0)),
        out_shape=jax.ShapeDtypeStruct((rows, D_POOL), F32),
        compiler_params=_params(("arbitrary",)),
        name="pool_sample",
    )(p_all, buf_pad, pw, ps)


def _outproj_kernel(x_ref, ap_ref, as_ref, sp_ref, ss_ref, pp_ref, ps_ref, ong_ref, wout_ref, n2_ref,
                    wrh_ref, wrl_ref, rb_ref, xm_ref, xn_ref, ids_ref, gates_ref, *, prompt_tiles):
    g = ong_ref[...]
    sample = pl.program_id(0) >= prompt_tiles
    ma = _rms(jnp.where(sample, as_ref[...], ap_ref[...]), g[:, :D_ATTN]).astype(BF16)
    ms = _rms(jnp.where(sample, ss_ref[...], sp_ref[...]), g[:, D_ATTN:D_ATTN + D_SSM]).astype(BF16)
    mp = _rms(jnp.where(sample, ps_ref[...], pp_ref[...]), g[:, D_ATTN + D_SSM:]).astype(BF16)
    xm = (x_ref[...]
          + jnp.dot(ma, wout_ref[:D_ATTN, :], preferred_element_type=F32)
          + jnp.dot(ms, wout_ref[D_ATTN:D_ATTN + D_SSM, :], preferred_element_type=F32)
          + jnp.dot(mp, wout_ref[D_ATTN + D_SSM:, :], preferred_element_type=F32))
    xm_ref[...] = xm
    xn = _rms(xm, n2_ref[...])
    tm = xn.shape[0]
    for j in range(xn.shape[1] // ROW_CHUNK):
        xn_ref[_chunk_rows(j, tm), :] = xn[:, j * ROW_CHUNK:(j + 1) * ROW_CHUNK]
    xh = xn.astype(BF16)
    xl = (xn - xh.astype(F32)).astype(BF16)
    logits = (jnp.dot(xh, wrh_ref[...], preferred_element_type=F32)
              + jnp.dot(xl, wrh_ref[...], preferred_element_type=F32)
              + jnp.dot(xh, wrl_ref[...], preferred_element_type=F32))
    lt = jnp.transpose(logits)[:rb_ref.shape[0]] + rb_ref[...]
    ng, ne = N_EXPERT_GROUPS, EXPERTS_PER_GROUP
    gl = [lt[j:j + 1, :] for j in range(ng)]
    gmax = functools.reduce(jnp.maximum, gl)
    gsel = jnp.full(gmax.shape, ng - 1, jnp.int32)
    for j in range(ng - 2, -1, -1):
        gsel = jnp.where(gl[j] == gmax, j, gsel)
    gw = 1.0 / functools.reduce(jnp.add, [jnp.exp(v - gmax) for v in gl])
    es = []
    for j in range(ne):
        v = lt[ng + (ng - 1) * ne + j:ng + (ng - 1) * ne + j + 1, :]
        for gi in range(ng - 2, -1, -1):
            v = jnp.where(gsel == gi, lt[ng + gi * ne + j:ng + gi * ne + j + 1, :], v)
        es.append(v)

    def first_argmax(vals):
        vmax = functools.reduce(jnp.maximum, vals)
        idx = jnp.full(vmax.shape, len(vals) - 1, jnp.int32)
        for j in range(len(vals) - 2, -1, -1):
            idx = jnp.where(vals[j] == vmax, j, idx)
        return vmax, idx

    v1, i1 = first_argmax(es)
    v2, i2 = first_argmax([jnp.where(i1 == j, -jnp.inf, es[j]) for j in range(ne)])
    t = jnp.exp(v2 - v1)
    p1 = 1.0 / (1.0 + t)
    p2 = t / (1.0 + t)
    row = lax.broadcasted_iota(jnp.int32, ids_ref.shape, 0)
    ids_ref[...] = jnp.where(row == 0, gsel * ne + i1, jnp.where(row == 1, gsel * ne + i2, 0))
    gates_ref[...] = jnp.where(row == 0, p1 * gw, jnp.where(row == 1, p2 * gw, 0.0))


def _outproj(x, mix_prompt, mix_sample, ong, wout, n2, wr_hi, wr_lo, rb):
    m, d = x.shape
    tm = ROW_TILE
    pt = mix_prompt[0].shape[0] // tm
    row = lambda w: pl.BlockSpec((tm, w), lambda i: (i, 0))
    prow = lambda w: pl.BlockSpec((tm, w), lambda i: (jnp.minimum(i, pt - 1), 0))
    srow = lambda w: pl.BlockSpec((tm, w), lambda i: (jnp.maximum(i - pt, 0), 0))
    col = pl.BlockSpec((8, tm), lambda i: (0, i))
    return pl.pallas_call(
        functools.partial(_outproj_kernel, prompt_tiles=pt),
        grid=(m // tm,),
        in_specs=[row(d), prow(D_ATTN), srow(D_ATTN), prow(D_SSM), srow(D_SSM), prow(D_POOL), srow(D_POOL),
                  _const_spec((1, d)),
                  _const_spec(wout.shape), _const_spec((1, d)), _const_spec(wr_hi.shape),
                  _const_spec(wr_lo.shape), _const_spec(rb.shape)],
        out_specs=(row(d), pl.BlockSpec((tm * ROW_CHUNKS, ROW_CHUNK), lambda i: (i, 0)), col, col),
        out_shape=(jax.ShapeDtypeStruct((m, d), F32), jax.ShapeDtypeStruct((m * ROW_CHUNKS, ROW_CHUNK), F32),
                   jax.ShapeDtypeStruct((8, m), jnp.int32), jax.ShapeDtypeStruct((8, m), F32)),
        compiler_params=_params(("parallel",)),
        name="outproj_router",
    )(x, mix_prompt[0], mix_sample[0], mix_prompt[1], mix_sample[1], mix_prompt[2], mix_sample[2],
      ong, wout, n2, wr_hi, wr_lo, rb)


def _token_copy(src_hbm, tok, dst, r, sem):
    return pltpu.make_async_copy(src_hbm.at[pl.ds(pl.multiple_of(tok * ROW_CHUNKS, ROW_CHUNKS), ROW_CHUNKS), :],
                                 dst.at[pl.ds(pl.multiple_of(r * ROW_CHUNKS, ROW_CHUNKS), ROW_CHUNKS), :], sem)


def _expert_kernel(vt_ref, ve_ref, vlo_ref, vhi_ref, tok_ref, x_hbm, gate_ref, wg_ref, wu_ref, wd_ref,
                   o_ref, xs, wgb, wub, wdb, sem):
    v = pl.program_id(0)
    tm = gate_ref.shape[0]
    lo = vlo_ref[v]
    hi = vhi_ref[v]
    base = vt_ref[v] * tm
    nonempty = hi > lo
    first = jnp.logical_and(nonempty, lo == base)
    changed = jnp.logical_or(v == 0, ve_ref[v] != ve_ref[jnp.maximum(v - 1, 0)])

    @pl.when(changed)
    def _():
        wgb[...] = wg_ref[...].astype(BF16)
        wub[...] = wu_ref[...].astype(BF16)
        wdb[...] = wd_ref[...].astype(BF16)

    @pl.when(first)
    def _():
        def issue(r, carry):
            _token_copy(x_hbm, tok_ref[base + r], xs, r, sem).start()
            return carry

        lax.fori_loop(0, tm, issue, 0)

        def drain(r, carry):
            _token_copy(x_hbm, 0, xs, r, sem).wait()
            return carry

        lax.fori_loop(0, tm, drain, 0)

    @pl.when(nonempty)
    def _():
        xb = jnp.concatenate([xs[_chunk_rows(j, tm), :] for j in range(ROW_CHUNKS)], axis=1).astype(BF16)
        hg = jnp.dot(xb, wgb[...], preferred_element_type=F32)
        hu = jnp.dot(xb, wub[...], preferred_element_type=F32)
        row = base + lax.broadcasted_iota(jnp.int32, (tm, 1), 0)
        keep = jnp.logical_and(row >= lo, row < hi)
        act = jnp.where(keep, (hg * jax.nn.sigmoid(hg)) * hu * gate_ref[...], 0.0)
        y = jnp.dot(act.astype(BF16), wdb[...], preferred_element_type=F32)

        @pl.when(first)
        def _():
            for j in range(ROW_CHUNKS):
                o_ref[_chunk_rows(j, tm), :] = y[:, j * ROW_CHUNK:(j + 1) * ROW_CHUNK]

        @pl.when(jnp.logical_not(first))
        def _():
            for j in range(ROW_CHUNKS):
                o_ref[_chunk_rows(j, tm), :] += y[:, j * ROW_CHUNK:(j + 1) * ROW_CHUNK]


def _expert_ffn(plan, xn_rows, wg, wu, wd):
    v_tile, v_exp, v_lo, v_hi, tok_sorted, gate_sorted = plan
    n_flat = tok_sorted.shape[0]
    d = wg.shape[1]
    tm = ROW_TILE
    grid_spec = pltpu.PrefetchScalarGridSpec(
        num_scalar_prefetch=5,
        grid=(v_tile.shape[0],),
        in_specs=[pl.BlockSpec(memory_space=pl.ANY),
                  pl.BlockSpec((tm, 1), lambda v, vt, ve, lo, hi, tok: (vt[v], 0)),
                  pl.BlockSpec((None, d, D_EXPERT), lambda v, vt, ve, lo, hi, tok: (ve[v], 0, 0)),
                  pl.BlockSpec((None, d, D_EXPERT), lambda v, vt, ve, lo, hi, tok: (ve[v], 0, 0)),
                  pl.BlockSpec((None, D_EXPERT, d), lambda v, vt, ve, lo, hi, tok: (ve[v], 0, 0))],
        out_specs=pl.BlockSpec((tm * ROW_CHUNKS, ROW_CHUNK), lambda v, vt, ve, lo, hi, tok: (vt[v], 0)),
        scratch_shapes=[pltpu.VMEM((tm * ROW_CHUNKS, ROW_CHUNK), F32),
                        pltpu.VMEM((d, D_EXPERT), BF16), pltpu.VMEM((d, D_EXPERT), BF16),
                        pltpu.VMEM((D_EXPERT, d), BF16), pltpu.SemaphoreType.DMA(())])
    return pl.pallas_call(
        _expert_kernel,
        grid_spec=grid_spec,
        out_shape=jax.ShapeDtypeStruct((n_flat * ROW_CHUNKS, ROW_CHUNK), F32),
        compiler_params=_params(("arbitrary",)),
        name="expert_ffn",
    )(v_tile, v_exp, v_lo, v_hi, tok_sorted, xn_rows, gate_sorted, wg, wu, wd)


def _combine_kernel(d1_ref, d2_ref, xm_ref, ys_hbm, o_ref, buf1, buf2, sem):
    tm = o_ref.shape[0]
    base = pl.program_id(0) * tm

    def issue(r, carry):
        _token_copy(ys_hbm, d1_ref[base + r], buf1, r, sem).start()
        _token_copy(ys_hbm, d2_ref[base + r], buf2, r, sem).start()
        return carry

    lax.fori_loop(0, tm, issue, 0)

    def drain(r, carry):
        _token_copy(ys_hbm, 0, buf1, r, sem).wait()
        _token_copy(ys_hbm, 0, buf2, r, sem).wait()
        return carry

    lax.fori_loop(0, tm, drain, 0)
    for j in range(ROW_CHUNKS):
        cs = slice(j * ROW_CHUNK, (j + 1) * ROW_CHUNK)
        o_ref[:, cs] = xm_ref[:, cs] + buf1[_chunk_rows(j, tm), :] + buf2[_chunk_rows(j, tm), :]


def _combine(dest1, dest2, xm, ys_rows):
    m, d = xm.shape
    tm = ROW_TILE
    grid_spec = pltpu.PrefetchScalarGridSpec(
        num_scalar_prefetch=2,
        grid=(m // tm,),
        in_specs=[pl.BlockSpec((tm, d), lambda i, a, b: (i, 0)), pl.BlockSpec(memory_space=pl.ANY)],
        out_specs=pl.BlockSpec((tm, d), lambda i, a, b: (i, 0)),
        scratch_shapes=[pltpu.VMEM((tm * ROW_CHUNKS, ROW_CHUNK), F32), pltpu.VMEM((tm * ROW_CHUNKS, ROW_CHUNK), F32),
                        pltpu.SemaphoreType.DMA(())])
    return pl.pallas_call(
        _combine_kernel,
        grid_spec=grid_spec,
        out_shape=jax.ShapeDtypeStruct((m, d), F32),
        compiler_params=_params(("arbitrary",)),
        name="moe_combine",
    )(dest1, dest2, xm, ys_rows)


def _dispatch_plan(ids, gates, m):
    tm = ROW_TILE
    n_flat = 2 * m
    n_tiles = n_flat // tm
    e_flat = ids[:2].reshape(n_flat)
    g_flat = gates[:2].reshape(n_flat)
    tok = jnp.arange(n_flat, dtype=jnp.int32) % m
    _, tok_sorted, gate_sorted = lax.sort((e_flat, tok, g_flat), num_keys=1, is_stable=True)
    onehot = (e_flat[:, None] == jnp.arange(N_EXPERTS, dtype=jnp.int32)[None, :]).astype(jnp.int32)
    csum = jnp.cumsum(onehot, axis=0)
    counts = csum[-1]
    row_end = jnp.cumsum(counts)
    row_start = row_end - counts
    dest = jnp.sum(onehot * (row_start[None, :] + csum - 1), axis=1).astype(jnp.int32)
    cuts = jnp.sort(jnp.concatenate([jnp.arange(n_tiles, dtype=jnp.int32) * tm, row_start[1:].astype(jnp.int32)]))
    v_lo = cuts
    v_hi = jnp.concatenate([cuts[1:], jnp.full((1,), n_flat, jnp.int32)])
    v_tile = jnp.minimum(v_lo // tm, n_tiles - 1).astype(jnp.int32)
    v_exp = jnp.minimum(jnp.sum((v_lo[:, None] >= row_end[None, :]).astype(jnp.int32), axis=1), N_EXPERTS - 1)
    plan = (v_tile, v_exp.astype(jnp.int32), v_lo, v_hi, tok_sorted, gate_sorted.reshape(n_flat, 1))
    return plan, dest[:m], dest[m:]


def _block_diag(w):
    g, r, c = w.shape
    eye = jnp.eye(g, dtype=w.dtype)
    return (eye[:, None, :, None] * w[:, :, None, :]).reshape(g * r, g * c)


def kernel(x_prompt, x_sample, cache_k, cache_v, cache_logf, page_table, state_ssm_re, state_ssm_im, state_pool, norm1_g, w_in, f_bias, q_gain, k_gain, ssm_a_re, ssm_a_im, ssm_log_dt, ssm_b_re, ssm_b_im, ssm_c_re, ssm_c_im, ssm_d, ssm_w_glu, pool_w, pool_scale, out_norm_g, w_out, norm2_g, router_group_w, router_group_b, router_expert_w, router_expert_b, moe_w_gate, moe_w_up, moe_w_down):
    n_prompt, t, d = x_prompt.shape
    nb, tn, _ = x_sample.shape
    depth = w_in.shape[0]
    n_phys = cache_k.shape[1]
    past = page_table.shape[1] * cache_k.shape[2]
    assert n_prompt == 1
    ms = nb * tn
    m = t + ms
    x = jnp.concatenate([x_prompt.reshape(t, d), x_sample.reshape(ms, d)], axis=0)

    ck = cache_k.reshape(depth * n_phys, PAGE_SIZE * N_HEADS, HEAD_DIM)
    cv = cache_v.reshape(depth * n_phys, PAGE_SIZE * N_HEADS, HEAD_DIM)
    clf = jnp.swapaxes(cache_logf, 2, 3).reshape(depth * n_phys, N_HEADS, PAGE_SIZE)
    zero_state = jnp.zeros((1, D_STATE), F32)
    zero_buf = jnp.zeros((POOL_GROUP, D_POOL), F32)

    outs_p, outs_s = [], []
    for l in range(depth):
        wl = w_in[l]
        wqkv = wl[:, :3 * D_ATTN].astype(BF16)
        wup = wl[:, 3 * D_ATTN + N_HEADS:].astype(BF16)
        wft = jnp.pad(wl[:, 3 * D_ATTN:3 * D_ATTN + N_HEADS].T, ((0, 8), (0, 0))).astype(BF16)
        q, k, v, kb, vb, lft, u, p_in = _proj(
            x, norm1_g[l].reshape(1, d), wqkv, wup, wft, f_bias[l].reshape(N_HEADS, 1),
            q_gain[l].reshape(1, HEAD_DIM), k_gain[l].reshape(1, HEAD_DIM))

        c_prompt = _prompt_cumsum(lft, t)
        qk_bound = 1.02 * (HEAD_DIM ** 0.5) * jnp.max(jnp.abs(q_gain[l])) * jnp.max(jnp.abs(k_gain[l]))
        a_p = _fox_prompt(q, kb, vb, c_prompt, t, qk_bound)
        lfn = lft[:, t:].reshape(N_HEADS, nb, tn).transpose(1, 0, 2)
        a_s = _fox_sample(q, k, v, lfn, ck, cv, clf, page_table, l, n_phys, t)

        consts = (ssm_a_re[l].reshape(1, D_STATE), ssm_a_im[l].reshape(1, D_STATE),
                  jnp.repeat(ssm_log_dt[l], SSM_STATE).reshape(1, D_STATE),
                  _block_diag(jnp.swapaxes(ssm_b_re[l], 1, 2)), _block_diag(jnp.swapaxes(ssm_b_im[l], 1, 2)),
                  _block_diag(jnp.swapaxes(ssm_c_re[l], 1, 2)).astype(BF16),
                  _block_diag(jnp.swapaxes(ssm_c_im[l], 1, 2)).astype(BF16),
                  ssm_d[l].reshape(1, D_SSM), ssm_w_glu[l].astype(BF16))
        s_p, hre_p, him_p = _s5(u, 0, SCAN_SEGMENTS, ROW_TILE // SCAN_SEGMENTS, t // ROW_TILE, zero_state,
                                zero_state, consts, True)
        s_s, hre_s, him_s = _s5(u, t // ms, nb, tn, 1, state_ssm_re[l].reshape(nb, D_STATE),
                                state_ssm_im[l].reshape(nb, D_STATE), consts, False)

        pw = pool_w[l].astype(BF16)
        ps = pool_scale[l].reshape(1, D_POOL)
        po_p = _pool_prompt(p_in, zero_buf, pw, ps, t)
        buf_s = jnp.pad(state_pool[l], ((0, 0), (1, 0), (0, 0))).reshape(nb * 16, D_POOL)
        po_s = _pool_sample(p_in, buf_s, pw, ps, t, nb, tn, past)

        n_route = N_EXPERT_GROUPS + N_EXPERTS
        wr = jnp.pad(jnp.concatenate([router_group_w[l], router_expert_w[l]], axis=1), ((0, 0), (0, 128 - n_route)))
        rb = jnp.pad(jnp.concatenate([router_group_b[l], router_expert_b[l]]), (0, 24 - n_route)).reshape(24, 1)
        wr_hi = wr.astype(BF16)
        wr_lo = (wr - wr_hi.astype(F32)).astype(BF16)
        xm, xn, ids, gates = _outproj(x, (a_p, s_p, po_p), (a_s, s_s, po_s), out_norm_g[l].reshape(1, d),
                                      w_out[l].astype(BF16), norm2_g[l].reshape(1, d), wr_hi, wr_lo, rb)

        plan, dest1, dest2 = _dispatch_plan(ids, gates, m)
        ys = _expert_ffn(plan, xn, moe_w_gate[l], moe_w_up[l], moe_w_down[l])
        x = _combine(dest1, dest2, xm, ys)

        lf_rows = lft.T
        th = t * N_HEADS
        outs_p.append((k[:th].reshape(1, t, N_HEADS, HEAD_DIM), v[:th].reshape(1, t, N_HEADS, HEAD_DIM),
                       lf_rows[:t].reshape(1, t, N_HEADS),
                       hre_p.reshape(1, SSM_GROUPS, SSM_STATE), him_p.reshape(1, SSM_GROUPS, SSM_STATE),
                       p_in[t - POOL_BUF:t].reshape(1, POOL_BUF, D_POOL)))
        p_s = p_in[t:].reshape(nb, tn, D_POOL)
        outs_s.append((k[th:].reshape(nb, tn, N_HEADS, HEAD_DIM), v[th:].reshape(nb, tn, N_HEADS, HEAD_DIM),
                       lf_rows[t:].reshape(nb, tn, N_HEADS),
                       hre_s.reshape(nb, SSM_GROUPS, SSM_STATE), him_s.reshape(nb, SSM_GROUPS, SSM_STATE),
                       jnp.concatenate([state_pool[l], p_s], axis=1)[:, -POOL_BUF:]))

    stack = lambda outs, i: jnp.stack([o[i] for o in outs], axis=0)
    return (x[:t].reshape(1, t, d), x[t:].reshape(nb, tn, d),
            *[stack(outs_p, i) for i in range(6)], *[stack(outs_s, i) for i in range(6)])
```

```python
import functools
import math

import jax
import jax.numpy as jnp
from jax import lax
from jax.experimental import pallas as pl
from jax.experimental.pallas import tpu as pltpu

F32 = jnp.float32
BF16 = jnp.bfloat16
HIGHEST = lax.Precision.HIGHEST
NT_DIMS = (((1,), (1,)), ((), ()))

EPS = 1e-6
HEAD_DIM = 128
N_HEADS = 8
D_ATTN = N_HEADS * HEAD_DIM
D_SSM = 512
SSM_GROUPS = 32
SSM_GROUP = 16
SSM_STATE = 64
D_STATE = SSM_GROUPS * SSM_STATE
D_POOL = 512
POOL_WINDOWS = (2, 4, 8, 16)
POOL_GROUP = 128
POOL_BUF = 15
N_EXPERT_GROUPS = 4
EXPERTS_PER_GROUP = 4
N_EXPERTS = 16
D_EXPERT = 512
PAGE_SIZE = 128
NEG_BIG = -1e30
SKIP_MARGIN = 30.0

ROW_TILE = 256
ROW_CHUNK = 128
ROW_CHUNKS = 2048 // ROW_CHUNK
ATTN_TILE = 512
PAGES_PER_STEP = 8
SCAN_COLS = 512
SCAN_SEGMENTS = 8
VMEM_LIMIT = 56 * 1024 * 1024


def _params(sem, vmem=VMEM_LIMIT):
    return pltpu.CompilerParams(dimension_semantics=sem, vmem_limit_bytes=vmem)


def _const_spec(shape):
    nd = len(shape)
    return pl.BlockSpec(shape, lambda *_: (0,) * nd, pipeline_mode=pl.Buffered(1))


def _rms(y, g):
    return y * lax.rsqrt(jnp.mean(y * y, axis=-1, keepdims=True) + EPS) * g


def _dot01(mat01, x, pieces):
    m = mat01.astype(BF16)
    acc = None
    rem = x
    for _ in range(pieces):
        part = rem.astype(BF16)
        rem = rem - part.astype(F32)
        d = jnp.dot(m, part, preferred_element_type=F32)
        acc = d if acc is None else acc + d
    return acc


def _chunk_rows(j, n):
    return pl.ds(j, n, stride=ROW_CHUNKS)


def _head_rows(h, n):
    return pl.ds(h, n, stride=N_HEADS)


def _proj_kernel(x_ref, g1_ref, wqkv_ref, wup_ref, wft_ref, fb_ref, qg_ref, kg_ref,
                 q_ref, kp_ref, vp_ref, ks_ref, vs_ref, kb_ref, vb_ref, lf_ref, u_ref, p_ref, *, prompt_tiles):
    hb = _rms(x_ref[...], g1_ref[...]).astype(BF16)
    tm = hb.shape[0]
    qg = qg_ref[...]
    kg = kg_ref[...]
    k_heads, v_heads = [], []
    for c in range(D_ATTN // 256):
        lo = c * 256
        qc = jnp.dot(hb, wqkv_ref[:, lo:lo + 256], preferred_element_type=F32)
        kc = jnp.dot(hb, wqkv_ref[:, D_ATTN + lo:D_ATTN + lo + 256], preferred_element_type=F32)
        vc = jnp.dot(hb, wqkv_ref[:, 2 * D_ATTN + lo:2 * D_ATTN + lo + 256], preferred_element_type=F32)
        for j in range(2):
            a = lo + j * HEAD_DIM
            q_ref[:, a:a + HEAD_DIM] = _rms(qc[:, j * HEAD_DIM:(j + 1) * HEAD_DIM], qg)
            kn = _rms(kc[:, j * HEAD_DIM:(j + 1) * HEAD_DIM], kg)
            kb_ref[:, a:a + HEAD_DIM] = kn.astype(BF16)
            k_heads.append(kn)
            v_heads.append(vc[:, j * HEAD_DIM:(j + 1) * HEAD_DIM])
        vb_ref[:, lo:lo + 256] = vc.astype(BF16)

    def store_kv(k_ref, v_ref):
        for h in range(N_HEADS):
            k_ref[_head_rows(h, tm), :] = k_heads[h]
            v_ref[_head_rows(h, tm), :] = v_heads[h]

    pl.when(pl.program_id(0) < prompt_tiles)(lambda: store_kv(kp_ref, vp_ref))
    pl.when(pl.program_id(0) >= prompt_tiles)(lambda: store_kv(ks_ref, vs_ref))
    u_ref[...] = jnp.dot(hb, wup_ref[:, :D_SSM], preferred_element_type=F32)
    p_ref[...] = jnp.dot(hb, wup_ref[:, D_SSM:], preferred_element_type=F32)
    ft = lax.dot_general(wft_ref[...], hb, NT_DIMS, preferred_element_type=F32)[:N_HEADS]
    z = ft + fb_ref[...]
    lf_ref[...] = jnp.minimum(z, 0.0) - jnp.log1p(jnp.exp(-jnp.abs(z)))


def _proj(x, t, g1, wqkv, wup, wft, fb, qg, kg):
    m, d = x.shape
    tm = ROW_TILE
    pt = t // tm
    row = lambda w: pl.BlockSpec((tm, w), lambda i: (i, 0))
    heads_p = pl.BlockSpec((tm * N_HEADS, HEAD_DIM), lambda i: (jnp.minimum(i, pt - 1), 0))
    heads_s = pl.BlockSpec((tm * N_HEADS, HEAD_DIM), lambda i: (jnp.maximum(i - pt, 0), 0))
    kv_p = jax.ShapeDtypeStruct((t * N_HEADS, HEAD_DIM), F32)
    kv_s = jax.ShapeDtypeStruct(((m - t) * N_HEADS, HEAD_DIM), F32)
    outs = (
        jax.ShapeDtypeStruct((m, D_ATTN), F32), kv_p, kv_p, kv_s, kv_s, jax.ShapeDtypeStruct((m, D_ATTN), BF16),
        jax.ShapeDtypeStruct((m, D_ATTN), BF16), jax.ShapeDtypeStruct((N_HEADS, m), F32),
        jax.ShapeDtypeStruct((m, D_SSM), F32), jax.ShapeDtypeStruct((m, D_POOL), F32))
    return pl.pallas_call(
        functools.partial(_proj_kernel, prompt_tiles=pt),
        grid=(m // tm,),
        in_specs=[row(d), _const_spec((1, d)), _const_spec(wqkv.shape), _const_spec(wup.shape),
                  _const_spec(wft.shape), _const_spec((N_HEADS, 1)), _const_spec((1, HEAD_DIM)),
                  _const_spec((1, HEAD_DIM))],
        out_specs=(row(D_ATTN), heads_p, heads_p, heads_s, heads_s, row(D_ATTN), row(D_ATTN),
                   pl.BlockSpec((N_HEADS, tm), lambda i: (0, i)), row(D_SSM), row(D_POOL)),
        out_shape=outs,
        compiler_params=_params(("arbitrary",)),
        name="proj",
    )(x, g1, wqkv, wup, wft, fb, qg, kg)


def _cumsum_kernel(lf_ref, c_ref):
    t = c_ref.shape[1]
    w = ATTN_TILE
    r = lax.broadcasted_iota(jnp.int32, (w, w), 0)
    c = lax.broadcasted_iota(jnp.int32, (w, w), 1)
    upper = jnp.where(r <= c, 1.0, 0.0).astype(F32)
    carry = jnp.zeros((N_HEADS, 1), F32)
    for s in range(t // w):
        blk = jnp.dot(lf_ref[:, s * w:(s + 1) * w], upper, precision=HIGHEST,
                      preferred_element_type=F32) + carry
        c_ref[:, s * w:(s + 1) * w] = blk
        carry = blk[:, w - 1:w]


def _prompt_cumsum(lft, t):
    return pl.pallas_call(
        _cumsum_kernel,
        grid=(1,),
        in_specs=[pl.BlockSpec((N_HEADS, t), lambda i: (0, 0))],
        out_specs=pl.BlockSpec((N_HEADS, t), lambda i: (0, 0)),
        out_shape=jax.ShapeDtypeStruct((N_HEADS, t), F32),
        compiler_params=_params(("arbitrary",)),
        name="prompt_cumsum",
    )(lft)


def _fox_prompt_kernel(kstart_ref, q_ref, k_ref, v_ref, c_ref, o_ref):
    qi = pl.program_id(1)
    first_tile = kstart_ref[pl.program_id(0), qi]
    tq = q_ref.shape[0]
    q = (q_ref[...] * (HEAD_DIM ** -0.5)).astype(BF16)

    def tile(ki, carry, masked):
        m, l, acc = carry
        off = pl.multiple_of(ki * tq, tq)
        kt = k_ref[pl.ds(off, tq), :]
        vt = v_ref[pl.ds(off, tq), :]
        s = lax.dot_general(q, kt, NT_DIMS, preferred_element_type=F32) - c_ref[ki]
        if masked:
            r = lax.broadcasted_iota(jnp.int32, s.shape, 0)
            c = lax.broadcasted_iota(jnp.int32, s.shape, 1)
            s = jnp.where(c <= r, s, NEG_BIG)
        m_new = jnp.maximum(m, jnp.max(s, axis=-1, keepdims=True))
        alpha = jnp.exp(m - m_new)
        p = jnp.exp(s - m_new)
        l = alpha * l + jnp.sum(p, axis=-1, keepdims=True)
        acc = alpha * acc + jnp.dot(p.astype(BF16), vt, preferred_element_type=F32)
        return m_new, l, acc

    init = (jnp.full((tq, 1), NEG_BIG, F32), jnp.zeros((tq, 1), F32), jnp.zeros((tq, HEAD_DIM), F32))
    carry = lax.fori_loop(first_tile, qi, lambda ki, cr: tile(ki, cr, False), init)
    _, l, acc = tile(qi, carry, True)
    o_ref[...] = acc / l


def _negligible_key_tiles(c, qk_bound, tq):
    c_first = c[:, ::tq]
    c_last = c[:, tq - 1::tq]
    nq = c_first.shape[1]
    gap = c_first[:, :, None] - c_last[:, None, :]
    earlier = jnp.arange(nq)[None, :] < jnp.arange(nq)[:, None]
    skip = (gap < -(2.0 * qk_bound + SKIP_MARGIN)) & earlier[None]
    return jnp.sum(skip.astype(jnp.int32), axis=2)


def _fox_prompt(q, kb, vb, c, t, qk_bound):
    tq = ATTN_TILE
    nq = t // tq
    c4 = c.reshape(N_HEADS, nq, 1, tq)
    kstart = _negligible_key_tiles(c, qk_bound, tq)
    grid_spec = pltpu.PrefetchScalarGridSpec(
        num_scalar_prefetch=1,
        grid=(N_HEADS, nq),
        in_specs=[pl.BlockSpec((tq, HEAD_DIM), lambda h, i, ks: (i, h)),
                  pl.BlockSpec((t, HEAD_DIM), lambda h, i, ks: (0, h)),
                  pl.BlockSpec((t, HEAD_DIM), lambda h, i, ks: (0, h)),
                  pl.BlockSpec((None, nq, 1, tq), lambda h, i, ks: (h, 0, 0, 0))],
        out_specs=pl.BlockSpec((tq, HEAD_DIM), lambda h, i, ks: (i, h)))
    return pl.pallas_call(
        _fox_prompt_kernel,
        grid_spec=grid_spec,
        out_shape=jax.ShapeDtypeStruct((t, D_ATTN), F32),
        compiler_params=_params(("parallel", "parallel")),
        name="fox_prompt",
    )(kstart, q, kb, vb, c4)


def _fox_sample_kernel(pt_ref, q_ref, kn_ref, vn_ref, lfn_ref, *rest, pg):
    del pt_ref
    k_refs = rest[:pg]
    v_refs = rest[pg:2 * pg]
    lf_refs = rest[2 * pg:3 * pg]
    o_ref = rest[3 * pg]
    m_scr, l_scr, acc_scr, c_scr = rest[3 * pg + 1:]
    g = pl.program_id(1)
    tn = q_ref.shape[0]
    scale = HEAD_DIM ** -0.5

    @pl.when(g == 0)
    def _():
        m_scr[...] = jnp.full(m_scr.shape, NEG_BIG, F32)
        l_scr[...] = jnp.zeros(l_scr.shape, F32)
        acc_scr[...] = jnp.zeros(acc_scr.shape, F32)
        c_scr[...] = jnp.zeros(c_scr.shape, F32)

    r = lax.broadcasted_iota(jnp.int32, (PAGE_SIZE, PAGE_SIZE), 0)
    c = lax.broadcasted_iota(jnp.int32, (PAGE_SIZE, PAGE_SIZE), 1)
    upper = jnp.where(r <= c, 1.0, 0.0).astype(BF16)
    lf = jnp.concatenate([lf_refs[j][...] for j in range(pg)], axis=0)
    hi = lf.astype(BF16).astype(F32)
    rem = lf - hi
    mid = rem.astype(BF16).astype(F32)
    pieces = jnp.concatenate([hi, mid, rem - mid], axis=0).astype(BF16)
    cs = jnp.dot(pieces, upper, preferred_element_type=F32)
    n = pg * N_HEADS
    c_in = cs[:n] + cs[n:2 * n] + cs[2 * n:]
    carry = c_scr[...]
    c_pages = []
    for j in range(pg):
        cp = c_in[j * N_HEADS:(j + 1) * N_HEADS] + carry
        carry = cp[:, PAGE_SIZE - 1:PAGE_SIZE]
        c_pages.append(cp)
    c_scr[...] = carry

    q = (q_ref[...] * scale).astype(BF16)
    s = jnp.concatenate(
        [jnp.concatenate(
            [lax.dot_general(q[:, h * HEAD_DIM:(h + 1) * HEAD_DIM], k_refs[j][_head_rows(h, PAGE_SIZE), :].astype(BF16), NT_DIMS,
                             preferred_element_type=F32) - c_pages[j][h:h + 1, :] for j in range(pg)], axis=1)
         for h in range(N_HEADS)], axis=0)
    m_old = m_scr[...]
    m_new = jnp.maximum(m_old, jnp.max(s, axis=-1, keepdims=True))
    alpha = jnp.exp(m_old - m_new)
    p = jnp.exp(s - m_new)
    l_scr[...] = alpha * l_scr[...] + jnp.sum(p, axis=-1, keepdims=True)
    pvs = []
    for h in range(N_HEADS):
        ph = p[h * tn:(h + 1) * tn].astype(BF16)
        pv = jnp.dot(ph[:, :PAGE_SIZE], v_refs[0][_head_rows(h, PAGE_SIZE), :].astype(BF16),
                     preferred_element_type=F32)
        for j in range(1, pg):
            pv = pv + jnp.dot(ph[:, j * PAGE_SIZE:(j + 1) * PAGE_SIZE],
                              v_refs[j][_head_rows(h, PAGE_SIZE), :].astype(BF16),
                              preferred_element_type=F32)
        pvs.append(pv)
    acc_scr[...] = alpha * acc_scr[...] + jnp.concatenate(pvs, axis=0)
    m_scr[...] = m_new

    @pl.when(g == pl.num_programs(1) - 1)
    def _():
        lfn = lfn_ref[...]
        run = c_scr[...]
        c_new = []
        for j in range(tn):
            run = run + lfn[:, j:j + 1]
            c_new.append(run)
        qf = q_ref[...] * scale
        row = lax.broadcasted_iota(jnp.int32, (tn, 1), 0)
        for h in range(N_HEADS):
            hs = slice(h * HEAD_DIM, (h + 1) * HEAD_DIM)
            rows = slice(h * tn, (h + 1) * tn)
            qh = qf[:, hs]
            kh = kn_ref[_head_rows(h, tn), :]
            vh = vn_ref[_head_rows(h, tn), :]
            cols = []
            for j in range(tn):
                sj = jnp.sum(qh * kh[j:j + 1, :], axis=-1, keepdims=True) - c_new[j][h:h + 1, :]
                cols.append(jnp.where(row >= j, sj, NEG_BIG))
            m_old = m_scr[rows, :]
            m_fin = m_old
            for sj in cols:
                m_fin = jnp.maximum(m_fin, sj)
            alpha = jnp.exp(m_old - m_fin)
            l = alpha * l_scr[rows, :]
            acc = alpha * acc_scr[rows, :]
            for j, sj in enumerate(cols):
                pj = jnp.exp(sj - m_fin)
                l = l + pj
                acc = acc + pj * vh[j:j + 1, :]
            o_ref[:, hs] = acc / l


def _fox_sample(q, k, v, lfn, cache_k, cache_v, cache_lft, page_table, layer, n_phys, t_rows):
    nb, n_pages = page_table.shape
    tn = (q.shape[0] - t_rows) // nb
    pg = min(PAGES_PER_STEP, n_pages)
    row0 = t_rows // tn
    base = layer * n_phys

    def kv_page(j):
        return pl.BlockSpec((None, PAGE_SIZE * N_HEADS, HEAD_DIM),
                            lambda b, g, pt: (base + pt[b, g * pg + j], 0, 0))

    def lf_page(j):
        return pl.BlockSpec((None, N_HEADS, PAGE_SIZE), lambda b, g, pt: (base + pt[b, g * pg + j], 0, 0))

    new_kv = pl.BlockSpec((tn * N_HEADS, HEAD_DIM), lambda b, g, pt: (b, 0))
    in_specs = [pl.BlockSpec((tn, D_ATTN), lambda b, g, pt: (row0 + b, 0)), new_kv, new_kv,
                pl.BlockSpec((None, N_HEADS, tn), lambda b, g, pt: (b, 0, 0))]
    in_specs += [kv_page(j) for j in range(pg)]
    in_specs += [kv_page(j) for j in range(pg)]
    in_specs += [lf_page(j) for j in range(pg)]
    grid_spec = pltpu.PrefetchScalarGridSpec(
        num_scalar_prefetch=1,
        grid=(nb, n_pages // pg),
        in_specs=in_specs,
        out_specs=pl.BlockSpec((tn, D_ATTN), lambda b, g, pt: (b, 0)),
        scratch_shapes=[pltpu.VMEM((N_HEADS * tn, 1), F32), pltpu.VMEM((N_HEADS * tn, 1), F32),
                        pltpu.VMEM((N_HEADS * tn, HEAD_DIM), F32), pltpu.VMEM((N_HEADS, 1), F32)])
    return pl.pallas_call(
        functools.partial(_fox_sample_kernel, pg=pg),
        grid_spec=grid_spec,
        out_shape=jax.ShapeDtypeStruct((nb * tn, D_ATTN), F32),
        compiler_params=_params(("parallel", "arbitrary")),
        name="fox_sample",
    )(page_table, q, k, v, lfn, *([cache_k] * pg), *([cache_v] * pg), *([cache_lft] * pg))


def _s5_kernel(u_ref, h0re_ref, h0im_ref, are_ref, aim_ref, ldt_ref, bdre_ref, bdim_ref,
               cdre_ref, cdim_ref, d_ref, wglu_ref, *rest, nb, nt, chained):
    o_ref, hre_out, him_out, bbre, bbim, bure, buim, hre, him, abre, abim, pwre, pwim = rest
    i = pl.program_id(0)

    @pl.when(i == 0)
    def _():
        a_re = are_ref[...]
        a_im = aim_ref[...]
        dt = jnp.exp(ldt_ref[...])
        mag = jnp.exp(dt * a_re)
        ab_re = mag * jnp.cos(dt * a_im)
        ab_im = mag * jnp.sin(dt * a_im)
        den = a_re * a_re + a_im * a_im
        nr = ab_re - 1.0
        coef_re = (nr * a_re + ab_im * a_im) / den
        coef_im = (ab_im * a_re - nr * a_im) / den
        bbre[...] = (coef_re * bdre_ref[...] - coef_im * bdim_ref[...]).astype(BF16)
        bbim[...] = (coef_re * bdim_ref[...] + coef_im * bdre_ref[...]).astype(BF16)
        abre[...] = ab_re
        abim[...] = ab_im
        hre[...] = h0re_ref[...]
        him[...] = h0im_ref[...]
        if chained:
            def power(j, carry):
                pr, pi = carry
                pwre[pl.ds(j, 1), :] = pr
                pwim[pl.ds(j, 1), :] = pi
                return ab_re * pr - ab_im * pi, ab_re * pi + ab_im * pr

            lax.fori_loop(0, nt, power, (ab_re, ab_im))

    n = nb * nt
    tsh = nt.bit_length() - 1
    bsh = nb.bit_length() - 1
    r = lax.broadcasted_iota(jnp.int32, (n, n), 0)
    c = lax.broadcasted_iota(jnp.int32, (n, n), 1)
    to_time_major = jnp.where(((r & (nb - 1)) == (c >> tsh)) & ((r >> bsh) == (c & (nt - 1))), 1.0, 0.0).astype(F32)
    to_seq_major = jnp.where(((c & (nb - 1)) == (r >> tsh)) & ((c >> bsh) == (r & (nt - 1))), 1.0, 0.0).astype(F32)
    u = _dot01(to_time_major, u_ref[...], 3)
    ub = u.astype(BF16)
    bure[...] = jnp.dot(ub, bbre[...], preferred_element_type=F32)
    buim[...] = jnp.dot(ub, bbim[...], preferred_element_type=F32)

    for cidx in range(D_STATE // SCAN_COLS):
        cs = slice(cidx * SCAN_COLS, (cidx + 1) * SCAN_COLS)
        ar = jnp.broadcast_to(abre[:, cs], (nb, SCAN_COLS))
        ai = jnp.broadcast_to(abim[:, cs], (nb, SCAN_COLS))

        def step(t, carry, cs=cs, ar=ar, ai=ai):
            hr, hi = carry
            rows = pl.ds(pl.multiple_of(t * nb, nb), nb)
            nr = ar * hr - ai * hi + bure[rows, cs]
            ni = ar * hi + ai * hr + buim[rows, cs]
            bure[rows, cs] = nr
            buim[rows, cs] = ni
            return nr, ni

        if not chained:
            hr, hi = lax.fori_loop(0, nt, step, (hre[:, cs], him[:, cs]), unroll=8)
            hre[:, cs] = hr
            him[:, cs] = hi
            continue

        zeros = jnp.zeros((nb, SCAN_COLS), F32)
        fr, fi = lax.fori_loop(0, nt, step, (zeros, zeros), unroll=8)
        ptr = pwre[nt - 1:nt, cs]
        pti = pwim[nt - 1:nt, cs]
        cr = hre[:, cs]
        ci = him[:, cs]
        seg = lax.broadcasted_iota(jnp.int32, (nb, SCAN_COLS), 0)
        inr, ini = zeros, zeros
        for s in range(nb):
            inr = jnp.where(seg == s, cr, inr)
            ini = jnp.where(seg == s, ci, ini)
            cr, ci = fr[s:s + 1, :] + ptr * cr - pti * ci, fi[s:s + 1, :] + ptr * ci + pti * cr
        hre[:, cs] = cr
        him[:, cs] = ci

        def fix(t, carry, cs=cs, inr=inr, ini=ini):
            rows = pl.ds(pl.multiple_of(t * nb, nb), nb)
            pr = pwre[pl.ds(t, 1), cs]
            pi = pwim[pl.ds(t, 1), cs]
            bure[rows, cs] += pr * inr - pi * ini
            buim[rows, cs] += pr * ini + pi * inr
            return carry

        lax.fori_loop(0, nt, fix, 0, unroll=8)

    y = (jnp.dot(bure[...].astype(BF16), cdre_ref[...], preferred_element_type=F32)
         - jnp.dot(buim[...].astype(BF16), cdim_ref[...], preferred_element_type=F32)
         + d_ref[...] * u)
    gl = 0.5 * y * (1.0 + jnp.tanh(math.sqrt(2.0 / math.pi) * (y + 0.044715 * (y * y * y))))
    gate = jax.nn.sigmoid(jnp.dot(gl.astype(BF16), wglu_ref[...], preferred_element_type=F32))
    o_ref[...] = _dot01(to_seq_major, gl * gate, 3)
    hre_out[...] = hre[...]
    him_out[...] = him[...]


def _s5(u_all, row_block0, nb, nt, n_steps, h0re, h0im, consts, chained):
    rows = nb * nt
    ns = 1 if chained else nb
    are, aim, ldt, bdre, bdim, cdre, cdim, dvec, wglu = consts
    in_specs = [pl.BlockSpec((rows, D_SSM), lambda i: (row_block0 + i, 0)),
                _const_spec((ns, D_STATE)), _const_spec((ns, D_STATE)),
                _const_spec((1, D_STATE)), _const_spec((1, D_STATE)), _const_spec((1, D_STATE)),
                _const_spec((D_SSM, D_STATE)), _const_spec((D_SSM, D_STATE)),
                _const_spec((D_STATE, D_SSM)), _const_spec((D_STATE, D_SSM)),
                _const_spec((1, D_SSM)), _const_spec((D_SSM, D_SSM))]
    args = [u_all, h0re, h0im, are, aim, ldt, bdre, bdim, cdre, cdim, dvec, wglu]
    state = jax.ShapeDtypeStruct((ns, D_STATE), F32)
    return pl.pallas_call(
        functools.partial(_s5_kernel, nb=nb, nt=nt, chained=chained),
        grid=(n_steps,),
        in_specs=in_specs,
        out_specs=(pl.BlockSpec((rows, D_SSM), lambda i: (i, 0)),
                   pl.BlockSpec((ns, D_STATE), lambda i: (0, 0)),
                   pl.BlockSpec((ns, D_STATE), lambda i: (0, 0))),
        out_shape=(jax.ShapeDtypeStruct((n_steps * rows, D_SSM), F32), state, state),
        scratch_shapes=[pltpu.VMEM((D_SSM, D_STATE), BF16), pltpu.VMEM((D_SSM, D_STATE), BF16),
                        pltpu.VMEM((rows, D_STATE), F32), pltpu.VMEM((rows, D_STATE), F32),
                        pltpu.VMEM((ns, D_STATE), F32), pltpu.VMEM((ns, D_STATE), F32),
                        pltpu.VMEM((1, D_STATE), F32), pltpu.VMEM((1, D_STATE), F32),
                        pltpu.VMEM((nt, D_STATE), F32), pltpu.VMEM((nt, D_STATE), F32)],
        compiler_params=_params(("arbitrary",)),
        name="s5_prompt" if chained else "s5_sample",
    )(*args)


def _pool_kernel(cur_ref, *rest, seq_len, prev_rows, pos0, prompt):
    if prompt:
        prev_ref, buf_ref, pw_ref, ps_ref, o_ref = rest
    else:
        buf_ref, pw_ref, ps_ref, o_ref = rest
    i = pl.program_id(0)
    cur = cur_ref[...]
    n = cur.shape[0]
    if prompt:
        prev = jnp.where(i == 0, buf_ref[...], prev_ref[...])
    else:
        prev = buf_ref[...]
    npv = prev.shape[0]
    lsh = seq_len.bit_length() - 1
    psh = prev_rows.bit_length() - 1
    r = lax.broadcasted_iota(jnp.int32, (n, n), 0)
    c = lax.broadcasted_iota(jnp.int32, (n, n), 1)
    same = (r >> lsh) == (c >> lsh)
    d = r - c
    r1 = lax.broadcasted_iota(jnp.int32, (n, npv), 0)
    c1 = lax.broadcasted_iota(jnp.int32, (n, npv), 1)
    same1 = (r1 >> lsh) == (c1 >> psh)
    d1 = (r1 & (seq_len - 1)) + prev_rows - (c1 & (prev_rows - 1))
    local = lax.broadcasted_iota(jnp.int32, (n, 1), 0) & (seq_len - 1)
    pos = pos0 + i * n + local
    outs = []
    for gi, w in enumerate(POOL_WINDOWS):
        gs = slice(gi * POOL_GROUP, (gi + 1) * POOL_GROUP)
        band = jnp.where(same & (d >= 0) & (d < w), 1.0, 0.0).astype(F32)
        band_prev = jnp.where(same1 & (d1 < w), 1.0, 0.0).astype(F32)
        xg = cur[:, gs]
        wsum = _dot01(band, xg, 3) + _dot01(band_prev, prev[:, gs], 3)
        cnt = jnp.minimum(w, pos + 1).astype(F32)
        pooled = wsum / cnt - xg
        outs.append(jnp.dot(pooled.astype(BF16), pw_ref[gi], preferred_element_type=F32))
    o_ref[...] = jnp.concatenate(outs, axis=1) * ps_ref[...]


def _pool_prompt(p_all, buf_pad, pw, ps, t):
    tt = ROW_TILE
    ratio = tt // POOL_GROUP
    return pl.pallas_call(
        functools.partial(_pool_kernel, seq_len=tt, prev_rows=POOL_GROUP, pos0=0, prompt=True),
        grid=(t // tt,),
        in_specs=[pl.BlockSpec((tt, D_POOL), lambda i: (i, 0)),
                  pl.BlockSpec((POOL_GROUP, D_POOL), lambda i: (jnp.maximum(i * ratio - 1, 0), 0)),
                  _const_spec((POOL_GROUP, D_POOL)), _const_spec(pw.shape), _const_spec((1, D_POOL))],
        out_specs=pl.BlockSpec((tt, D_POOL), lambda i: (i, 0)),
        out_shape=jax.ShapeDtypeStruct((t, D_POOL), F32),
        compiler_params=_params(("parallel",)),
        name="pool_prompt",
    )(p_all, p_all, buf_pad, pw, ps)


def _pool_sample(p_all, buf_pad, pw, ps, t, nb, tn, past):
    rows = nb * tn
    blk = t // rows
    return pl.pallas_call(
        functools.partial(_pool_kernel, seq_len=tn, prev_rows=16, pos0=past, prompt=False),
        grid=(1,),
        in_specs=[pl.BlockSpec((rows, D_POOL), lambda i: (blk, 0)),
                  _const_spec(buf_pad.shape), _const_spec(pw.shape), _const_spec((1, D_POOL))],
        out_specs=pl.BlockSpec((rows, D_POOL), lambda i: (0, 0)),
        out_shape=jax.ShapeDtypeStruct((rows, D_POOL), F32),
        compiler_params=_params(("arbitrary",)),
        name="pool_sample",
    )(p_all, buf_pad, pw, ps)


def _outproj_kernel(x_ref, ap_ref, as_ref, sp_ref, ss_ref, pp_ref, ps_ref, ong_ref, wout_ref, n2_ref,
                    wrh_ref, wrl_ref, rb_ref, xm_ref, xn_ref, ids_ref, gates_ref, *, prompt_tiles):
    g = ong_ref[...]
    sample = pl.program_id(0) >= prompt_tiles
    ma = _rms(jnp.where(sample, as_ref[...], ap_ref[...]), g[:, :D_ATTN]).astype(BF16)
    ms = _rms(jnp.where(sample, ss_ref[...], sp_ref[...]), g[:, D_ATTN:D_ATTN + D_SSM]).astype(BF16)
    mp = _rms(jnp.where(sample, ps_ref[...], pp_ref[...]), g[:, D_ATTN + D_SSM:]).astype(BF16)
    xm = (x_ref[...]
          + jnp.dot(ma, wout_ref[:D_ATTN, :], preferred_element_type=F32)
          + jnp.dot(ms, wout_ref[D_ATTN:D_ATTN + D_SSM, :], preferred_element_type=F32)
          + jnp.dot(mp, wout_ref[D_ATTN + D_SSM:, :], preferred_element_type=F32))
    xm_ref[...] = xm
    xn = _rms(xm, n2_ref[...])
    tm = xn.shape[0]
    for j in range(xn.shape[1] // ROW_CHUNK):
        xn_ref[_chunk_rows(j, tm), :] = xn[:, j * ROW_CHUNK:(j + 1) * ROW_CHUNK]
    xh = xn.astype(BF16)
    xl = (xn - xh.astype(F32)).astype(BF16)
    logits = (jnp.dot(xh, wrh_ref[...], preferred_element_type=F32)
              + jnp.dot(xl, wrh_ref[...], preferred_element_type=F32)
              + jnp.dot(xh, wrl_ref[...], preferred_element_type=F32))
    lt = jnp.transpose(logits)[:rb_ref.shape[0]] + rb_ref[...]
    ng, ne = N_EXPERT_GROUPS, EXPERTS_PER_GROUP
    gl = [lt[j:j + 1, :] for j in range(ng)]
    gmax = functools.reduce(jnp.maximum, gl)
    gsel = jnp.full(gmax.shape, ng - 1, jnp.int32)
    for j in range(ng - 2, -1, -1):
        gsel = jnp.where(gl[j] == gmax, j, gsel)
    gw = 1.0 / functools.reduce(jnp.add, [jnp.exp(v - gmax) for v in gl])
    es = []
    for j in range(ne):
        v = lt[ng + (ng - 1) * ne + j:ng + (ng - 1) * ne + j + 1, :]
        for gi in range(ng - 2, -1, -1):
            v = jnp.where(gsel == gi, lt[ng + gi * ne + j:ng + gi * ne + j + 1, :], v)
        es.append(v)

    def first_argmax(vals):
        vmax = functools.reduce(jnp.maximum, vals)
        idx = jnp.full(vmax.shape, len(vals) - 1, jnp.int32)
        for j in range(len(vals) - 2, -1, -1):
            idx = jnp.where(vals[j] == vmax, j, idx)
        return vmax, idx

    v1, i1 = first_argmax(es)
    v2, i2 = first_argmax([jnp.where(i1 == j, -jnp.inf, es[j]) for j in range(ne)])
    t = jnp.exp(v2 - v1)
    p1 = 1.0 / (1.0 + t)
    p2 = t / (1.0 + t)
    row = lax.broadcasted_iota(jnp.int32, ids_ref.shape, 0)
    ids_ref[...] = jnp.where(row == 0, gsel * ne + i1, jnp.where(row == 1, gsel * ne + i2, 0))
    gates_ref[...] = jnp.where(row == 0, p1 * gw, jnp.where(row == 1, p2 * gw, 0.0))


def _outproj(x, mix_prompt, mix_sample, ong, wout, n2, wr_hi, wr_lo, rb):
    m, d = x.shape
    tm = ROW_TILE
    pt = mix_prompt[0].shape[0] // tm
    row = lambda w: pl.BlockSpec((tm, w), lambda i: (i, 0))
    prow = lambda w: pl.BlockSpec((tm, w), lambda i: (jnp.minimum(i, pt - 1), 0))
    srow = lambda w: pl.BlockSpec((tm, w), lambda i: (jnp.maximum(i - pt, 0), 0))
    col = pl.BlockSpec((8, tm), lambda i: (0, i))
    return pl.pallas_call(
        functools.partial(_outproj_kernel, prompt_tiles=pt),
        grid=(m // tm,),
        in_specs=[row(d), prow(D_ATTN), srow(D_ATTN), prow(D_SSM), srow(D_SSM), prow(D_POOL), srow(D_POOL),
                  _const_spec((1, d)),
                  _const_spec(wout.shape), _const_spec((1, d)), _const_spec(wr_hi.shape),
                  _const_spec(wr_lo.shape), _const_spec(rb.shape)],
        out_specs=(row(d), pl.BlockSpec((tm * ROW_CHUNKS, ROW_CHUNK), lambda i: (i, 0)), col, col),
        out_shape=(jax.ShapeDtypeStruct((m, d), F32), jax.ShapeDtypeStruct((m * ROW_CHUNKS, ROW_CHUNK), F32),
                   jax.ShapeDtypeStruct((8, m), jnp.int32), jax.ShapeDtypeStruct((8, m), F32)),
        compiler_params=_params(("parallel",)),
        name="outproj_router",
    )(x, mix_prompt[0], mix_sample[0], mix_prompt[1], mix_sample[1], mix_prompt[2], mix_sample[2],
      ong, wout, n2, wr_hi, wr_lo, rb)


def _token_copy(src_hbm, tok, dst, r, sem):
    return pltpu.make_async_copy(src_hbm.at[pl.ds(pl.multiple_of(tok * ROW_CHUNKS, ROW_CHUNKS), ROW_CHUNKS), :],
                                 dst.at[pl.ds(pl.multiple_of(r * ROW_CHUNKS, ROW_CHUNKS), ROW_CHUNKS), :], sem)


def _gather_tokens(src_hbm, idx_ref, first_idx, dst, sem, n):
    def issue(r, carry):
        _token_copy(src_hbm, idx_ref[first_idx + r], dst, r, sem).start()
        return carry

    lax.fori_loop(0, n, issue, 0, unroll=8)


def _wait_tokens(src_hbm, dst, sem):
    pltpu.make_async_copy(src_hbm.at[pl.ds(0, dst.shape[0]), :], dst, sem).wait()


def _expert_kernel(vt_ref, ve_ref, vlo_ref, vhi_ref, tok_ref, x_hbm, gate_ref, wg_ref, wu_ref, wd_ref,
                   o_ref, xs, wgb, wub, wdb, sem, *, n_tiles):
    v = pl.program_id(0)
    tm = gate_ref.shape[0]
    lo = vlo_ref[v]
    hi = vhi_ref[v]
    tile = vt_ref[v]
    base = tile * tm
    slot = tile & 1
    nonempty = hi > lo
    first = jnp.logical_and(nonempty, lo == base)
    changed = jnp.logical_or(v == 0, ve_ref[v] != ve_ref[jnp.maximum(v - 1, 0)])

    @pl.when(changed)
    def _():
        wgb[...] = wg_ref[...].astype(BF16)
        wub[...] = wu_ref[...].astype(BF16)
        wdb[...] = wd_ref[...].astype(BF16)

    @pl.when(v == 0)
    def _():
        _gather_tokens(x_hbm, tok_ref, 0, xs.at[0], sem.at[0], tm)

    @pl.when(first)
    def _():
        _wait_tokens(x_hbm, xs.at[slot], sem.at[slot])

    @pl.when(jnp.logical_and(first, tile + 1 < n_tiles))
    def _():
        _gather_tokens(x_hbm, tok_ref, base + tm, xs.at[1 - slot], sem.at[1 - slot], tm)

    @pl.when(nonempty)
    def _():
        xt = xs.at[slot]
        xb = jnp.concatenate([xt[_chunk_rows(j, tm), :] for j in range(ROW_CHUNKS)], axis=1).astype(BF16)
        hg = jnp.dot(xb, wgb[...], preferred_element_type=F32)
        hu = jnp.dot(xb, wub[...], preferred_element_type=F32)
        row = base + lax.broadcasted_iota(jnp.int32, (tm, 1), 0)
        keep = jnp.logical_and(row >= lo, row < hi)
        act = jnp.where(keep, (hg * jax.nn.sigmoid(hg)) * hu * gate_ref[...], 0.0)
        y = jnp.dot(act.astype(BF16), wdb[...], preferred_element_type=F32)

        @pl.when(first)
        def _():
            for j in range(ROW_CHUNKS):
                o_ref[_chunk_rows(j, tm), :] = y[:, j * ROW_CHUNK:(j + 1) * ROW_CHUNK]

        @pl.when(jnp.logical_not(first))
        def _():
            for j in range(ROW_CHUNKS):
                o_ref[_chunk_rows(j, tm), :] += y[:, j * ROW_CHUNK:(j + 1) * ROW_CHUNK]


def _expert_ffn(plan, xn_rows, layer, wg, wu, wd):
    v_tile, v_exp, v_lo, v_hi, tok_sorted, gate_sorted = plan
    n_flat = tok_sorted.shape[0]
    d = wg.shape[2]
    tm = ROW_TILE
    grid_spec = pltpu.PrefetchScalarGridSpec(
        num_scalar_prefetch=5,
        grid=(v_tile.shape[0],),
        in_specs=[pl.BlockSpec(memory_space=pl.ANY),
                  pl.BlockSpec((tm, 1), lambda v, vt, ve, lo, hi, tok: (vt[v], 0)),
                  pl.BlockSpec((None, None, d, D_EXPERT), lambda v, vt, ve, lo, hi, tok: (layer, ve[v], 0, 0)),
                  pl.BlockSpec((None, None, d, D_EXPERT), lambda v, vt, ve, lo, hi, tok: (layer, ve[v], 0, 0)),
                  pl.BlockSpec((None, None, D_EXPERT, d), lambda v, vt, ve, lo, hi, tok: (layer, ve[v], 0, 0))],
        out_specs=pl.BlockSpec((tm * ROW_CHUNKS, ROW_CHUNK), lambda v, vt, ve, lo, hi, tok: (vt[v], 0)),
        scratch_shapes=[pltpu.VMEM((2, tm * ROW_CHUNKS, ROW_CHUNK), F32),
                        pltpu.VMEM((d, D_EXPERT), BF16), pltpu.VMEM((d, D_EXPERT), BF16),
                        pltpu.VMEM((D_EXPERT, d), BF16), pltpu.SemaphoreType.DMA((2,))])
    return pl.pallas_call(
        functools.partial(_expert_kernel, n_tiles=n_flat // tm),
        grid_spec=grid_spec,
        out_shape=jax.ShapeDtypeStruct((n_flat * ROW_CHUNKS, ROW_CHUNK), F32),
        compiler_params=_params(("arbitrary",)),
        name="expert_ffn",
    )(v_tile, v_exp, v_lo, v_hi, tok_sorted, xn_rows, gate_sorted, wg, wu, wd)


def _combine_kernel(d1_ref, d2_ref, xm_ref, ys_hbm, *rest, n_tiles, prompt_tiles):
    out_refs, (buf1, buf2, sem) = rest[:-3], rest[-3:]
    i = pl.program_id(0)
    tm = xm_ref.shape[0]
    slot = i & 1

    def start(tile, s):
        _gather_tokens(ys_hbm, d1_ref, tile * tm, buf1.at[s], sem.at[0, s], tm)
        _gather_tokens(ys_hbm, d2_ref, tile * tm, buf2.at[s], sem.at[1, s], tm)

    pl.when(i == 0)(lambda: start(0, 0))
    _wait_tokens(ys_hbm, buf1.at[slot], sem.at[0, slot])
    _wait_tokens(ys_hbm, buf2.at[slot], sem.at[1, slot])
    pl.when(i + 1 < n_tiles)(lambda: start(i + 1, 1 - slot))
    rows1 = buf1.at[slot]
    rows2 = buf2.at[slot]

    def write(o_ref):
        for j in range(ROW_CHUNKS):
            cs = slice(j * ROW_CHUNK, (j + 1) * ROW_CHUNK)
            o_ref[:, cs] = xm_ref[:, cs] + rows1[_chunk_rows(j, tm), :] + rows2[_chunk_rows(j, tm), :]

    if prompt_tiles is None:
        write(out_refs[0])
    else:
        pl.when(i < prompt_tiles)(lambda: write(out_refs[0]))
        pl.when(i >= prompt_tiles)(lambda: write(out_refs[1]))


def _combine(dest1, dest2, xm, ys_rows, split_rows=None):
    m, d = xm.shape
    tm = ROW_TILE
    row = pl.BlockSpec((tm, d), lambda i, a, b: (i, 0))
    if split_rows is None:
        pt = None
        out_specs = row
        out_shape = jax.ShapeDtypeStruct((m, d), F32)
    else:
        pt = split_rows // tm
        out_specs = (pl.BlockSpec((tm, d), lambda i, a, b: (jnp.minimum(i, pt - 1), 0)),
                     pl.BlockSpec((tm, d), lambda i, a, b: (jnp.maximum(i - pt, 0), 0)))
        out_shape = (jax.ShapeDtypeStruct((split_rows, d), F32), jax.ShapeDtypeStruct((m - split_rows, d), F32))
    buf = pltpu.VMEM((2, tm * ROW_CHUNKS, ROW_CHUNK), F32)
    grid_spec = pltpu.PrefetchScalarGridSpec(
        num_scalar_prefetch=2,
        grid=(m // tm,),
        in_specs=[row, pl.BlockSpec(memory_space=pl.ANY)],
        out_specs=out_specs,
        scratch_shapes=[buf, buf, pltpu.SemaphoreType.DMA((2, 2))])
    return pl.pallas_call(
        functools.partial(_combine_kernel, n_tiles=m // tm, prompt_tiles=pt),
        grid_spec=grid_spec,
        out_shape=out_shape,
        compiler_params=_params(("arbitrary",)),
        name="moe_combine",
    )(dest1, dest2, xm, ys_rows)


def _dispatch_plan(ids, gates, m):
    tm = ROW_TILE
    n_flat = 2 * m
    n_tiles = n_flat // tm
    e_flat = ids[:2].reshape(n_flat)
    g_flat = gates[:2].reshape(n_flat)
    tok = jnp.arange(n_flat, dtype=jnp.int32) % m
    _, tok_sorted, gate_sorted = lax.sort((e_flat, tok, g_flat), num_keys=1, is_stable=True)
    onehot = (e_flat[:, None] == jnp.arange(N_EXPERTS, dtype=jnp.int32)[None, :]).astype(jnp.int32)
    csum = jnp.cumsum(onehot, axis=0)
    counts = csum[-1]
    row_end = jnp.cumsum(counts)
    row_start = row_end - counts
    dest = jnp.sum(onehot * (row_start[None, :] + csum - 1), axis=1).astype(jnp.int32)
    cuts = jnp.sort(jnp.concatenate([jnp.arange(n_tiles, dtype=jnp.int32) * tm, row_start[1:].astype(jnp.int32)]))
    v_lo = cuts
    v_hi = jnp.concatenate([cuts[1:], jnp.full((1,), n_flat, jnp.int32)])
    v_tile = jnp.minimum(v_lo // tm, n_tiles - 1).astype(jnp.int32)
    v_exp = jnp.minimum(jnp.sum((v_lo[:, None] >= row_end[None, :]).astype(jnp.int32), axis=1), N_EXPERTS - 1)
    plan = (v_tile, v_exp.astype(jnp.int32), v_lo, v_hi, tok_sorted, gate_sorted.reshape(n_flat, 1))
    return plan, dest[:m], dest[m:]


def _block_diag(w):
    g, r, c = w.shape
    eye = jnp.eye(g, dtype=w.dtype)
    return (eye[:, None, :, None] * w[:, :, None, :]).reshape(g * r, g * c)


def kernel(x_prompt, x_sample, cache_k, cache_v, cache_logf, page_table, state_ssm_re, state_ssm_im, state_pool, norm1_g, w_in, f_bias, q_gain, k_gain, ssm_a_re, ssm_a_im, ssm_log_dt, ssm_b_re, ssm_b_im, ssm_c_re, ssm_c_im, ssm_d, ssm_w_glu, pool_w, pool_scale, out_norm_g, w_out, norm2_g, router_group_w, router_group_b, router_expert_w, router_expert_b, moe_w_gate, moe_w_up, moe_w_down):
    n_prompt, t, d = x_prompt.shape
    nb, tn, _ = x_sample.shape
    depth = w_in.shape[0]
    n_phys = cache_k.shape[1]
    past = page_table.shape[1] * cache_k.shape[2]
    assert n_prompt == 1
    ms = nb * tn
    m = t + ms
    x = jnp.concatenate([x_prompt.reshape(t, d), x_sample.reshape(ms, d)], axis=0)

    ck = cache_k.reshape(depth * n_phys, PAGE_SIZE * N_HEADS, HEAD_DIM)
    cv = cache_v.reshape(depth * n_phys, PAGE_SIZE * N_HEADS, HEAD_DIM)
    clf = jnp.swapaxes(cache_logf, 2, 3).reshape(depth * n_phys, N_HEADS, PAGE_SIZE)
    zero_state = jnp.zeros((1, D_STATE), F32)
    zero_buf = jnp.zeros((POOL_GROUP, D_POOL), F32)

    outs_p, outs_s = [], []
    for l in range(depth):
        wl = w_in[l]
        wqkv = wl[:, :3 * D_ATTN].astype(BF16)
        wup = wl[:, 3 * D_ATTN + N_HEADS:].astype(BF16)
        wft = jnp.pad(wl[:, 3 * D_ATTN:3 * D_ATTN + N_HEADS].T, ((0, 8), (0, 0))).astype(BF16)
        q, k_p, v_p, k_s, v_s, kb, vb, lft, u, p_in = _proj(
            x, t, norm1_g[l].reshape(1, d), wqkv, wup, wft, f_bias[l].reshape(N_HEADS, 1),
            q_gain[l].reshape(1, HEAD_DIM), k_gain[l].reshape(1, HEAD_DIM))

        c_prompt = _prompt_cumsum(lft, t)
        qk_bound = 1.02 * (HEAD_DIM ** 0.5) * jnp.max(jnp.abs(q_gain[l])) * jnp.max(jnp.abs(k_gain[l]))
        a_p = _fox_prompt(q, kb, vb, c_prompt, t, qk_bound)
        lfn = lft[:, t:].reshape(N_HEADS, nb, tn).transpose(1, 0, 2)
        a_s = _fox_sample(q, k_s, v_s, lfn, ck, cv, clf, page_table, l, n_phys, t)

        consts = (ssm_a_re[l].reshape(1, D_STATE), ssm_a_im[l].reshape(1, D_STATE),
                  jnp.repeat(ssm_log_dt[l], SSM_STATE).reshape(1, D_STATE),
                  _block_diag(jnp.swapaxes(ssm_b_re[l], 1, 2)), _block_diag(jnp.swapaxes(ssm_b_im[l], 1, 2)),
                  _block_diag(jnp.swapaxes(ssm_c_re[l], 1, 2)).astype(BF16),
                  _block_diag(jnp.swapaxes(ssm_c_im[l], 1, 2)).astype(BF16),
                  ssm_d[l].reshape(1, D_SSM), ssm_w_glu[l].astype(BF16))
        s_p, hre_p, him_p = _s5(u, 0, SCAN_SEGMENTS, ROW_TILE // SCAN_SEGMENTS, t // ROW_TILE, zero_state,
                                zero_state, consts, True)
        s_s, hre_s, him_s = _s5(u, t // ms, nb, tn, 1, state_ssm_re[l].reshape(nb, D_STATE),
                                state_ssm_im[l].reshape(nb, D_STATE), consts, False)

        pw = pool_w[l].astype(BF16)
        ps = pool_scale[l].reshape(1, D_POOL)
        po_p = _pool_prompt(p_in, zero_buf, pw, ps, t)
        buf_s = jnp.pad(state_pool[l], ((0, 0), (1, 0), (0, 0))).reshape(nb * 16, D_POOL)
        po_s = _pool_sample(p_in, buf_s, pw, ps, t, nb, tn, past)

        n_route = N_EXPERT_GROUPS + N_EXPERTS
        wr = jnp.pad(jnp.concatenate([router_group_w[l], router_expert_w[l]], axis=1), ((0, 0), (0, 128 - n_route)))
        rb = jnp.pad(jnp.concatenate([router_group_b[l], router_expert_b[l]]), (0, 24 - n_route)).reshape(24, 1)
        wr_hi = wr.astype(BF16)
        wr_lo = (wr - wr_hi.astype(F32)).astype(BF16)
        xm, xn, ids, gates = _outproj(x, (a_p, s_p, po_p), (a_s, s_s, po_s), out_norm_g[l].reshape(1, d),
                                      w_out[l].astype(BF16), norm2_g[l].reshape(1, d), wr_hi, wr_lo, rb)

        plan, dest1, dest2 = _dispatch_plan(ids, gates, m)
        ys = _expert_ffn(plan, xn, l, moe_w_gate, moe_w_up, moe_w_down)
        if l + 1 < depth:
            x = _combine(dest1, dest2, xm, ys)
        else:
            y_prompt, y_sample = _combine(dest1, dest2, xm, ys, split_rows=t)

        lf_rows = lft.T
        outs_p.append((k_p.reshape(1, t, N_HEADS, HEAD_DIM), v_p.reshape(1, t, N_HEADS, HEAD_DIM),
                       lf_rows[:t].reshape(1, t, N_HEADS),
                       hre_p.reshape(1, SSM_GROUPS, SSM_STATE), him_p.reshape(1, SSM_GROUPS, SSM_STATE),
                       p_in[t - POOL_BUF:t].reshape(1, POOL_BUF, D_POOL)))
        p_s = p_in[t:].reshape(nb, tn, D_POOL)
        outs_s.append((k_s.reshape(nb, tn, N_HEADS, HEAD_DIM), v_s.reshape(nb, tn, N_HEADS, HEAD_DIM),
                       lf_rows[t:].reshape(nb, tn, N_HEADS),
                       hre_s.reshape(nb, SSM_GROUPS, SSM_STATE), him_s.reshape(nb, SSM_GROUPS, SSM_STATE),
                       jnp.concatenate([state_pool[l], p_s], axis=1)[:, -POOL_BUF:]))

    stack = lambda outs, i: jnp.stack([o[i] for o in outs], axis=0)
    return (y_prompt.reshape(1, t, d), y_sample.reshape(nb, tn, d),
            *[stack(outs_p, i) for i in range(6)], *[stack(outs_s, i) for i in range(6)])
```

```python
import functools
import math

import jax
import jax.numpy as jnp
from jax import lax
from jax.experimental import pallas as pl
from jax.experimental.pallas import tpu as pltpu

F32 = jnp.float32
BF16 = jnp.bfloat16
HIGHEST = lax.Precision.HIGHEST
NT_DIMS = (((1,), (1,)), ((), ()))

EPS = 1e-6
HEAD_DIM = 128
N_HEADS = 8
D_ATTN = N_HEADS * HEAD_DIM
D_SSM = 512
SSM_GROUPS = 32
SSM_GROUP = 16
SSM_STATE = 64
D_STATE = SSM_GROUPS * SSM_STATE
D_POOL = 512
POOL_WINDOWS = (2, 4, 8, 16)
POOL_GROUP = 128
POOL_BUF = 15
N_EXPERT_GROUPS = 4
EXPERTS_PER_GROUP = 4
N_EXPERTS = 16
D_EXPERT = 512
PAGE_SIZE = 128
NEG_BIG = -1e30
SKIP_MARGIN = 30.0

ROW_TILE = 256
ROW_CHUNK = 128
ROW_CHUNKS = 2048 // ROW_CHUNK
ATTN_TILE = 512
PAGES_PER_STEP = 8
SCAN_COLS = 512
SCAN_SEGMENTS = 8
VMEM_LIMIT = 56 * 1024 * 1024


def _params(sem, vmem=VMEM_LIMIT):
    return pltpu.CompilerParams(dimension_semantics=sem, vmem_limit_bytes=vmem)


def _const_spec(shape):
    nd = len(shape)
    return pl.BlockSpec(shape, lambda *_: (0,) * nd, pipeline_mode=pl.Buffered(1))


def _layer_spec(stacked, layer):
    nd = stacked.ndim - 1
    return pl.BlockSpec((None,) + stacked.shape[1:], lambda *_: (layer,) + (0,) * nd, pipeline_mode=pl.Buffered(1))


def _rms(y, g):
    return y * lax.rsqrt(jnp.mean(y * y, axis=-1, keepdims=True) + EPS) * g


def _dot01(mat01, x, pieces):
    m = mat01.astype(BF16)
    acc = None
    rem = x
    for _ in range(pieces):
        part = rem.astype(BF16)
        rem = rem - part.astype(F32)
        d = jnp.dot(m, part, preferred_element_type=F32)
        acc = d if acc is None else acc + d
    return acc


def _chunk_rows(j, n):
    return pl.ds(j, n, stride=ROW_CHUNKS)


def _head_rows(h, n):
    return pl.ds(h, n, stride=N_HEADS)


def _proj_kernel(x_ref, g1_ref, wqkv_ref, wup_ref, wft_ref, fb_ref, qg_ref, kg_ref,
                 q_ref, kp_ref, vp_ref, ks_ref, vs_ref, kb_ref, vb_ref, lf_ref, u_ref, p_ref, *, prompt_tiles):
    hb = _rms(x_ref[...], g1_ref[...]).astype(BF16)
    tm = hb.shape[0]
    qg = qg_ref[...]
    kg = kg_ref[...]
    k_heads, v_heads = [], []
    for c in range(D_ATTN // 256):
        lo = c * 256
        qc = jnp.dot(hb, wqkv_ref[:, lo:lo + 256], preferred_element_type=F32)
        kc = jnp.dot(hb, wqkv_ref[:, D_ATTN + lo:D_ATTN + lo + 256], preferred_element_type=F32)
        vc = jnp.dot(hb, wqkv_ref[:, 2 * D_ATTN + lo:2 * D_ATTN + lo + 256], preferred_element_type=F32)
        for j in range(2):
            a = lo + j * HEAD_DIM
            q_ref[:, a:a + HEAD_DIM] = _rms(qc[:, j * HEAD_DIM:(j + 1) * HEAD_DIM], qg)
            kn = _rms(kc[:, j * HEAD_DIM:(j + 1) * HEAD_DIM], kg)
            kb_ref[:, a:a + HEAD_DIM] = kn.astype(BF16)
            k_heads.append(kn)
            v_heads.append(vc[:, j * HEAD_DIM:(j + 1) * HEAD_DIM])
        vb_ref[:, lo:lo + 256] = vc.astype(BF16)

    def store_kv(k_ref, v_ref):
        for h in range(N_HEADS):
            k_ref[_head_rows(h, tm), :] = k_heads[h]
            v_ref[_head_rows(h, tm), :] = v_heads[h]

    pl.when(pl.program_id(0) < prompt_tiles)(lambda: store_kv(kp_ref, vp_ref))
    pl.when(pl.program_id(0) >= prompt_tiles)(lambda: store_kv(ks_ref, vs_ref))
    u_ref[...] = jnp.dot(hb, wup_ref[:, :D_SSM], preferred_element_type=F32)
    p_ref[...] = jnp.dot(hb, wup_ref[:, D_SSM:], preferred_element_type=F32)
    ft = lax.dot_general(wft_ref[...], hb, NT_DIMS, preferred_element_type=F32)[:N_HEADS]
    z = ft + fb_ref[...]
    lf_ref[...] = jnp.minimum(z, 0.0) - jnp.log1p(jnp.exp(-jnp.abs(z)))


def _proj(x, t, layer, g1, wqkv, wup, wft, fb, qg, kg):
    m, d = x.shape
    tm = ROW_TILE
    pt = t // tm
    row = lambda w: pl.BlockSpec((tm, w), lambda i: (i, 0))
    heads_p = pl.BlockSpec((tm * N_HEADS, HEAD_DIM), lambda i: (jnp.minimum(i, pt - 1), 0))
    heads_s = pl.BlockSpec((tm * N_HEADS, HEAD_DIM), lambda i: (jnp.maximum(i - pt, 0), 0))
    kv_p = jax.ShapeDtypeStruct((t * N_HEADS, HEAD_DIM), F32)
    kv_s = jax.ShapeDtypeStruct(((m - t) * N_HEADS, HEAD_DIM), F32)
    outs = (
        jax.ShapeDtypeStruct((m, D_ATTN), F32), kv_p, kv_p, kv_s, kv_s, jax.ShapeDtypeStruct((m, D_ATTN), BF16),
        jax.ShapeDtypeStruct((m, D_ATTN), BF16), jax.ShapeDtypeStruct((N_HEADS, m), F32),
        jax.ShapeDtypeStruct((m, D_SSM), F32), jax.ShapeDtypeStruct((m, D_POOL), F32))
    return pl.pallas_call(
        functools.partial(_proj_kernel, prompt_tiles=pt),
        grid=(m // tm,),
        in_specs=[row(d), _const_spec((1, d)), _layer_spec(wqkv, layer), _layer_spec(wup, layer),
                  _layer_spec(wft, layer), _const_spec((N_HEADS, 1)), _const_spec((1, HEAD_DIM)),
                  _const_spec((1, HEAD_DIM))],
        out_specs=(row(D_ATTN), heads_p, heads_p, heads_s, heads_s, row(D_ATTN), row(D_ATTN),
                   pl.BlockSpec((N_HEADS, tm), lambda i: (0, i)), row(D_SSM), row(D_POOL)),
        out_shape=outs,
        compiler_params=_params(("arbitrary",)),
        name="proj",
    )(x, g1, wqkv, wup, wft, fb, qg, kg)


def _cumsum_kernel(lf_ref, c_ref):
    t = c_ref.shape[1]
    w = ATTN_TILE
    r = lax.broadcasted_iota(jnp.int32, (w, w), 0)
    c = lax.broadcasted_iota(jnp.int32, (w, w), 1)
    upper = jnp.where(r <= c, 1.0, 0.0).astype(F32)
    carry = jnp.zeros((N_HEADS, 1), F32)
    for s in range(t // w):
        blk = jnp.dot(lf_ref[:, s * w:(s + 1) * w], upper, precision=HIGHEST,
                      preferred_element_type=F32) + carry
        c_ref[:, s * w:(s + 1) * w] = blk
        carry = blk[:, w - 1:w]


def _prompt_cumsum(lft, t):
    return pl.pallas_call(
        _cumsum_kernel,
        grid=(1,),
        in_specs=[pl.BlockSpec((N_HEADS, t), lambda i: (0, 0))],
        out_specs=pl.BlockSpec((N_HEADS, t), lambda i: (0, 0)),
        out_shape=jax.ShapeDtypeStruct((N_HEADS, t), F32),
        compiler_params=_params(("arbitrary",)),
        name="prompt_cumsum",
    )(lft)


def _fox_prompt_kernel(kstart_ref, q_ref, k_ref, v_ref, c_ref, o_ref):
    qi = pl.program_id(1)
    first_tile = kstart_ref[pl.program_id(0), qi]
    tq = q_ref.shape[0]
    q = (q_ref[...] * (HEAD_DIM ** -0.5)).astype(BF16)

    def tile(ki, carry, masked):
        m, l, acc = carry
        off = pl.multiple_of(ki * tq, tq)
        kt = k_ref[pl.ds(off, tq), :]
        vt = v_ref[pl.ds(off, tq), :]
        s = lax.dot_general(q, kt, NT_DIMS, preferred_element_type=F32) - c_ref[ki]
        if masked:
            r = lax.broadcasted_iota(jnp.int32, s.shape, 0)
            c = lax.broadcasted_iota(jnp.int32, s.shape, 1)
            s = jnp.where(c <= r, s, NEG_BIG)
        m_new = jnp.maximum(m, jnp.max(s, axis=-1, keepdims=True))
        alpha = jnp.exp(m - m_new)
        p = jnp.exp(s - m_new)
        l = alpha * l + jnp.sum(p, axis=-1, keepdims=True)
        acc = alpha * acc + jnp.dot(p.astype(BF16), vt, preferred_element_type=F32)
        return m_new, l, acc

    init = (jnp.full((tq, 1), NEG_BIG, F32), jnp.zeros((tq, 1), F32), jnp.zeros((tq, HEAD_DIM), F32))
    carry = lax.fori_loop(first_tile, qi, lambda ki, cr: tile(ki, cr, False), init)
    _, l, acc = tile(qi, carry, True)
    o_ref[...] = acc / l


def _negligible_key_tiles(c, qk_bound, tq):
    c_first = c[:, ::tq]
    c_last = c[:, tq - 1::tq]
    nq = c_first.shape[1]
    gap = c_first[:, :, None] - c_last[:, None, :]
    earlier = jnp.arange(nq)[None, :] < jnp.arange(nq)[:, None]
    skip = (gap < -(2.0 * qk_bound + SKIP_MARGIN)) & earlier[None]
    return jnp.sum(skip.astype(jnp.int32), axis=2)


def _fox_prompt(q, kb, vb, c, t, qk_bound):
    tq = ATTN_TILE
    nq = t // tq
    c4 = c.reshape(N_HEADS, nq, 1, tq)
    kstart = _negligible_key_tiles(c, qk_bound, tq)
    grid_spec = pltpu.PrefetchScalarGridSpec(
        num_scalar_prefetch=1,
        grid=(N_HEADS, nq),
        in_specs=[pl.BlockSpec((tq, HEAD_DIM), lambda h, i, ks: (i, h)),
                  pl.BlockSpec((t, HEAD_DIM), lambda h, i, ks: (0, h)),
                  pl.BlockSpec((t, HEAD_DIM), lambda h, i, ks: (0, h)),
                  pl.BlockSpec((None, nq, 1, tq), lambda h, i, ks: (h, 0, 0, 0))],
        out_specs=pl.BlockSpec((tq, HEAD_DIM), lambda h, i, ks: (i, h)))
    return pl.pallas_call(
        _fox_prompt_kernel,
        grid_spec=grid_spec,
        out_shape=jax.ShapeDtypeStruct((t, D_ATTN), F32),
        compiler_params=_params(("parallel", "parallel")),
        name="fox_prompt",
    )(kstart, q, kb, vb, c4)


def _fox_sample_kernel(pt_ref, q_ref, kn_ref, vn_ref, lfn_ref, *rest, pg):
    del pt_ref
    k_refs = rest[:pg]
    v_refs = rest[pg:2 * pg]
    lf_refs = rest[2 * pg:3 * pg]
    o_ref = rest[3 * pg]
    m_scr, l_scr, acc_scr, c_scr = rest[3 * pg + 1:]
    g = pl.program_id(1)
    tn = q_ref.shape[0]
    scale = HEAD_DIM ** -0.5

    @pl.when(g == 0)
    def _():
        m_scr[...] = jnp.full(m_scr.shape, NEG_BIG, F32)
        l_scr[...] = jnp.zeros(l_scr.shape, F32)
        acc_scr[...] = jnp.zeros(acc_scr.shape, F32)
        c_scr[...] = jnp.zeros(c_scr.shape, F32)

    r = lax.broadcasted_iota(jnp.int32, (PAGE_SIZE, PAGE_SIZE), 0)
    c = lax.broadcasted_iota(jnp.int32, (PAGE_SIZE, PAGE_SIZE), 1)
    upper = jnp.where(r <= c, 1.0, 0.0).astype(BF16)
    lf = jnp.concatenate([lf_refs[j][...] for j in range(pg)], axis=0)
    hi = lf.astype(BF16).astype(F32)
    rem = lf - hi
    mid = rem.astype(BF16).astype(F32)
    pieces = jnp.concatenate([hi, mid, rem - mid], axis=0).astype(BF16)
    cs = jnp.dot(pieces, upper, preferred_element_type=F32)
    n = pg * N_HEADS
    c_in = cs[:n] + cs[n:2 * n] + cs[2 * n:]
    carry = c_scr[...]
    c_pages = []
    for j in range(pg):
        cp = c_in[j * N_HEADS:(j + 1) * N_HEADS] + carry
        carry = cp[:, PAGE_SIZE - 1:PAGE_SIZE]
        c_pages.append(cp)
    c_scr[...] = carry

    q = (q_ref[...] * scale).astype(BF16)
    s = jnp.concatenate(
        [jnp.concatenate(
            [lax.dot_general(q[:, h * HEAD_DIM:(h + 1) * HEAD_DIM], k_refs[j][_head_rows(h, PAGE_SIZE), :].astype(BF16), NT_DIMS,
                             preferred_element_type=F32) - c_pages[j][h:h + 1, :] for j in range(pg)], axis=1)
         for h in range(N_HEADS)], axis=0)
    m_old = m_scr[...]
    m_new = jnp.maximum(m_old, jnp.max(s, axis=-1, keepdims=True))
    alpha = jnp.exp(m_old - m_new)
    p = jnp.exp(s - m_new)
    l_scr[...] = alpha * l_scr[...] + jnp.sum(p, axis=-1, keepdims=True)
    pvs = []
    for h in range(N_HEADS):
        ph = p[h * tn:(h + 1) * tn].astype(BF16)
        pv = jnp.dot(ph[:, :PAGE_SIZE], v_refs[0][_head_rows(h, PAGE_SIZE), :].astype(BF16),
                     preferred_element_type=F32)
        for j in range(1, pg):
            pv = pv + jnp.dot(ph[:, j * PAGE_SIZE:(j + 1) * PAGE_SIZE],
                              v_refs[j][_head_rows(h, PAGE_SIZE), :].astype(BF16),
                              preferred_element_type=F32)
        pvs.append(pv)
    acc_scr[...] = alpha * acc_scr[...] + jnp.concatenate(pvs, axis=0)
    m_scr[...] = m_new

    @pl.when(g == pl.num_programs(1) - 1)
    def _():
        lfn = lfn_ref[...]
        run = c_scr[...]
        c_new = []
        for j in range(tn):
            run = run + lfn[:, j:j + 1]
            c_new.append(run)
        qf = q_ref[...] * scale
        row = lax.broadcasted_iota(jnp.int32, (tn, 1), 0)
        for h in range(N_HEADS):
            hs = slice(h * HEAD_DIM, (h + 1) * HEAD_DIM)
            rows = slice(h * tn, (h + 1) * tn)
            qh = qf[:, hs]
            kh = kn_ref[_head_rows(h, tn), :]
            vh = vn_ref[_head_rows(h, tn), :]
            cols = []
            for j in range(tn):
                sj = jnp.sum(qh * kh[j:j + 1, :], axis=-1, keepdims=True) - c_new[j][h:h + 1, :]
                cols.append(jnp.where(row >= j, sj, NEG_BIG))
            m_old = m_scr[rows, :]
            m_fin = m_old
            for sj in cols:
                m_fin = jnp.maximum(m_fin, sj)
            alpha = jnp.exp(m_old - m_fin)
            l = alpha * l_scr[rows, :]
            acc = alpha * acc_scr[rows, :]
            for j, sj in enumerate(cols):
                pj = jnp.exp(sj - m_fin)
                l = l + pj
                acc = acc + pj * vh[j:j + 1, :]
            o_ref[:, hs] = acc / l


def _fox_sample(q, k, v, lfn, cache_k, cache_v, cache_lft, page_table, layer, n_phys, t_rows):
    nb, n_pages = page_table.shape
    tn = (q.shape[0] - t_rows) // nb
    pg = min(PAGES_PER_STEP, n_pages)
    row0 = t_rows // tn
    base = layer * n_phys

    def kv_page(j):
        return pl.BlockSpec((None, PAGE_SIZE * N_HEADS, HEAD_DIM),
                            lambda b, g, pt: (base + pt[b, g * pg + j], 0, 0))

    def lf_page(j):
        return pl.BlockSpec((None, N_HEADS, PAGE_SIZE), lambda b, g, pt: (base + pt[b, g * pg + j], 0, 0))

    new_kv = pl.BlockSpec((tn * N_HEADS, HEAD_DIM), lambda b, g, pt: (b, 0))
    in_specs = [pl.BlockSpec((tn, D_ATTN), lambda b, g, pt: (row0 + b, 0)), new_kv, new_kv,
                pl.BlockSpec((None, N_HEADS, tn), lambda b, g, pt: (b, 0, 0))]
    in_specs += [kv_page(j) for j in range(pg)]
    in_specs += [kv_page(j) for j in range(pg)]
    in_specs += [lf_page(j) for j in range(pg)]
    grid_spec = pltpu.PrefetchScalarGridSpec(
        num_scalar_prefetch=1,
        grid=(nb, n_pages // pg),
        in_specs=in_specs,
        out_specs=pl.BlockSpec((tn, D_ATTN), lambda b, g, pt: (b, 0)),
        scratch_shapes=[pltpu.VMEM((N_HEADS * tn, 1), F32), pltpu.VMEM((N_HEADS * tn, 1), F32),
                        pltpu.VMEM((N_HEADS * tn, HEAD_DIM), F32), pltpu.VMEM((N_HEADS, 1), F32)])
    return pl.pallas_call(
        functools.partial(_fox_sample_kernel, pg=pg),
        grid_spec=grid_spec,
        out_shape=jax.ShapeDtypeStruct((nb * tn, D_ATTN), F32),
        compiler_params=_params(("parallel", "arbitrary")),
        name="fox_sample",
    )(page_table, q, k, v, lfn, *([cache_k] * pg), *([cache_v] * pg), *([cache_lft] * pg))


def _block_diag_wide(tall, rows_per_group, cols_per_group, n_cols):
    wide = jnp.concatenate([tall] * (n_cols // tall.shape[1]), axis=1)
    r = lax.broadcasted_iota(jnp.int32, wide.shape, 0)
    c = lax.broadcasted_iota(jnp.int32, wide.shape, 1)
    same = (r >> (rows_per_group.bit_length() - 1)) == (c >> (cols_per_group.bit_length() - 1))
    return jnp.where(same, wide, 0.0)


def _s5_kernel(u_ref, h0re_ref, h0im_ref, are_ref, aim_ref, ldt_ref, btre_ref, btim_ref,
               ctre_ref, ctim_ref, d_ref, wglu_ref, *rest, nb, nt, chained):
    o_ref, hre_out, him_out, bbre, bbim, cdre, cdim, bure, buim, hre, him, abre, abim, pwre, pwim = rest
    i = pl.program_id(0)
    half_in = D_SSM // 2
    half_st = D_STATE // 2

    @pl.when(i == 0)
    def _():
        bdre = _block_diag_wide(btre_ref[...], SSM_GROUP, SSM_STATE, D_STATE)
        bdim = _block_diag_wide(btim_ref[...], SSM_GROUP, SSM_STATE, D_STATE)
        cdre[...] = _block_diag_wide(ctre_ref[...], SSM_STATE, SSM_GROUP, D_SSM).astype(BF16)
        cdim[...] = _block_diag_wide(ctim_ref[...], SSM_STATE, SSM_GROUP, D_SSM).astype(BF16)
        a_re = are_ref[...]
        a_im = aim_ref[...]
        dt = jnp.exp(ldt_ref[...])
        mag = jnp.exp(dt * a_re)
        ab_re = mag * jnp.cos(dt * a_im)
        ab_im = mag * jnp.sin(dt * a_im)
        den = a_re * a_re + a_im * a_im
        nr = ab_re - 1.0
        coef_re = (nr * a_re + ab_im * a_im) / den
        coef_im = (ab_im * a_re - nr * a_im) / den
        bbre[...] = (coef_re * bdre - coef_im * bdim).astype(BF16)
        bbim[...] = (coef_re * bdim + coef_im * bdre).astype(BF16)
        abre[...] = ab_re
        abim[...] = ab_im
        hre[...] = h0re_ref[...]
        him[...] = h0im_ref[...]
        if chained:
            def power(j, carry):
                pr, pi = carry
                pwre[pl.ds(j, 1), :] = pr
                pwim[pl.ds(j, 1), :] = pi
                return ab_re * pr - ab_im * pi, ab_re * pi + ab_im * pr

            lax.fori_loop(0, nt, power, (ab_re, ab_im))

    n = nb * nt
    tsh = nt.bit_length() - 1
    bsh = nb.bit_length() - 1
    r = lax.broadcasted_iota(jnp.int32, (n, n), 0)
    c = lax.broadcasted_iota(jnp.int32, (n, n), 1)
    to_time_major = jnp.where(((r & (nb - 1)) == (c >> tsh)) & ((r >> bsh) == (c & (nt - 1))), 1.0, 0.0).astype(F32)
    to_seq_major = jnp.where(((c & (nb - 1)) == (r >> tsh)) & ((c >> bsh) == (r & (nt - 1))), 1.0, 0.0).astype(F32)
    u = _dot01(to_time_major, u_ref[...], 3)
    ub = u.astype(BF16)
    for k in range(2):
        rs = slice(k * half_in, (k + 1) * half_in)
        cs = slice(k * half_st, (k + 1) * half_st)
        bure[:, cs] = jnp.dot(ub[:, rs], bbre[rs, cs], preferred_element_type=F32)
        buim[:, cs] = jnp.dot(ub[:, rs], bbim[rs, cs], preferred_element_type=F32)

    for cidx in range(D_STATE // SCAN_COLS):
        cs = slice(cidx * SCAN_COLS, (cidx + 1) * SCAN_COLS)
        ar = jnp.broadcast_to(abre[:, cs], (nb, SCAN_COLS))
        ai = jnp.broadcast_to(abim[:, cs], (nb, SCAN_COLS))

        def step(t, carry, cs=cs, ar=ar, ai=ai):
            hr, hi = carry
            rows = pl.ds(pl.multiple_of(t * nb, nb), nb)
            nr = ar * hr - ai * hi + bure[rows, cs]
            ni = ar * hi + ai * hr + buim[rows, cs]
            bure[rows, cs] = nr
            buim[rows, cs] = ni
            return nr, ni

        if not chained:
            hr, hi = lax.fori_loop(0, nt, step, (hre[:, cs], him[:, cs]), unroll=8)
            hre[:, cs] = hr
            him[:, cs] = hi
            continue

        zeros = jnp.zeros((nb, SCAN_COLS), F32)
        fr, fi = lax.fori_loop(0, nt, step, (zeros, zeros), unroll=8)
        ptr = pwre[nt - 1:nt, cs]
        pti = pwim[nt - 1:nt, cs]
        cr = hre[:, cs]
        ci = him[:, cs]
        seg = lax.broadcasted_iota(jnp.int32, (nb, SCAN_COLS), 0)
        inr, ini = zeros, zeros
        for s in range(nb):
            inr = jnp.where(seg == s, cr, inr)
            ini = jnp.where(seg == s, ci, ini)
            cr, ci = fr[s:s + 1, :] + ptr * cr - pti * ci, fi[s:s + 1, :] + ptr * ci + pti * cr
        hre[:, cs] = cr
        him[:, cs] = ci

        def fix(t, carry, cs=cs, inr=inr, ini=ini):
            rows = pl.ds(pl.multiple_of(t * nb, nb), nb)
            pr = pwre[pl.ds(t, 1), cs]
            pi = pwim[pl.ds(t, 1), cs]
            bure[rows, cs] += pr * inr - pi * ini
            buim[rows, cs] += pr * ini + pi * inr
            return carry

        lax.fori_loop(0, nt, fix, 0, unroll=8)

    halves = []
    for k in range(2):
        rs = slice(k * half_st, (k + 1) * half_st)
        cs = slice(k * half_in, (k + 1) * half_in)
        halves.append(jnp.dot(bure[:, rs].astype(BF16), cdre[rs, cs], preferred_element_type=F32)
                      - jnp.dot(buim[:, rs].astype(BF16), cdim[rs, cs], preferred_element_type=F32))
    y = jnp.concatenate(halves, axis=1) + d_ref[...] * u
    gl = 0.5 * y * (1.0 + jnp.tanh(math.sqrt(2.0 / math.pi) * (y + 0.044715 * (y * y * y))))
    gate = jax.nn.sigmoid(jnp.dot(gl.astype(BF16), wglu_ref[...], preferred_element_type=F32))
    o_ref[...] = _dot01(to_seq_major, gl * gate, 3)
    hre_out[...] = hre[...]
    him_out[...] = him[...]


def _s5(u_all, row_block0, nb, nt, n_steps, h0re, h0im, consts, chained):
    rows = nb * nt
    ns = 1 if chained else nb
    are, aim, ldt, btre, btim, ctre, ctim, dvec, wglu = consts
    in_specs = [pl.BlockSpec((rows, D_SSM), lambda i: (row_block0 + i, 0)),
                _const_spec((ns, D_STATE)), _const_spec((ns, D_STATE)),
                _const_spec((1, D_STATE)), _const_spec((1, D_STATE)), _const_spec((1, D_STATE)),
                _const_spec(btre.shape), _const_spec(btim.shape),
                _const_spec(ctre.shape), _const_spec(ctim.shape),
                _const_spec((1, D_SSM)), _const_spec((D_SSM, D_SSM))]
    args = [u_all, h0re, h0im, are, aim, ldt, btre, btim, ctre, ctim, dvec, wglu]
    state = jax.ShapeDtypeStruct((ns, D_STATE), F32)
    return pl.pallas_call(
        functools.partial(_s5_kernel, nb=nb, nt=nt, chained=chained),
        grid=(n_steps,),
        in_specs=in_specs,
        out_specs=(pl.BlockSpec((rows, D_SSM), lambda i: (i, 0)),
                   pl.BlockSpec((ns, D_STATE), lambda i: (0, 0)),
                   pl.BlockSpec((ns, D_STATE), lambda i: (0, 0))),
        out_shape=(jax.ShapeDtypeStruct((n_steps * rows, D_SSM), F32), state, state),
        scratch_shapes=[pltpu.VMEM((D_SSM, D_STATE), BF16), pltpu.VMEM((D_SSM, D_STATE), BF16),
                        pltpu.VMEM((D_STATE, D_SSM), BF16), pltpu.VMEM((D_STATE, D_SSM), BF16),
                        pltpu.VMEM((rows, D_STATE), F32), pltpu.VMEM((rows, D_STATE), F32),
                        pltpu.VMEM((ns, D_STATE), F32), pltpu.VMEM((ns, D_STATE), F32),
                        pltpu.VMEM((1, D_STATE), F32), pltpu.VMEM((1, D_STATE), F32),
                        pltpu.VMEM((nt, D_STATE), F32), pltpu.VMEM((nt, D_STATE), F32)],
        compiler_params=_params(("arbitrary",)),
        name="s5_prompt" if chained else "s5_sample",
    )(*args)


def _pool_kernel(cur_ref, *rest, seq_len, prev_rows, pos0, prompt):
    if prompt:
        prev_ref, buf_ref, pw_ref, ps_ref, o_ref = rest
    else:
        buf_ref, pw_ref, ps_ref, o_ref = rest
    i = pl.program_id(0)
    cur = cur_ref[...]
    n = cur.shape[0]
    if prompt:
        prev = jnp.where(i == 0, buf_ref[...], prev_ref[...])
    else:
        prev = buf_ref[...]
    npv = prev.shape[0]
    lsh = seq_len.bit_length() - 1
    psh = prev_rows.bit_length() - 1
    r = lax.broadcasted_iota(jnp.int32, (n, n), 0)
    c = lax.broadcasted_iota(jnp.int32, (n, n), 1)
    same = (r >> lsh) == (c >> lsh)
    d = r - c
    r1 = lax.broadcasted_iota(jnp.int32, (n, npv), 0)
    c1 = lax.broadcasted_iota(jnp.int32, (n, npv), 1)
    same1 = (r1 >> lsh) == (c1 >> psh)
    d1 = (r1 & (seq_len - 1)) + prev_rows - (c1 & (prev_rows - 1))
    local = lax.broadcasted_iota(jnp.int32, (n, 1), 0) & (seq_len - 1)
    pos = pos0 + i * n + local
    outs = []
    for gi, w in enumerate(POOL_WINDOWS):
        gs = slice(gi * POOL_GROUP, (gi + 1) * POOL_GROUP)
        band = jnp.where(same & (d >= 0) & (d < w), 1.0, 0.0).astype(F32)
        band_prev = jnp.where(same1 & (d1 < w), 1.0, 0.0).astype(F32)
        xg = cur[:, gs]
        wsum = _dot01(band, xg, 3) + _dot01(band_prev, prev[:, gs], 3)
        cnt = jnp.minimum(w, pos + 1).astype(F32)
        pooled = wsum / cnt - xg
        outs.append(jnp.dot(pooled.astype(BF16), pw_ref[gi], preferred_element_type=F32))
    o_ref[...] = jnp.concatenate(outs, axis=1) * ps_ref[...]


def _pool_prompt(p_all, buf_pad, pw, ps, t):
    tt = ROW_TILE
    ratio = tt // POOL_GROUP
    return pl.pallas_call(
        functools.partial(_pool_kernel, seq_len=tt, prev_rows=POOL_GROUP, pos0=0, prompt=True),
        grid=(t // tt,),
        in_specs=[pl.BlockSpec((tt, D_POOL), lambda i: (i, 0)),
                  pl.BlockSpec((POOL_GROUP, D_POOL), lambda i: (jnp.maximum(i * ratio - 1, 0), 0)),
                  _const_spec((POOL_GROUP, D_POOL)), _const_spec(pw.shape), _const_spec((1, D_POOL))],
        out_specs=pl.BlockSpec((tt, D_POOL), lambda i: (i, 0)),
        out_shape=jax.ShapeDtypeStruct((t, D_POOL), F32),
        compiler_params=_params(("parallel",)),
        name="pool_prompt",
    )(p_all, p_all, buf_pad, pw, ps)


def _pool_sample(p_all, buf_pad, pw, ps, t, nb, tn, past):
    rows = nb * tn
    blk = t // rows
    return pl.pallas_call(
        functools.partial(_pool_kernel, seq_len=tn, prev_rows=16, pos0=past, prompt=False),
        grid=(1,),
        in_specs=[pl.BlockSpec((rows, D_POOL), lambda i: (blk, 0)),
                  _const_spec(buf_pad.shape), _const_spec(pw.shape), _const_spec((1, D_POOL))],
        out_specs=pl.BlockSpec((rows, D_POOL), lambda i: (0, 0)),
        out_shape=jax.ShapeDtypeStruct((rows, D_POOL), F32),
        compiler_params=_params(("arbitrary",)),
        name="pool_sample",
    )(p_all, buf_pad, pw, ps)


def _outproj_kernel(x_ref, ap_ref, as_ref, sp_ref, ss_ref, pp_ref, ps_ref, ong_ref, wout_ref, n2_ref,
                    wrh_ref, wrl_ref, rb_ref, xm_ref, xn_ref, ids_ref, gates_ref, *, prompt_tiles):
    g = ong_ref[...]
    sample = pl.program_id(0) >= prompt_tiles
    ma = _rms(jnp.where(sample, as_ref[...], ap_ref[...]), g[:, :D_ATTN]).astype(BF16)
    ms = _rms(jnp.where(sample, ss_ref[...], sp_ref[...]), g[:, D_ATTN:D_ATTN + D_SSM]).astype(BF16)
    mp = _rms(jnp.where(sample, ps_ref[...], pp_ref[...]), g[:, D_ATTN + D_SSM:]).astype(BF16)
    xm = (x_ref[...]
          + jnp.dot(ma, wout_ref[:D_ATTN, :], preferred_element_type=F32)
          + jnp.dot(ms, wout_ref[D_ATTN:D_ATTN + D_SSM, :], preferred_element_type=F32)
          + jnp.dot(mp, wout_ref[D_ATTN + D_SSM:, :], preferred_element_type=F32))
    xm_ref[...] = xm
    xn = _rms(xm, n2_ref[...])
    tm = xn.shape[0]
    for j in range(xn.shape[1] // ROW_CHUNK):
        xn_ref[_chunk_rows(j, tm), :] = xn[:, j * ROW_CHUNK:(j + 1) * ROW_CHUNK]
    xh = xn.astype(BF16)
    xl = (xn - xh.astype(F32)).astype(BF16)
    logits = (jnp.dot(xh, wrh_ref[...], preferred_element_type=F32)
              + jnp.dot(xl, wrh_ref[...], preferred_element_type=F32)
              + jnp.dot(xh, wrl_ref[...], preferred_element_type=F32))
    lt = jnp.transpose(logits)[:rb_ref.shape[0]] + rb_ref[...]
    ng, ne = N_EXPERT_GROUPS, EXPERTS_PER_GROUP
    gl = [lt[j:j + 1, :] for j in range(ng)]
    gmax = functools.reduce(jnp.maximum, gl)
    gsel = jnp.full(gmax.shape, ng - 1, jnp.int32)
    for j in range(ng - 2, -1, -1):
        gsel = jnp.where(gl[j] == gmax, j, gsel)
    gw = 1.0 / functools.reduce(jnp.add, [jnp.exp(v - gmax) for v in gl])
    es = []
    for j in range(ne):
        v = lt[ng + (ng - 1) * ne + j:ng + (ng - 1) * ne + j + 1, :]
        for gi in range(ng - 2, -1, -1):
            v = jnp.where(gsel == gi, lt[ng + gi * ne + j:ng + gi * ne + j + 1, :], v)
        es.append(v)

    def first_argmax(vals):
        vmax = functools.reduce(jnp.maximum, vals)
        idx = jnp.full(vmax.shape, len(vals) - 1, jnp.int32)
        for j in range(len(vals) - 2, -1, -1):
            idx = jnp.where(vals[j] == vmax, j, idx)
        return vmax, idx

    v1, i1 = first_argmax(es)
    v2, i2 = first_argmax([jnp.where(i1 == j, -jnp.inf, es[j]) for j in range(ne)])
    t = jnp.exp(v2 - v1)
    p1 = 1.0 / (1.0 + t)
    p2 = t / (1.0 + t)
    row = lax.broadcasted_iota(jnp.int32, ids_ref.shape, 0)
    ids_ref[...] = jnp.where(row == 0, gsel * ne + i1, jnp.where(row == 1, gsel * ne + i2, 0))
    gates_ref[...] = jnp.where(row == 0, p1 * gw, jnp.where(row == 1, p2 * gw, 0.0))


def _outproj(x, mix_prompt, mix_sample, layer, ong, wout, n2, wr_hi, wr_lo, rb):
    m, d = x.shape
    tm = ROW_TILE
    pt = mix_prompt[0].shape[0] // tm
    row = lambda w: pl.BlockSpec((tm, w), lambda i: (i, 0))
    prow = lambda w: pl.BlockSpec((tm, w), lambda i: (jnp.minimum(i, pt - 1), 0))
    srow = lambda w: pl.BlockSpec((tm, w), lambda i: (jnp.maximum(i - pt, 0), 0))
    col = pl.BlockSpec((8, tm), lambda i: (0, i))
    return pl.pallas_call(
        functools.partial(_outproj_kernel, prompt_tiles=pt),
        grid=(m // tm,),
        in_specs=[row(d), prow(D_ATTN), srow(D_ATTN), prow(D_SSM), srow(D_SSM), prow(D_POOL), srow(D_POOL),
                  _const_spec((1, d)),
                  _layer_spec(wout, layer), _const_spec((1, d)), _const_spec(wr_hi.shape),
                  _const_spec(wr_lo.shape), _const_spec(rb.shape)],
        out_specs=(row(d), pl.BlockSpec((tm * ROW_CHUNKS, ROW_CHUNK), lambda i: (i, 0)), col, col),
        out_shape=(jax.ShapeDtypeStruct((m, d), F32), jax.ShapeDtypeStruct((m * ROW_CHUNKS, ROW_CHUNK), F32),
                   jax.ShapeDtypeStruct((8, m), jnp.int32), jax.ShapeDtypeStruct((8, m), F32)),
        compiler_params=_params(("parallel",)),
        name="outproj_router",
    )(x, mix_prompt[0], mix_sample[0], mix_prompt[1], mix_sample[1], mix_prompt[2], mix_sample[2],
      ong, wout, n2, wr_hi, wr_lo, rb)


def _token_copy(src_hbm, tok, dst, r, sem):
    return pltpu.make_async_copy(src_hbm.at[pl.ds(pl.multiple_of(tok * ROW_CHUNKS, ROW_CHUNKS), ROW_CHUNKS), :],
                                 dst.at[pl.ds(pl.multiple_of(r * ROW_CHUNKS, ROW_CHUNKS), ROW_CHUNKS), :], sem)


def _gather_tokens(src_hbm, idx_ref, first_idx, dst, sem, n):
    def issue(r, carry):
        _token_copy(src_hbm, idx_ref[first_idx + r], dst, r, sem).start()
        return carry

    lax.fori_loop(0, n, issue, 0, unroll=8)


def _wait_tokens(src_hbm, dst, sem):
    pltpu.make_async_copy(src_hbm.at[pl.ds(0, dst.shape[0]), :], dst, sem).wait()


def _expert_kernel(vt_ref, ve_ref, vlo_ref, vhi_ref, tok_ref, x_hbm, gate_ref, wg_ref, wu_ref, wd_ref,
                   o_ref, xs, wgb, wub, wdb, sem, *, n_tiles):
    v = pl.program_id(0)
    tm = gate_ref.shape[0]
    lo = vlo_ref[v]
    hi = vhi_ref[v]
    tile = vt_ref[v]
    base = tile * tm
    slot = tile & 1
    nonempty = hi > lo
    first = jnp.logical_and(nonempty, lo == base)
    changed = jnp.logical_or(v == 0, ve_ref[v] != ve_ref[jnp.maximum(v - 1, 0)])

    @pl.when(changed)
    def _():
        wgb[...] = wg_ref[...].astype(BF16)
        wub[...] = wu_ref[...].astype(BF16)
        wdb[...] = wd_ref[...].astype(BF16)

    @pl.when(v == 0)
    def _():
        _gather_tokens(x_hbm, tok_ref, 0, xs.at[0], sem.at[0], tm)

    @pl.when(first)
    def _():
        _wait_tokens(x_hbm, xs.at[slot], sem.at[slot])

    @pl.when(jnp.logical_and(first, tile + 1 < n_tiles))
    def _():
        _gather_tokens(x_hbm, tok_ref, base + tm, xs.at[1 - slot], sem.at[1 - slot], tm)

    @pl.when(nonempty)
    def _():
        xt = xs.at[slot]
        xb = jnp.concatenate([xt[_chunk_rows(j, tm), :] for j in range(ROW_CHUNKS)], axis=1).astype(BF16)
        hg = jnp.dot(xb, wgb[...], preferred_element_type=F32)
        hu = jnp.dot(xb, wub[...], preferred_element_type=F32)
        row = base + lax.broadcasted_iota(jnp.int32, (tm, 1), 0)
        keep = jnp.logical_and(row >= lo, row < hi)
        act = jnp.where(keep, (hg * jax.nn.sigmoid(hg)) * hu * gate_ref[...], 0.0)
        y = jnp.dot(act.astype(BF16), wdb[...], preferred_element_type=F32)

        @pl.when(first)
        def _():
            for j in range(ROW_CHUNKS):
                o_ref[_chunk_rows(j, tm), :] = y[:, j * ROW_CHUNK:(j + 1) * ROW_CHUNK]

        @pl.when(jnp.logical_not(first))
        def _():
            for j in range(ROW_CHUNKS):
                o_ref[_chunk_rows(j, tm), :] += y[:, j * ROW_CHUNK:(j + 1) * ROW_CHUNK]


def _expert_ffn(plan, xn_rows, layer, wg, wu, wd):
    v_tile, v_exp, v_lo, v_hi, tok_sorted, gate_sorted = plan
    n_flat = tok_sorted.shape[0]
    d = wg.shape[2]
    tm = ROW_TILE
    grid_spec = pltpu.PrefetchScalarGridSpec(
        num_scalar_prefetch=5,
        grid=(v_tile.shape[0],),
        in_specs=[pl.BlockSpec(memory_space=pl.ANY),
                  pl.BlockSpec((tm, 1), lambda v, vt, ve, lo, hi, tok: (vt[v], 0)),
                  pl.BlockSpec((None, None, d, D_EXPERT), lambda v, vt, ve, lo, hi, tok: (layer, ve[v], 0, 0)),
                  pl.BlockSpec((None, None, d, D_EXPERT), lambda v, vt, ve, lo, hi, tok: (layer, ve[v], 0, 0)),
                  pl.BlockSpec((None, None, D_EXPERT, d), lambda v, vt, ve, lo, hi, tok: (layer, ve[v], 0, 0))],
        out_specs=pl.BlockSpec((tm * ROW_CHUNKS, ROW_CHUNK), lambda v, vt, ve, lo, hi, tok: (vt[v], 0)),
        scratch_shapes=[pltpu.VMEM((2, tm * ROW_CHUNKS, ROW_CHUNK), F32),
                        pltpu.VMEM((d, D_EXPERT), BF16), pltpu.VMEM((d, D_EXPERT), BF16),
                        pltpu.VMEM((D_EXPERT, d), BF16), pltpu.SemaphoreType.DMA((2,))])
    return pl.pallas_call(
        functools.partial(_expert_kernel, n_tiles=n_flat // tm),
        grid_spec=grid_spec,
        out_shape=jax.ShapeDtypeStruct((n_flat * ROW_CHUNKS, ROW_CHUNK), F32),
        compiler_params=_params(("arbitrary",)),
        name="expert_ffn",
    )(v_tile, v_exp, v_lo, v_hi, tok_sorted, xn_rows, gate_sorted, wg, wu, wd)


def _combine_kernel(d1_ref, d2_ref, xm_ref, ys_hbm, *rest, n_tiles, prompt_tiles):
    out_refs, (buf1, buf2, sem) = rest[:-3], rest[-3:]
    i = pl.program_id(0)
    tm = xm_ref.shape[0]
    slot = i & 1

    def start(tile, s):
        _gather_tokens(ys_hbm, d1_ref, tile * tm, buf1.at[s], sem.at[0, s], tm)
        _gather_tokens(ys_hbm, d2_ref, tile * tm, buf2.at[s], sem.at[1, s], tm)

    pl.when(i == 0)(lambda: start(0, 0))
    _wait_tokens(ys_hbm, buf1.at[slot], sem.at[0, slot])
    _wait_tokens(ys_hbm, buf2.at[slot], sem.at[1, slot])
    pl.when(i + 1 < n_tiles)(lambda: start(i + 1, 1 - slot))
    rows1 = buf1.at[slot]
    rows2 = buf2.at[slot]

    def write(o_ref):
        for j in range(ROW_CHUNKS):
            cs = slice(j * ROW_CHUNK, (j + 1) * ROW_CHUNK)
            o_ref[:, cs] = xm_ref[:, cs] + rows1[_chunk_rows(j, tm), :] + rows2[_chunk_rows(j, tm), :]

    if prompt_tiles is None:
        write(out_refs[0])
    else:
        pl.when(i < prompt_tiles)(lambda: write(out_refs[0]))
        pl.when(i >= prompt_tiles)(lambda: write(out_refs[1]))


def _combine(dest1, dest2, xm, ys_rows, split_rows=None):
    m, d = xm.shape
    tm = ROW_TILE
    row = pl.BlockSpec((tm, d), lambda i, a, b: (i, 0))
    if split_rows is None:
        pt = None
        out_specs = row
        out_shape = jax.ShapeDtypeStruct((m, d), F32)
    else:
        pt = split_rows // tm
        out_specs = (pl.BlockSpec((tm, d), lambda i, a, b: (jnp.minimum(i, pt - 1), 0)),
                     pl.BlockSpec((tm, d), lambda i, a, b: (jnp.maximum(i - pt, 0), 0)))
        out_shape = (jax.ShapeDtypeStruct((split_rows, d), F32), jax.ShapeDtypeStruct((m - split_rows, d), F32))
    buf = pltpu.VMEM((2, tm * ROW_CHUNKS, ROW_CHUNK), F32)
    grid_spec = pltpu.PrefetchScalarGridSpec(
        num_scalar_prefetch=2,
        grid=(m // tm,),
        in_specs=[row, pl.BlockSpec(memory_space=pl.ANY)],
        out_specs=out_specs,
        scratch_shapes=[buf, buf, pltpu.SemaphoreType.DMA((2, 2))])
    return pl.pallas_call(
        functools.partial(_combine_kernel, n_tiles=m // tm, prompt_tiles=pt),
        grid_spec=grid_spec,
        out_shape=out_shape,
        compiler_params=_params(("arbitrary",)),
        name="moe_combine",
    )(dest1, dest2, xm, ys_rows)


def _dispatch_plan(ids, gates, m):
    tm = ROW_TILE
    n_flat = 2 * m
    n_tiles = n_flat // tm
    e_flat = ids[:2].reshape(n_flat)
    g_flat = gates[:2].reshape(n_flat)
    tok = jnp.arange(n_flat, dtype=jnp.int32) % m
    _, tok_sorted, gate_sorted = lax.sort((e_flat, tok, g_flat), num_keys=1, is_stable=True)
    onehot = (e_flat[:, None] == jnp.arange(N_EXPERTS, dtype=jnp.int32)[None, :]).astype(jnp.int32)
    csum = jnp.cumsum(onehot, axis=0)
    counts = csum[-1]
    row_end = jnp.cumsum(counts)
    row_start = row_end - counts
    dest = jnp.sum(onehot * (row_start[None, :] + csum - 1), axis=1).astype(jnp.int32)
    cuts = jnp.sort(jnp.concatenate([jnp.arange(n_tiles, dtype=jnp.int32) * tm, row_start[1:].astype(jnp.int32)]))
    v_lo = cuts
    v_hi = jnp.concatenate([cuts[1:], jnp.full((1,), n_flat, jnp.int32)])
    v_tile = jnp.minimum(v_lo // tm, n_tiles - 1).astype(jnp.int32)
    v_exp = jnp.minimum(jnp.sum((v_lo[:, None] >= row_end[None, :]).astype(jnp.int32), axis=1), N_EXPERTS - 1)
    plan = (v_tile, v_exp.astype(jnp.int32), v_lo, v_hi, tok_sorted, gate_sorted.reshape(n_flat, 1))
    return plan, dest[:m], dest[m:]


def _tall_blocks(w):
    g, c, r = w.shape
    tall = jnp.swapaxes(w, 1, 2).reshape(g * r, c)
    return jnp.tile(tall, (1, 128 // c))


def kernel(x_prompt, x_sample, cache_k, cache_v, cache_logf, page_table, state_ssm_re, state_ssm_im, state_pool, norm1_g, w_in, f_bias, q_gain, k_gain, ssm_a_re, ssm_a_im, ssm_log_dt, ssm_b_re, ssm_b_im, ssm_c_re, ssm_c_im, ssm_d, ssm_w_glu, pool_w, pool_scale, out_norm_g, w_out, norm2_g, router_group_w, router_group_b, router_expert_w, router_expert_b, moe_w_gate, moe_w_up, moe_w_down):
    n_prompt, t, d = x_prompt.shape
    nb, tn, _ = x_sample.shape
    depth = w_in.shape[0]
    n_phys = cache_k.shape[1]
    past = page_table.shape[1] * cache_k.shape[2]
    assert n_prompt == 1
    ms = nb * tn
    m = t + ms
    x = jnp.concatenate([x_prompt.reshape(t, d), x_sample.reshape(ms, d)], axis=0)

    ck = cache_k.reshape(depth * n_phys, PAGE_SIZE * N_HEADS, HEAD_DIM)
    cv = cache_v.reshape(depth * n_phys, PAGE_SIZE * N_HEADS, HEAD_DIM)
    clf = jnp.swapaxes(cache_logf, 2, 3).reshape(depth * n_phys, N_HEADS, PAGE_SIZE)
    zero_state = jnp.zeros((1, D_STATE), F32)
    zero_buf = jnp.zeros((POOL_GROUP, D_POOL), F32)

    wqkv = w_in[:, :, :3 * D_ATTN].astype(BF16)
    wup = w_in[:, :, 3 * D_ATTN + N_HEADS:].astype(BF16)
    wft = jnp.pad(jnp.swapaxes(w_in[:, :, 3 * D_ATTN:3 * D_ATTN + N_HEADS], 1, 2), ((0, 0), (0, 8), (0, 0))).astype(BF16)
    wout = w_out.astype(BF16)

    outs_p, outs_s = [], []
    for l in range(depth):
        q, k_p, v_p, k_s, v_s, kb, vb, lft, u, p_in = _proj(
            x, t, l, norm1_g[l].reshape(1, d), wqkv, wup, wft, f_bias[l].reshape(N_HEADS, 1),
            q_gain[l].reshape(1, HEAD_DIM), k_gain[l].reshape(1, HEAD_DIM))

        c_prompt = _prompt_cumsum(lft, t)
        qk_bound = 1.02 * (HEAD_DIM ** 0.5) * jnp.max(jnp.abs(q_gain[l])) * jnp.max(jnp.abs(k_gain[l]))
        a_p = _fox_prompt(q, kb, vb, c_prompt, t, qk_bound)
        lfn = lft[:, t:].reshape(N_HEADS, nb, tn).transpose(1, 0, 2)
        a_s = _fox_sample(q, k_s, v_s, lfn, ck, cv, clf, page_table, l, n_phys, t)

        consts = (ssm_a_re[l].reshape(1, D_STATE), ssm_a_im[l].reshape(1, D_STATE),
                  jnp.repeat(ssm_log_dt[l], SSM_STATE).reshape(1, D_STATE),
                  _tall_blocks(ssm_b_re[l]), _tall_blocks(ssm_b_im[l]),
                  _tall_blocks(ssm_c_re[l]), _tall_blocks(ssm_c_im[l]),
                  ssm_d[l].reshape(1, D_SSM), ssm_w_glu[l].astype(BF16))
        s_p, hre_p, him_p = _s5(u, 0, SCAN_SEGMENTS, ROW_TILE // SCAN_SEGMENTS, t // ROW_TILE, zero_state,
                                zero_state, consts, True)
        s_s, hre_s, him_s = _s5(u, t // ms, nb, tn, 1, state_ssm_re[l].reshape(nb, D_STATE),
                                state_ssm_im[l].reshape(nb, D_STATE), consts, False)

        pw = pool_w[l].astype(BF16)
        ps = pool_scale[l].reshape(1, D_POOL)
        po_p = _pool_prompt(p_in, zero_buf, pw, ps, t)
        buf_s = jnp.pad(state_pool[l], ((0, 0), (1, 0), (0, 0))).reshape(nb * 16, D_POOL)
        po_s = _pool_sample(p_in, buf_s, pw, ps, t, nb, tn, past)

        n_route = N_EXPERT_GROUPS + N_EXPERTS
        wr = jnp.pad(jnp.concatenate([router_group_w[l], router_expert_w[l]], axis=1), ((0, 0), (0, 128 - n_route)))
        rb = jnp.pad(jnp.concatenate([router_group_b[l], router_expert_b[l]]), (0, 24 - n_route)).reshape(24, 1)
        wr_hi = wr.astype(BF16)
        wr_lo = (wr - wr_hi.astype(F32)).astype(BF16)
        xm, xn, ids, gates = _outproj(x, (a_p, s_p, po_p), (a_s, s_s, po_s), l, out_norm_g[l].reshape(1, d),
                                      wout, norm2_g[l].reshape(1, d), wr_hi, wr_lo, rb)

        plan, dest1, dest2 = _dispatch_plan(ids, gates, m)
        ys = _expert_ffn(plan, xn, l, moe_w_gate, moe_w_up, moe_w_down)
        if l + 1 < depth:
            x = _combine(dest1, dest2, xm, ys)
        else:
            y_prompt, y_sample = _combine(dest1, dest2, xm, ys, split_rows=t)

        lf_rows = lft.T
        outs_p.append((k_p.reshape(1, t, N_HEADS, HEAD_DIM), v_p.reshape(1, t, N_HEADS, HEAD_DIM),
                       lf_rows[:t].reshape(1, t, N_HEADS),
                       hre_p.reshape(1, SSM_GROUPS, SSM_STATE), him_p.reshape(1, SSM_GROUPS, SSM_STATE),
                       p_in[t - POOL_BUF:t].reshape(1, POOL_BUF, D_POOL)))
        p_s = p_in[t:].reshape(nb, tn, D_POOL)
        outs_s.append((k_s.reshape(nb, tn, N_HEADS, HEAD_DIM), v_s.reshape(nb, tn, N_HEADS, HEAD_DIM),
                       lf_rows[t:].reshape(nb, tn, N_HEADS),
                       hre_s.reshape(nb, SSM_GROUPS, SSM_STATE), him_s.reshape(nb, SSM_GROUPS, SSM_STATE),
                       jnp.concatenate([state_pool[l], p_s], axis=1)[:, -POOL_BUF:]))

    stack = lambda outs, i: jnp.stack([o[i] for o in outs], axis=0)
    return (y_prompt.reshape(1, t, d), y_sample.reshape(nb, tn, d),
            *[stack(outs_p, i) for i in range(6)], *[stack(outs_s, i) for i in range(6)])
```

```python
import functools
import math

import jax
import jax.numpy as jnp
from jax import lax
from jax.experimental import pallas as pl
from jax.experimental.pallas import tpu as pltpu

F32 = jnp.float32
BF16 = jnp.bfloat16
HIGHEST = lax.Precision.HIGHEST
NT_DIMS = (((1,), (1,)), ((), ()))

EPS = 1e-6
HEAD_DIM = 128
N_HEADS = 8
D_ATTN = N_HEADS * HEAD_DIM
D_SSM = 512
SSM_GROUPS = 32
SSM_GROUP = 16
SSM_STATE = 64
D_STATE = SSM_GROUPS * SSM_STATE
D_POOL = 512
POOL_WINDOWS = (2, 4, 8, 16)
POOL_GROUP = 128
POOL_BUF = 15
N_EXPERT_GROUPS = 4
EXPERTS_PER_GROUP = 4
N_EXPERTS = 16
D_EXPERT = 512
PAGE_SIZE = 128
NEG_BIG = -1e30
SKIP_MARGIN = 30.0

ROW_TILE = 256
ROW_CHUNK = 128
ROW_CHUNKS = 2048 // ROW_CHUNK
ATTN_TILE = 512
PAGES_PER_STEP = 16
SCAN_COLS = 512
SCAN_SEGMENTS = 8
VMEM_LIMIT = 56 * 1024 * 1024


def _params(sem, vmem=VMEM_LIMIT):
    return pltpu.CompilerParams(dimension_semantics=sem, vmem_limit_bytes=vmem)


def _const_spec(shape):
    nd = len(shape)
    return pl.BlockSpec(shape, lambda *_: (0,) * nd, pipeline_mode=pl.Buffered(1))


def _layer_spec(stacked, layer):
    nd = stacked.ndim - 1
    return pl.BlockSpec((None,) + stacked.shape[1:], lambda *_: (layer,) + (0,) * nd, pipeline_mode=pl.Buffered(1))


def _rms(y, g):
    return y * lax.rsqrt(jnp.mean(y * y, axis=-1, keepdims=True) + EPS) * g


def _dot01(mat01, x, pieces):
    m = mat01.astype(BF16)
    acc = None
    rem = x
    for _ in range(pieces):
        part = rem.astype(BF16)
        rem = rem - part.astype(F32)
        d = jnp.dot(m, part, preferred_element_type=F32)
        acc = d if acc is None else acc + d
    return acc


def _chunk_rows(j, n):
    return pl.ds(j, n, stride=ROW_CHUNKS)


def _head_rows(h, n):
    return pl.ds(h, n, stride=N_HEADS)


def _proj_kernel(x_ref, g1_ref, wqkv_ref, wup_ref, wft_ref, fb_ref, qg_ref, kg_ref,
                 q_ref, kp_ref, vp_ref, ks_ref, vs_ref, kb_ref, vb_ref, lf_ref, u_ref, p_ref):
    hb = _rms(x_ref[...], g1_ref[...]).astype(BF16)
    tm = hb.shape[0]
    qg = qg_ref[...]
    kg = kg_ref[...]
    k_heads, v_heads = [], []
    for c in range(D_ATTN // 256):
        lo = c * 256
        qc = jnp.dot(hb, wqkv_ref[:, lo:lo + 256], preferred_element_type=F32)
        kc = jnp.dot(hb, wqkv_ref[:, D_ATTN + lo:D_ATTN + lo + 256], preferred_element_type=F32)
        vc = jnp.dot(hb, wqkv_ref[:, 2 * D_ATTN + lo:2 * D_ATTN + lo + 256], preferred_element_type=F32)
        for j in range(2):
            a = lo + j * HEAD_DIM
            q_ref[:, a:a + HEAD_DIM] = _rms(qc[:, j * HEAD_DIM:(j + 1) * HEAD_DIM], qg)
            kn = _rms(kc[:, j * HEAD_DIM:(j + 1) * HEAD_DIM], kg)
            kb_ref[:, a:a + HEAD_DIM] = kn.astype(BF16)
            k_heads.append(kn)
            v_heads.append(vc[:, j * HEAD_DIM:(j + 1) * HEAD_DIM])
        vb_ref[:, lo:lo + 256] = vc.astype(BF16)

    for h in range(N_HEADS):
        for k_ref, v_ref in ((kp_ref, vp_ref), (ks_ref, vs_ref)):
            k_ref[_head_rows(h, tm), :] = k_heads[h]
            v_ref[_head_rows(h, tm), :] = v_heads[h]
    u_ref[...] = jnp.dot(hb, wup_ref[:, :D_SSM], preferred_element_type=F32)
    p_ref[...] = jnp.dot(hb, wup_ref[:, D_SSM:], preferred_element_type=F32)
    ft = lax.dot_general(wft_ref[...], hb, NT_DIMS, preferred_element_type=F32)[:N_HEADS]
    z = ft + fb_ref[...]
    lf_ref[...] = jnp.minimum(z, 0.0) - jnp.log1p(jnp.exp(-jnp.abs(z)))


def _proj(x, t, layer, g1, wqkv, wup, wft, fb, qg, kg):
    m, d = x.shape
    tm = ROW_TILE
    pt = t // tm
    st = (m - t) // tm
    tile = lambda i: jnp.where(i < st, pt + i, i - st)
    row = lambda w: pl.BlockSpec((tm, w), lambda i: (tile(i), 0))
    heads_p = pl.BlockSpec((tm * N_HEADS, HEAD_DIM), lambda i: (jnp.maximum(i - st, 0), 0))
    heads_s = pl.BlockSpec((tm * N_HEADS, HEAD_DIM), lambda i: (jnp.minimum(i, st), 0))
    kv_p = jax.ShapeDtypeStruct((t * N_HEADS, HEAD_DIM), F32)
    kv_s = jax.ShapeDtypeStruct(((m - t + tm) * N_HEADS, HEAD_DIM), F32)
    outs = (
        jax.ShapeDtypeStruct((m, D_ATTN), F32), kv_p, kv_p, kv_s, kv_s, jax.ShapeDtypeStruct((m, D_ATTN), BF16),
        jax.ShapeDtypeStruct((m, D_ATTN), BF16), jax.ShapeDtypeStruct((N_HEADS, m), F32),
        jax.ShapeDtypeStruct((m, D_SSM), F32), jax.ShapeDtypeStruct((m, D_POOL), F32))
    return pl.pallas_call(
        _proj_kernel,
        grid=(m // tm,),
        in_specs=[row(d), _const_spec((1, d)), _layer_spec(wqkv, layer), _layer_spec(wup, layer),
                  _layer_spec(wft, layer), _const_spec((N_HEADS, 1)), _const_spec((1, HEAD_DIM)),
                  _const_spec((1, HEAD_DIM))],
        out_specs=(row(D_ATTN), heads_p, heads_p, heads_s, heads_s, row(D_ATTN), row(D_ATTN),
                   pl.BlockSpec((N_HEADS, tm), lambda i: (0, tile(i))), row(D_SSM), row(D_POOL)),
        out_shape=outs,
        compiler_params=_params(("arbitrary",)),
        name="proj",
    )(x, g1, wqkv, wup, wft, fb, qg, kg)


def _cumsum_kernel(lf_ref, c_ref):
    t = c_ref.shape[1]
    w = ATTN_TILE
    r = lax.broadcasted_iota(jnp.int32, (w, w), 0)
    c = lax.broadcasted_iota(jnp.int32, (w, w), 1)
    upper = jnp.where(r <= c, 1.0, 0.0).astype(F32)
    carry = jnp.zeros((N_HEADS, 1), F32)
    for s in range(t // w):
        blk = jnp.dot(lf_ref[:, s * w:(s + 1) * w], upper, precision=HIGHEST,
                      preferred_element_type=F32) + carry
        c_ref[:, s * w:(s + 1) * w] = blk
        carry = blk[:, w - 1:w]


def _prompt_cumsum(lft, t):
    return pl.pallas_call(
        _cumsum_kernel,
        grid=(1,),
        in_specs=[pl.BlockSpec((N_HEADS, t), lambda i: (0, 0))],
        out_specs=pl.BlockSpec((N_HEADS, t), lambda i: (0, 0)),
        out_shape=jax.ShapeDtypeStruct((N_HEADS, t), F32),
        compiler_params=_params(("arbitrary",)),
        name="prompt_cumsum",
    )(lft)


def _fox_prompt_kernel(kstart_ref, q_ref, k_ref, v_ref, c_ref, o_ref):
    qi = pl.program_id(1)
    first_tile = kstart_ref[pl.program_id(0), qi]
    tq = q_ref.shape[0]
    q = (q_ref[...] * (HEAD_DIM ** -0.5)).astype(BF16)

    def tile(ki, carry, masked):
        m, l, acc = carry
        off = pl.multiple_of(ki * tq, tq)
        kt = k_ref[pl.ds(off, tq), :]
        vt = v_ref[pl.ds(off, tq), :]
        s = lax.dot_general(q, kt, NT_DIMS, preferred_element_type=F32) - c_ref[ki]
        if masked:
            r = lax.broadcasted_iota(jnp.int32, s.shape, 0)
            c = lax.broadcasted_iota(jnp.int32, s.shape, 1)
            s = jnp.where(c <= r, s, NEG_BIG)
        m_new = jnp.maximum(m, jnp.max(s, axis=-1, keepdims=True))
        alpha = jnp.exp(m - m_new)
        p = jnp.exp(s - m_new)
        l = alpha * l + jnp.sum(p, axis=-1, keepdims=True)
        acc = alpha * acc + jnp.dot(p.astype(BF16), vt, preferred_element_type=F32)
        return m_new, l, acc

    init = (jnp.full((tq, 1), NEG_BIG, F32), jnp.zeros((tq, 1), F32), jnp.zeros((tq, HEAD_DIM), F32))
    carry = lax.fori_loop(first_tile, qi, lambda ki, cr: tile(ki, cr, False), init)
    _, l, acc = tile(qi, carry, True)
    o_ref[...] = acc / l


def _negligible_key_tiles(c, qk_bound, tq):
    c_first = c[:, ::tq]
    c_last = c[:, tq - 1::tq]
    nq = c_first.shape[1]
    gap = c_first[:, :, None] - c_last[:, None, :]
    earlier = jnp.arange(nq)[None, :] < jnp.arange(nq)[:, None]
    skip = (gap < -(2.0 * qk_bound + SKIP_MARGIN)) & earlier[None]
    return jnp.sum(skip.astype(jnp.int32), axis=2)


def _fox_prompt(q, kb, vb, c, t, qk_bound):
    tq = ATTN_TILE
    nq = t // tq
    c4 = c.reshape(N_HEADS, nq, 1, tq)
    kstart = _negligible_key_tiles(c, qk_bound, tq)
    grid_spec = pltpu.PrefetchScalarGridSpec(
        num_scalar_prefetch=1,
        grid=(N_HEADS, nq),
        in_specs=[pl.BlockSpec((tq, HEAD_DIM), lambda h, i, ks: (i, h)),
                  pl.BlockSpec((t, HEAD_DIM), lambda h, i, ks: (0, h)),
                  pl.BlockSpec((t, HEAD_DIM), lambda h, i, ks: (0, h)),
                  pl.BlockSpec((None, nq, 1, tq), lambda h, i, ks: (h, 0, 0, 0))],
        out_specs=pl.BlockSpec((tq, HEAD_DIM), lambda h, i, ks: (i, h)))
    return pl.pallas_call(
        _fox_prompt_kernel,
        grid_spec=grid_spec,
        out_shape=jax.ShapeDtypeStruct((t, D_ATTN), F32),
        compiler_params=_params(("parallel", "parallel")),
        name="fox_prompt",
    )(kstart, q, kb, vb, c4)


def _fox_sample_kernel(pt_ref, q_ref, kn_ref, vn_ref, lfn_ref, *rest, pg):
    del pt_ref
    k_refs = rest[:pg]
    v_refs = rest[pg:2 * pg]
    lf_refs = rest[2 * pg:3 * pg]
    o_ref = rest[3 * pg]
    m_scr, l_scr, acc_scr, c_scr = rest[3 * pg + 1:]
    g = pl.program_id(1)
    tn = q_ref.shape[0]
    scale = HEAD_DIM ** -0.5

    @pl.when(g == 0)
    def _():
        m_scr[...] = jnp.full(m_scr.shape, NEG_BIG, F32)
        l_scr[...] = jnp.zeros(l_scr.shape, F32)
        acc_scr[...] = jnp.zeros(acc_scr.shape, F32)
        c_scr[...] = jnp.zeros(c_scr.shape, F32)

    r = lax.broadcasted_iota(jnp.int32, (PAGE_SIZE, PAGE_SIZE), 0)
    c = lax.broadcasted_iota(jnp.int32, (PAGE_SIZE, PAGE_SIZE), 1)
    upper = jnp.where(r <= c, 1.0, 0.0).astype(BF16)
    lf = jnp.concatenate([lf_refs[j][...] for j in range(pg)], axis=0)
    hi = lf.astype(BF16).astype(F32)
    rem = lf - hi
    mid = rem.astype(BF16).astype(F32)
    pieces = jnp.concatenate([hi, mid, rem - mid], axis=0).astype(BF16)
    cs = jnp.dot(pieces, upper, preferred_element_type=F32)
    n = pg * N_HEADS
    c_in = cs[:n] + cs[n:2 * n] + cs[2 * n:]
    carry = c_scr[...]
    c_pages = []
    for j in range(pg):
        cp = c_in[j * N_HEADS:(j + 1) * N_HEADS] + carry
        carry = cp[:, PAGE_SIZE - 1:PAGE_SIZE]
        c_pages.append(cp)
    c_scr[...] = carry

    q = (q_ref[...] * scale).astype(BF16)
    s = jnp.concatenate(
        [jnp.concatenate(
            [lax.dot_general(q[:, h * HEAD_DIM:(h + 1) * HEAD_DIM], k_refs[j][_head_rows(h, PAGE_SIZE), :].astype(BF16), NT_DIMS,
                             preferred_element_type=F32) - c_pages[j][h:h + 1, :] for j in range(pg)], axis=1)
         for h in range(N_HEADS)], axis=0)
    m_old = m_scr[...]
    m_new = jnp.maximum(m_old, jnp.max(s, axis=-1, keepdims=True))
    alpha = jnp.exp(m_old - m_new)
    p = jnp.exp(s - m_new)
    l_scr[...] = alpha * l_scr[...] + jnp.sum(p, axis=-1, keepdims=True)
    pvs = []
    for h in range(N_HEADS):
        ph = p[h * tn:(h + 1) * tn].astype(BF16)
        pv = jnp.dot(ph[:, :PAGE_SIZE], v_refs[0][_head_rows(h, PAGE_SIZE), :].astype(BF16),
                     preferred_element_type=F32)
        for j in range(1, pg):
            pv = pv + jnp.dot(ph[:, j * PAGE_SIZE:(j + 1) * PAGE_SIZE],
                              v_refs[j][_head_rows(h, PAGE_SIZE), :].astype(BF16),
                              preferred_element_type=F32)
        pvs.append(pv)
    acc_scr[...] = alpha * acc_scr[...] + jnp.concatenate(pvs, axis=0)
    m_scr[...] = m_new

    @pl.when(g == pl.num_programs(1) - 1)
    def _():
        lfn = lfn_ref[...]
        run = c_scr[...]
        c_new = []
        for j in range(tn):
            run = run + lfn[:, j:j + 1]
            c_new.append(run)
        qf = q_ref[...] * scale
        row = lax.broadcasted_iota(jnp.int32, (tn, 1), 0)
        for h in range(N_HEADS):
            hs = slice(h * HEAD_DIM, (h + 1) * HEAD_DIM)
            rows = slice(h * tn, (h + 1) * tn)
            qh = qf[:, hs]
            kh = kn_ref[_head_rows(h, tn), :]
            vh = vn_ref[_head_rows(h, tn), :]
            cols = []
            for j in range(tn):
                sj = jnp.sum(qh * kh[j:j + 1, :], axis=-1, keepdims=True) - c_new[j][h:h + 1, :]
                cols.append(jnp.where(row >= j, sj, NEG_BIG))
            m_old = m_scr[rows, :]
            m_fin = m_old
            for sj in cols:
                m_fin = jnp.maximum(m_fin, sj)
            alpha = jnp.exp(m_old - m_fin)
            l = alpha * l_scr[rows, :]
            acc = alpha * acc_scr[rows, :]
            for j, sj in enumerate(cols):
                pj = jnp.exp(sj - m_fin)
                l = l + pj
                acc = acc + pj * vh[j:j + 1, :]
            o_ref[:, hs] = acc / l


def _fox_sample(q, k, v, lfn, cache_k, cache_v, cache_lft, page_table, layer, n_phys, t_rows):
    nb, n_pages = page_table.shape
    tn = (q.shape[0] - t_rows) // nb
    pg = min(PAGES_PER_STEP, n_pages)
    row0 = t_rows // tn
    base = layer * n_phys

    def kv_page(j):
        return pl.BlockSpec((None, PAGE_SIZE * N_HEADS, HEAD_DIM),
                            lambda b, g, pt: (base + pt[b, g * pg + j], 0, 0))

    def lf_page(j):
        return pl.BlockSpec((None, N_HEADS, PAGE_SIZE), lambda b, g, pt: (base + pt[b, g * pg + j], 0, 0))

    new_kv = pl.BlockSpec((tn * N_HEADS, HEAD_DIM), lambda b, g, pt: (b, 0))
    in_specs = [pl.BlockSpec((tn, D_ATTN), lambda b, g, pt: (row0 + b, 0)), new_kv, new_kv,
                pl.BlockSpec((None, N_HEADS, tn), lambda b, g, pt: (b, 0, 0))]
    in_specs += [kv_page(j) for j in range(pg)]
    in_specs += [kv_page(j) for j in range(pg)]
    in_specs += [lf_page(j) for j in range(pg)]
    grid_spec = pltpu.PrefetchScalarGridSpec(
        num_scalar_prefetch=1,
        grid=(nb, n_pages // pg),
        in_specs=in_specs,
        out_specs=pl.BlockSpec((tn, D_ATTN), lambda b, g, pt: (b, 0)),
        scratch_shapes=[pltpu.VMEM((N_HEADS * tn, 1), F32), pltpu.VMEM((N_HEADS * tn, 1), F32),
                        pltpu.VMEM((N_HEADS * tn, HEAD_DIM), F32), pltpu.VMEM((N_HEADS, 1), F32)])
    return pl.pallas_call(
        functools.partial(_fox_sample_kernel, pg=pg),
        grid_spec=grid_spec,
        out_shape=jax.ShapeDtypeStruct((nb * tn, D_ATTN), F32),
        compiler_params=_params(("parallel", "arbitrary")),
        name="fox_sample",
    )(page_table, q, k, v, lfn, *([cache_k] * pg), *([cache_v] * pg), *([cache_lft] * pg))


def _block_diag_wide(tall, rows_per_group, cols_per_group, n_cols):
    wide = jnp.concatenate([tall] * (n_cols // tall.shape[1]), axis=1)
    r = lax.broadcasted_iota(jnp.int32, wide.shape, 0)
    c = lax.broadcasted_iota(jnp.int32, wide.shape, 1)
    same = (r >> (rows_per_group.bit_length() - 1)) == (c >> (cols_per_group.bit_length() - 1))
    return jnp.where(same, wide, 0.0)


def _s5_kernel(u_ref, h0re_ref, h0im_ref, are_ref, aim_ref, ldt_ref, btre_ref, btim_ref,
               ctre_ref, ctim_ref, d_ref, wglu_ref, *rest, nb, nt, chained):
    o_ref, hre_out, him_out, bbre, bbim, cdre, cdim, bure, buim, hre, him, abre, abim, pwre, pwim = rest
    i = pl.program_id(0)
    half_in = D_SSM // 2
    half_st = D_STATE // 2

    @pl.when(i == 0)
    def _():
        bdre = _block_diag_wide(btre_ref[...], SSM_GROUP, SSM_STATE, D_STATE)
        bdim = _block_diag_wide(btim_ref[...], SSM_GROUP, SSM_STATE, D_STATE)
        cdre[...] = _block_diag_wide(ctre_ref[...], SSM_STATE, SSM_GROUP, D_SSM).astype(BF16)
        cdim[...] = _block_diag_wide(ctim_ref[...], SSM_STATE, SSM_GROUP, D_SSM).astype(BF16)
        a_re = are_ref[...]
        a_im = aim_ref[...]
        dt = jnp.exp(ldt_ref[...])
        mag = jnp.exp(dt * a_re)
        ab_re = mag * jnp.cos(dt * a_im)
        ab_im = mag * jnp.sin(dt * a_im)
        den = a_re * a_re + a_im * a_im
        nr = ab_re - 1.0
        coef_re = (nr * a_re + ab_im * a_im) / den
        coef_im = (ab_im * a_re - nr * a_im) / den
        bbre[...] = (coef_re * bdre - coef_im * bdim).astype(BF16)
        bbim[...] = (coef_re * bdim + coef_im * bdre).astype(BF16)
        abre[...] = ab_re
        abim[...] = ab_im
        hre[...] = h0re_ref[...]
        him[...] = h0im_ref[...]
        if chained:
            def power(j, carry):
                pr, pi = carry
                pwre[pl.ds(j, 1), :] = pr
                pwim[pl.ds(j, 1), :] = pi
                return ab_re * pr - ab_im * pi, ab_re * pi + ab_im * pr

            lax.fori_loop(0, nt, power, (ab_re, ab_im))

    n = nb * nt
    tsh = nt.bit_length() - 1
    bsh = nb.bit_length() - 1
    r = lax.broadcasted_iota(jnp.int32, (n, n), 0)
    c = lax.broadcasted_iota(jnp.int32, (n, n), 1)
    to_time_major = jnp.where(((r & (nb - 1)) == (c >> tsh)) & ((r >> bsh) == (c & (nt - 1))), 1.0, 0.0).astype(F32)
    to_seq_major = jnp.where(((c & (nb - 1)) == (r >> tsh)) & ((c >> bsh) == (r & (nt - 1))), 1.0, 0.0).astype(F32)
    u = _dot01(to_time_major, u_ref[...], 3)
    ub = u.astype(BF16)
    for k in range(2):
        rs = slice(k * half_in, (k + 1) * half_in)
        cs = slice(k * half_st, (k + 1) * half_st)
        bure[:, cs] = jnp.dot(ub[:, rs], bbre[rs, cs], preferred_element_type=F32)
        buim[:, cs] = jnp.dot(ub[:, rs], bbim[rs, cs], preferred_element_type=F32)

    for cidx in range(D_STATE // SCAN_COLS):
        cs = slice(cidx * SCAN_COLS, (cidx + 1) * SCAN_COLS)
        ar = jnp.broadcast_to(abre[:, cs], (nb, SCAN_COLS))
        ai = jnp.broadcast_to(abim[:, cs], (nb, SCAN_COLS))

        def step(t, carry, cs=cs, ar=ar, ai=ai):
            hr, hi = carry
            rows = pl.ds(pl.multiple_of(t * nb, nb), nb)
            nr = ar * hr - ai * hi + bure[rows, cs]
            ni = ar * hi + ai * hr + buim[rows, cs]
            bure[rows, cs] = nr
            buim[rows, cs] = ni
            return nr, ni

        if not chained:
            hr, hi = lax.fori_loop(0, nt, step, (hre[:, cs], him[:, cs]), unroll=8)
            hre[:, cs] = hr
            him[:, cs] = hi
            continue

        zeros = jnp.zeros((nb, SCAN_COLS), F32)
        fr, fi = lax.fori_loop(0, nt, step, (zeros, zeros), unroll=8)
        ptr = pwre[nt - 1:nt, cs]
        pti = pwim[nt - 1:nt, cs]
        cr = hre[:, cs]
        ci = him[:, cs]
        seg = lax.broadcasted_iota(jnp.int32, (nb, SCAN_COLS), 0)
        inr, ini = zeros, zeros
        for s in range(nb):
            inr = jnp.where(seg == s, cr, inr)
            ini = jnp.where(seg == s, ci, ini)
            cr, ci = fr[s:s + 1, :] + ptr * cr - pti * ci, fi[s:s + 1, :] + ptr * ci + pti * cr
        hre[:, cs] = cr
        him[:, cs] = ci

        def fix(t, carry, cs=cs, inr=inr, ini=ini):
            rows = pl.ds(pl.multiple_of(t * nb, nb), nb)
            pr = pwre[pl.ds(t, 1), cs]
            pi = pwim[pl.ds(t, 1), cs]
            bure[rows, cs] += pr * inr - pi * ini
            buim[rows, cs] += pr * ini + pi * inr
            return carry

        lax.fori_loop(0, nt, fix, 0, unroll=8)

    halves = []
    for k in range(2):
        rs = slice(k * half_st, (k + 1) * half_st)
        cs = slice(k * half_in, (k + 1) * half_in)
        halves.append(jnp.dot(bure[:, rs].astype(BF16), cdre[rs, cs], preferred_element_type=F32)
                      - jnp.dot(buim[:, rs].astype(BF16), cdim[rs, cs], preferred_element_type=F32))
    y = jnp.concatenate(halves, axis=1) + d_ref[...] * u
    gl = 0.5 * y * (1.0 + jnp.tanh(math.sqrt(2.0 / math.pi) * (y + 0.044715 * (y * y * y))))
    gate = jax.nn.sigmoid(jnp.dot(gl.astype(BF16), wglu_ref[...], preferred_element_type=F32))
    o_ref[...] = _dot01(to_seq_major, gl * gate, 3)
    hre_out[...] = hre[...]
    him_out[...] = him[...]


def _s5(u_all, row_block0, nb, nt, n_steps, h0re, h0im, consts, chained):
    rows = nb * nt
    ns = 1 if chained else nb
    are, aim, ldt, btre, btim, ctre, ctim, dvec, wglu = consts
    in_specs = [pl.BlockSpec((rows, D_SSM), lambda i: (row_block0 + i, 0)),
                _const_spec((ns, D_STATE)), _const_spec((ns, D_STATE)),
                _const_spec((1, D_STATE)), _const_spec((1, D_STATE)), _const_spec((1, D_STATE)),
                _const_spec(btre.shape), _const_spec(btim.shape),
                _const_spec(ctre.shape), _const_spec(ctim.shape),
                _const_spec((1, D_SSM)), _const_spec((D_SSM, D_SSM))]
    args = [u_all, h0re, h0im, are, aim, ldt, btre, btim, ctre, ctim, dvec, wglu]
    state = jax.ShapeDtypeStruct((ns, D_STATE), F32)
    return pl.pallas_call(
        functools.partial(_s5_kernel, nb=nb, nt=nt, chained=chained),
        grid=(n_steps,),
        in_specs=in_specs,
        out_specs=(pl.BlockSpec((rows, D_SSM), lambda i: (i, 0)),
                   pl.BlockSpec((ns, D_STATE), lambda i: (0, 0)),
                   pl.BlockSpec((ns, D_STATE), lambda i: (0, 0))),
        out_shape=(jax.ShapeDtypeStruct((n_steps * rows, D_SSM), F32), state, state),
        scratch_shapes=[pltpu.VMEM((D_SSM, D_STATE), BF16), pltpu.VMEM((D_SSM, D_STATE), BF16),
                        pltpu.VMEM((D_STATE, D_SSM), BF16), pltpu.VMEM((D_STATE, D_SSM), BF16),
                        pltpu.VMEM((rows, D_STATE), F32), pltpu.VMEM((rows, D_STATE), F32),
                        pltpu.VMEM((ns, D_STATE), F32), pltpu.VMEM((ns, D_STATE), F32),
                        pltpu.VMEM((1, D_STATE), F32), pltpu.VMEM((1, D_STATE), F32),
                        pltpu.VMEM((nt, D_STATE), F32), pltpu.VMEM((nt, D_STATE), F32)],
        compiler_params=_params(("arbitrary",)),
        name="s5_prompt" if chained else "s5_sample",
    )(*args)


def _pool_kernel(cur_ref, *rest, seq_len, prev_rows, pos0, prompt):
    if prompt:
        prev_ref, buf_ref, pw_ref, ps_ref, o_ref = rest
    else:
        buf_ref, pw_ref, ps_ref, o_ref = rest
    i = pl.program_id(0)
    cur = cur_ref[...]
    n = cur.shape[0]
    if prompt:
        prev = jnp.where(i == 0, buf_ref[...], prev_ref[...])
    else:
        prev = buf_ref[...]
    npv = prev.shape[0]
    lsh = seq_len.bit_length() - 1
    psh = prev_rows.bit_length() - 1
    r = lax.broadcasted_iota(jnp.int32, (n, n), 0)
    c = lax.broadcasted_iota(jnp.int32, (n, n), 1)
    same = (r >> lsh) == (c >> lsh)
    d = r - c
    r1 = lax.broadcasted_iota(jnp.int32, (n, npv), 0)
    c1 = lax.broadcasted_iota(jnp.int32, (n, npv), 1)
    same1 = (r1 >> lsh) == (c1 >> psh)
    d1 = (r1 & (seq_len - 1)) + prev_rows - (c1 & (prev_rows - 1))
    local = lax.broadcasted_iota(jnp.int32, (n, 1), 0) & (seq_len - 1)
    pos = pos0 + i * n + local
    outs = []
    for gi, w in enumerate(POOL_WINDOWS):
        gs = slice(gi * POOL_GROUP, (gi + 1) * POOL_GROUP)
        band = jnp.where(same & (d >= 0) & (d < w), 1.0, 0.0).astype(F32)
        band_prev = jnp.where(same1 & (d1 < w), 1.0, 0.0).astype(F32)
        xg = cur[:, gs]
        wsum = _dot01(band, xg, 3) + _dot01(band_prev, prev[:, gs], 3)
        cnt = jnp.minimum(w, pos + 1).astype(F32)
        pooled = wsum / cnt - xg
        outs.append(jnp.dot(pooled.astype(BF16), pw_ref[gi], preferred_element_type=F32))
    o_ref[...] = jnp.concatenate(outs, axis=1) * ps_ref[...]


def _pool_prompt(p_all, buf_pad, pw, ps, t):
    tt = ROW_TILE
    ratio = tt // POOL_GROUP
    return pl.pallas_call(
        functools.partial(_pool_kernel, seq_len=tt, prev_rows=POOL_GROUP, pos0=0, prompt=True),
        grid=(t // tt,),
        in_specs=[pl.BlockSpec((tt, D_POOL), lambda i: (i, 0)),
                  pl.BlockSpec((POOL_GROUP, D_POOL), lambda i: (jnp.maximum(i * ratio - 1, 0), 0)),
                  _const_spec((POOL_GROUP, D_POOL)), _const_spec(pw.shape), _const_spec((1, D_POOL))],
        out_specs=pl.BlockSpec((tt, D_POOL), lambda i: (i, 0)),
        out_shape=jax.ShapeDtypeStruct((t, D_POOL), F32),
        compiler_params=_params(("parallel",)),
        name="pool_prompt",
    )(p_all, p_all, buf_pad, pw, ps)


def _pool_sample(p_all, buf_pad, pw, ps, t, nb, tn, past):
    rows = nb * tn
    blk = t // rows
    return pl.pallas_call(
        functools.partial(_pool_kernel, seq_len=tn, prev_rows=16, pos0=past, prompt=False),
        grid=(1,),
        in_specs=[pl.BlockSpec((rows, D_POOL), lambda i: (blk, 0)),
                  _const_spec(buf_pad.shape), _const_spec(pw.shape), _const_spec((1, D_POOL))],
        out_specs=pl.BlockSpec((rows, D_POOL), lambda i: (0, 0)),
        out_shape=jax.ShapeDtypeStruct((rows, D_POOL), F32),
        compiler_params=_params(("arbitrary",)),
        name="pool_sample",
    )(p_all, buf_pad, pw, ps)


def _outproj_kernel(x_ref, ap_ref, as_ref, sp_ref, ss_ref, pp_ref, ps_ref, ong_ref, wout_ref, n2_ref,
                    wrh_ref, wrl_ref, rb_ref, xm_ref, xn_ref, ids_ref, gates_ref, *, prompt_tiles):
    g = ong_ref[...]
    sample = pl.program_id(0) >= prompt_tiles
    ma = _rms(jnp.where(sample, as_ref[...], ap_ref[...]), g[:, :D_ATTN]).astype(BF16)
    ms = _rms(jnp.where(sample, ss_ref[...], sp_ref[...]), g[:, D_ATTN:D_ATTN + D_SSM]).astype(BF16)
    mp = _rms(jnp.where(sample, ps_ref[...], pp_ref[...]), g[:, D_ATTN + D_SSM:]).astype(BF16)
    xm = (x_ref[...]
          + jnp.dot(ma, wout_ref[:D_ATTN, :], preferred_element_type=F32)
          + jnp.dot(ms, wout_ref[D_ATTN:D_ATTN + D_SSM, :], preferred_element_type=F32)
          + jnp.dot(mp, wout_ref[D_ATTN + D_SSM:, :], preferred_element_type=F32))
    xm_ref[...] = xm
    xn = _rms(xm, n2_ref[...])
    tm = xn.shape[0]
    for j in range(xn.shape[1] // ROW_CHUNK):
        xn_ref[_chunk_rows(j, tm), :] = xn[:, j * ROW_CHUNK:(j + 1) * ROW_CHUNK]
    xh = xn.astype(BF16)
    xl = (xn - xh.astype(F32)).astype(BF16)
    logits = (jnp.dot(xh, wrh_ref[...], preferred_element_type=F32)
              + jnp.dot(xl, wrh_ref[...], preferred_element_type=F32)
              + jnp.dot(xh, wrl_ref[...], preferred_element_type=F32))
    lt = jnp.transpose(logits)[:rb_ref.shape[0]] + rb_ref[...]
    ng, ne = N_EXPERT_GROUPS, EXPERTS_PER_GROUP
    gl = [lt[j:j + 1, :] for j in range(ng)]
    gmax = functools.reduce(jnp.maximum, gl)
    gsel = jnp.full(gmax.shape, ng - 1, jnp.int32)
    for j in range(ng - 2, -1, -1):
        gsel = jnp.where(gl[j] == gmax, j, gsel)
    gw = 1.0 / functools.reduce(jnp.add, [jnp.exp(v - gmax) for v in gl])
    es = []
    for j in range(ne):
        v = lt[ng + (ng - 1) * ne + j:ng + (ng - 1) * ne + j + 1, :]
        for gi in range(ng - 2, -1, -1):
            v = jnp.where(gsel == gi, lt[ng + gi * ne + j:ng + gi * ne + j + 1, :], v)
        es.append(v)

    def first_argmax(vals):
        vmax = functools.reduce(jnp.maximum, vals)
        idx = jnp.full(vmax.shape, len(vals) - 1, jnp.int32)
        for j in range(len(vals) - 2, -1, -1):
            idx = jnp.where(vals[j] == vmax, j, idx)
        return vmax, idx

    v1, i1 = first_argmax(es)
    v2, i2 = first_argmax([jnp.where(i1 == j, -jnp.inf, es[j]) for j in range(ne)])
    t = jnp.exp(v2 - v1)
    p1 = 1.0 / (1.0 + t)
    p2 = t / (1.0 + t)
    row = lax.broadcasted_iota(jnp.int32, ids_ref.shape, 0)
    ids_ref[...] = jnp.where(row == 0, gsel * ne + i1, jnp.where(row == 1, gsel * ne + i2, 0))
    gates_ref[...] = jnp.where(row == 0, p1 * gw, jnp.where(row == 1, p2 * gw, 0.0))


def _outproj(x, mix_prompt, mix_sample, layer, ong, wout, n2, wr_hi, wr_lo, rb):
    m, d = x.shape
    tm = ROW_TILE
    pt = mix_prompt[0].shape[0] // tm
    row = lambda w: pl.BlockSpec((tm, w), lambda i: (i, 0))
    prow = lambda w: pl.BlockSpec((tm, w), lambda i: (jnp.minimum(i, pt - 1), 0))
    srow = lambda w: pl.BlockSpec((tm, w), lambda i: (jnp.maximum(i - pt, 0), 0))
    col = pl.BlockSpec((8, tm), lambda i: (0, i))
    return pl.pallas_call(
        functools.partial(_outproj_kernel, prompt_tiles=pt),
        grid=(m // tm,),
        in_specs=[row(d), prow(D_ATTN), srow(D_ATTN), prow(D_SSM), srow(D_SSM), prow(D_POOL), srow(D_POOL),
                  _const_spec((1, d)),
                  _layer_spec(wout, layer), _const_spec((1, d)), _const_spec(wr_hi.shape),
                  _const_spec(wr_lo.shape), _const_spec(rb.shape)],
        out_specs=(row(d), pl.BlockSpec((tm * ROW_CHUNKS, ROW_CHUNK), lambda i: (i, 0)), col, col),
        out_shape=(jax.ShapeDtypeStruct((m, d), F32), jax.ShapeDtypeStruct((m * ROW_CHUNKS, ROW_CHUNK), F32),
                   jax.ShapeDtypeStruct((8, m), jnp.int32), jax.ShapeDtypeStruct((8, m), F32)),
        compiler_params=_params(("parallel",)),
        name="outproj_router",
    )(x, mix_prompt[0], mix_sample[0], mix_prompt[1], mix_sample[1], mix_prompt[2], mix_sample[2],
      ong, wout, n2, wr_hi, wr_lo, rb)


def _token_copy(src_hbm, tok, dst, r, sem):
    return pltpu.make_async_copy(src_hbm.at[pl.ds(pl.multiple_of(tok * ROW_CHUNKS, ROW_CHUNKS), ROW_CHUNKS), :],
                                 dst.at[pl.ds(pl.multiple_of(r * ROW_CHUNKS, ROW_CHUNKS), ROW_CHUNKS), :], sem)


def _gather_tokens(src_hbm, idx_ref, first_idx, dst, sem, n):
    def issue(r, carry):
        _token_copy(src_hbm, idx_ref[first_idx + r], dst, r, sem).start()
        return carry

    lax.fori_loop(0, n, issue, 0, unroll=8)


def _wait_tokens(src_hbm, dst, sem):
    pltpu.make_async_copy(src_hbm.at[pl.ds(0, dst.shape[0]), :], dst, sem).wait()


def _expert_kernel(vt_ref, ve_ref, vlo_ref, vhi_ref, tok_ref, x_hbm, gate_ref, wg_ref, wu_ref, wd_ref,
                   o_ref, xs, wgb, wub, wdb, sem, *, n_tiles):
    v = pl.program_id(0)
    tm = gate_ref.shape[0]
    lo = vlo_ref[v]
    hi = vhi_ref[v]
    tile = vt_ref[v]
    base = tile * tm
    slot = tile & 1
    nonempty = hi > lo
    first = jnp.logical_and(nonempty, lo == base)
    changed = jnp.logical_or(v == 0, ve_ref[v] != ve_ref[jnp.maximum(v - 1, 0)])

    @pl.when(changed)
    def _():
        wgb[...] = wg_ref[...].astype(BF16)
        wub[...] = wu_ref[...].astype(BF16)
        wdb[...] = wd_ref[...].astype(BF16)

    @pl.when(v == 0)
    def _():
        _gather_tokens(x_hbm, tok_ref, 0, xs.at[0], sem.at[0], tm)

    @pl.when(first)
    def _():
        _wait_tokens(x_hbm, xs.at[slot], sem.at[slot])

    @pl.when(jnp.logical_and(first, tile + 1 < n_tiles))
    def _():
        _gather_tokens(x_hbm, tok_ref, base + tm, xs.at[1 - slot], sem.at[1 - slot], tm)

    @pl.when(nonempty)
    def _():
        xt = xs.at[slot]
        xb = jnp.concatenate([xt[_chunk_rows(j, tm), :] for j in range(ROW_CHUNKS)], axis=1).astype(BF16)
        hg = jnp.dot(xb, wgb[...], preferred_element_type=F32)
        hu = jnp.dot(xb, wub[...], preferred_element_type=F32)
        row = base + lax.broadcasted_iota(jnp.int32, (tm, 1), 0)
        keep = jnp.logical_and(row >= lo, row < hi)
        act = jnp.where(keep, (hg * jax.nn.sigmoid(hg)) * hu * gate_ref[...], 0.0)
        y = jnp.dot(act.astype(BF16), wdb[...], preferred_element_type=F32)

        @pl.when(first)
        def _():
            for j in range(ROW_CHUNKS):
                o_ref[_chunk_rows(j, tm), :] = y[:, j * ROW_CHUNK:(j + 1) * ROW_CHUNK]

        @pl.when(jnp.logical_not(first))
        def _():
            for j in range(ROW_CHUNKS):
                o_ref[_chunk_rows(j, tm), :] += y[:, j * ROW_CHUNK:(j + 1) * ROW_CHUNK]


def _expert_ffn(plan, xn_rows, layer, wg, wu, wd):
    v_tile, v_exp, v_lo, v_hi, tok_sorted, gate_sorted = plan
    n_flat = tok_sorted.shape[0]
    d = wg.shape[2]
    tm = ROW_TILE
    grid_spec = pltpu.PrefetchScalarGridSpec(
        num_scalar_prefetch=5,
        grid=(v_tile.shape[0],),
        in_specs=[pl.BlockSpec(memory_space=pl.ANY),
                  pl.BlockSpec((tm, 1), lambda v, vt, ve, lo, hi, tok: (vt[v], 0)),
                  pl.BlockSpec((None, None, d, D_EXPERT), lambda v, vt, ve, lo, hi, tok: (layer, ve[v], 0, 0)),
                  pl.BlockSpec((None, None, d, D_EXPERT), lambda v, vt, ve, lo, hi, tok: (layer, ve[v], 0, 0)),
                  pl.BlockSpec((None, None, D_EXPERT, d), lambda v, vt, ve, lo, hi, tok: (layer, ve[v], 0, 0))],
        out_specs=pl.BlockSpec((tm * ROW_CHUNKS, ROW_CHUNK), lambda v, vt, ve, lo, hi, tok: (vt[v], 0)),
        scratch_shapes=[pltpu.VMEM((2, tm * ROW_CHUNKS, ROW_CHUNK), F32),
                        pltpu.VMEM((d, D_EXPERT), BF16), pltpu.VMEM((d, D_EXPERT), BF16),
                        pltpu.VMEM((D_EXPERT, d), BF16), pltpu.SemaphoreType.DMA((2,))])
    return pl.pallas_call(
        functools.partial(_expert_kernel, n_tiles=n_flat // tm),
        grid_spec=grid_spec,
        out_shape=jax.ShapeDtypeStruct((n_flat * ROW_CHUNKS, ROW_CHUNK), F32),
        compiler_params=_params(("arbitrary",)),
        name="expert_ffn",
    )(v_tile, v_exp, v_lo, v_hi, tok_sorted, xn_rows, gate_sorted, wg, wu, wd)


def _combine_kernel(d1_ref, d2_ref, xm_ref, ys_hbm, *rest, n_tiles, prompt_tiles):
    out_refs, (buf1, buf2, sem) = rest[:-3], rest[-3:]
    i = pl.program_id(0)
    tm = xm_ref.shape[0]
    slot = i & 1

    def start(tile, s):
        _gather_tokens(ys_hbm, d1_ref, tile * tm, buf1.at[s], sem.at[0, s], tm)
        _gather_tokens(ys_hbm, d2_ref, tile * tm, buf2.at[s], sem.at[1, s], tm)

    pl.when(i == 0)(lambda: start(0, 0))
    _wait_tokens(ys_hbm, buf1.at[slot], sem.at[0, slot])
    _wait_tokens(ys_hbm, buf2.at[slot], sem.at[1, slot])
    pl.when(i + 1 < n_tiles)(lambda: start(i + 1, 1 - slot))
    rows1 = buf1.at[slot]
    rows2 = buf2.at[slot]

    def write(o_ref):
        for j in range(ROW_CHUNKS):
            cs = slice(j * ROW_CHUNK, (j + 1) * ROW_CHUNK)
            o_ref[:, cs] = xm_ref[:, cs] + rows1[_chunk_rows(j, tm), :] + rows2[_chunk_rows(j, tm), :]

    if prompt_tiles is None:
        write(out_refs[0])
    else:
        pl.when(i < prompt_tiles)(lambda: write(out_refs[0]))
        pl.when(i >= prompt_tiles)(lambda: write(out_refs[1]))


def _combine(dest1, dest2, xm, ys_rows, split_rows=None):
    m, d = xm.shape
    tm = ROW_TILE
    row = pl.BlockSpec((tm, d), lambda i, a, b: (i, 0))
    if split_rows is None:
        pt = None
        out_specs = row
        out_shape = jax.ShapeDtypeStruct((m, d), F32)
    else:
        pt = split_rows // tm
        out_specs = (pl.BlockSpec((tm, d), lambda i, a, b: (jnp.minimum(i, pt - 1), 0)),
                     pl.BlockSpec((tm, d), lambda i, a, b: (jnp.maximum(i - pt, 0), 0)))
        out_shape = (jax.ShapeDtypeStruct((split_rows, d), F32), jax.ShapeDtypeStruct((m - split_rows, d), F32))
    buf = pltpu.VMEM((2, tm * ROW_CHUNKS, ROW_CHUNK), F32)
    grid_spec = pltpu.PrefetchScalarGridSpec(
        num_scalar_prefetch=2,
        grid=(m // tm,),
        in_specs=[row, pl.BlockSpec(memory_space=pl.ANY)],
        out_specs=out_specs,
        scratch_shapes=[buf, buf, pltpu.SemaphoreType.DMA((2, 2))])
    return pl.pallas_call(
        functools.partial(_combine_kernel, n_tiles=m // tm, prompt_tiles=pt),
        grid_spec=grid_spec,
        out_shape=out_shape,
        compiler_params=_params(("arbitrary",)),
        name="moe_combine",
    )(dest1, dest2, xm, ys_rows)


def _dispatch_plan(ids, gates, m):
    tm = ROW_TILE
    n_flat = 2 * m
    n_tiles = n_flat // tm
    e_flat = ids[:2].reshape(n_flat)
    g_flat = gates[:2].reshape(n_flat)
    tok = jnp.arange(n_flat, dtype=jnp.int32) % m
    _, tok_sorted, gate_sorted = lax.sort((e_flat, tok, g_flat), num_keys=1, is_stable=True)
    onehot = (e_flat[:, None] == jnp.arange(N_EXPERTS, dtype=jnp.int32)[None, :]).astype(jnp.int32)
    csum = jnp.cumsum(onehot, axis=0)
    counts = csum[-1]
    row_end = jnp.cumsum(counts)
    row_start = row_end - counts
    dest = jnp.sum(onehot * (row_start[None, :] + csum - 1), axis=1).astype(jnp.int32)
    cuts = jnp.sort(jnp.concatenate([jnp.arange(n_tiles, dtype=jnp.int32) * tm, row_start[1:].astype(jnp.int32)]))
    v_lo = cuts
    v_hi = jnp.concatenate([cuts[1:], jnp.full((1,), n_flat, jnp.int32)])
    v_tile = jnp.minimum(v_lo // tm, n_tiles - 1).astype(jnp.int32)
    v_exp = jnp.minimum(jnp.sum((v_lo[:, None] >= row_end[None, :]).astype(jnp.int32), axis=1), N_EXPERTS - 1)
    plan = (v_tile, v_exp.astype(jnp.int32), v_lo, v_hi, tok_sorted, gate_sorted.reshape(n_flat, 1))
    return plan, dest[:m], dest[m:]


def _tall_blocks(w):
    g, c, r = w.shape
    tall = jnp.swapaxes(w, 1, 2).reshape(g * r, c)
    return jnp.tile(tall, (1, 128 // c))


def kernel(x_prompt, x_sample, cache_k, cache_v, cache_logf, page_table, state_ssm_re, state_ssm_im, state_pool, norm1_g, w_in, f_bias, q_gain, k_gain, ssm_a_re, ssm_a_im, ssm_log_dt, ssm_b_re, ssm_b_im, ssm_c_re, ssm_c_im, ssm_d, ssm_w_glu, pool_w, pool_scale, out_norm_g, w_out, norm2_g, router_group_w, router_group_b, router_expert_w, router_expert_b, moe_w_gate, moe_w_up, moe_w_down):
    n_prompt, t, d = x_prompt.shape
    nb, tn, _ = x_sample.shape
    depth = w_in.shape[0]
    n_phys = cache_k.shape[1]
    past = page_table.shape[1] * cache_k.shape[2]
    assert n_prompt == 1
    ms = nb * tn
    m = t + ms
    x = jnp.concatenate([x_prompt.reshape(t, d), x_sample.reshape(ms, d)], axis=0)

    ck = cache_k.reshape(depth * n_phys, PAGE_SIZE * N_HEADS, HEAD_DIM)
    cv = cache_v.reshape(depth * n_phys, PAGE_SIZE * N_HEADS, HEAD_DIM)
    clf = jnp.swapaxes(cache_logf, 2, 3).reshape(depth * n_phys, N_HEADS, PAGE_SIZE)
    zero_state = jnp.zeros((1, D_STATE), F32)
    zero_buf = jnp.zeros((POOL_GROUP, D_POOL), F32)

    wqkv = w_in[:, :, :3 * D_ATTN].astype(BF16)
    wup = w_in[:, :, 3 * D_ATTN + N_HEADS:].astype(BF16)
    wft = jnp.pad(jnp.swapaxes(w_in[:, :, 3 * D_ATTN:3 * D_ATTN + N_HEADS], 1, 2), ((0, 0), (0, 8), (0, 0))).astype(BF16)
    wout = w_out.astype(BF16)

    outs_p, outs_s = [], []
    for l in range(depth):
        q, k_p, v_p, k_s, v_s, kb, vb, lft, u, p_in = _proj(
            x, t, l, norm1_g[l].reshape(1, d), wqkv, wup, wft, f_bias[l].reshape(N_HEADS, 1),
            q_gain[l].reshape(1, HEAD_DIM), k_gain[l].reshape(1, HEAD_DIM))

        c_prompt = _prompt_cumsum(lft, t)
        qk_bound = 1.02 * (HEAD_DIM ** 0.5) * jnp.max(jnp.abs(q_gain[l])) * jnp.max(jnp.abs(k_gain[l]))
        a_p = _fox_prompt(q, kb, vb, c_prompt, t, qk_bound)
        lfn = lft[:, t:].reshape(N_HEADS, nb, tn).transpose(1, 0, 2)
        a_s = _fox_sample(q, k_s, v_s, lfn, ck, cv, clf, page_table, l, n_phys, t)

        consts = (ssm_a_re[l].reshape(1, D_STATE), ssm_a_im[l].reshape(1, D_STATE),
                  jnp.repeat(ssm_log_dt[l], SSM_STATE).reshape(1, D_STATE),
                  _tall_blocks(ssm_b_re[l]), _tall_blocks(ssm_b_im[l]),
                  _tall_blocks(ssm_c_re[l]), _tall_blocks(ssm_c_im[l]),
                  ssm_d[l].reshape(1, D_SSM), ssm_w_glu[l].astype(BF16))
        s_p, hre_p, him_p = _s5(u, 0, SCAN_SEGMENTS, ROW_TILE // SCAN_SEGMENTS, t // ROW_TILE, zero_state,
                                zero_state, consts, True)
        s_s, hre_s, him_s = _s5(u, t // ms, nb, tn, 1, state_ssm_re[l].reshape(nb, D_STATE),
                                state_ssm_im[l].reshape(nb, D_STATE), consts, False)

        pw = pool_w[l].astype(BF16)
        ps = pool_scale[l].reshape(1, D_POOL)
        po_p = _pool_prompt(p_in, zero_buf, pw, ps, t)
        buf_s = jnp.pad(state_pool[l], ((0, 0), (1, 0), (0, 0))).reshape(nb * 16, D_POOL)
        po_s = _pool_sample(p_in, buf_s, pw, ps, t, nb, tn, past)

        n_route = N_EXPERT_GROUPS + N_EXPERTS
        wr = jnp.pad(jnp.concatenate([router_group_w[l], router_expert_w[l]], axis=1), ((0, 0), (0, 128 - n_route)))
        rb = jnp.pad(jnp.concatenate([router_group_b[l], router_expert_b[l]]), (0, 24 - n_route)).reshape(24, 1)
        wr_hi = wr.astype(BF16)
        wr_lo = (wr - wr_hi.astype(F32)).astype(BF16)
        xm, xn, ids, gates = _outproj(x, (a_p, s_p, po_p), (a_s, s_s, po_s), l, out_norm_g[l].reshape(1, d),
                                      wout, norm2_g[l].reshape(1, d), wr_hi, wr_lo, rb)

        plan, dest1, dest2 = _dispatch_plan(ids, gates, m)
        ys = _expert_ffn(plan, xn, l, moe_w_gate, moe_w_up, moe_w_down)
        if l + 1 < depth:
            x = _combine(dest1, dest2, xm, ys)
        else:
            y_prompt, y_sample = _combine(dest1, dest2, xm, ys, split_rows=t)

        lf_rows = lft.T
        outs_p.append((k_p.reshape(1, t, N_HEADS, HEAD_DIM), v_p.reshape(1, t, N_HEADS, HEAD_DIM),
                       lf_rows[:t].reshape(1, t, N_HEADS),
                       hre_p.reshape(1, SSM_GROUPS, SSM_STATE), him_p.reshape(1, SSM_GROUPS, SSM_STATE),
                       p_in[t - POOL_BUF:t].reshape(1, POOL_BUF, D_POOL)))
        p_s = p_in[t:].reshape(nb, tn, D_POOL)
        msh = ms * N_HEADS
        outs_s.append((k_s[:msh].reshape(nb, tn, N_HEADS, HEAD_DIM), v_s[:msh].reshape(nb, tn, N_HEADS, HEAD_DIM),
                       lf_rows[t:].reshape(nb, tn, N_HEADS),
                       hre_s.reshape(nb, SSM_GROUPS, SSM_STATE), him_s.reshape(nb, SSM_GROUPS, SSM_STATE),
                       jnp.concatenate([state_pool[l], p_s], axis=1)[:, -POOL_BUF:]))

    stack = lambda outs, i: jnp.stack([o[i] for o in outs], axis=0)
    return (y_prompt.reshape(1, t, d), y_sample.reshape(nb, tn, d),
            *[stack(outs_p, i) for i in range(6)], *[stack(outs_s, i) for i in range(6)])
```

```python
import functools
import math

import jax
import jax.numpy as jnp
from jax import lax
from jax.experimental import pallas as pl
from jax.experimental.pallas import tpu as pltpu

F32 = jnp.float32
BF16 = jnp.bfloat16
HIGHEST = lax.Precision.HIGHEST
NT_DIMS = (((1,), (1,)), ((), ()))

EPS = 1e-6
HEAD_DIM = 128
N_HEADS = 8
D_ATTN = N_HEADS * HEAD_DIM
D_SSM = 512
SSM_GROUPS = 32
SSM_GROUP = 16
SSM_STATE = 64
D_STATE = SSM_GROUPS * SSM_STATE
D_POOL = 512
POOL_WINDOWS = (2, 4, 8, 16)
POOL_GROUP = 128
POOL_BUF = 15
N_EXPERT_GROUPS = 4
EXPERTS_PER_GROUP = 4
N_EXPERTS = 16
D_EXPERT = 512
PAGE_SIZE = 128
NEG_BIG = -1e30
SKIP_MARGIN = 30.0

ROW_TILE = 256
ROW_CHUNK = 128
ROW_CHUNKS = 2048 // ROW_CHUNK
ATTN_TILE = 512
PAGES_PER_STEP = 16
SCAN_COLS = 512
SCAN_SEGMENTS = 8
VMEM_LIMIT = 56 * 1024 * 1024


def _params(sem, vmem=VMEM_LIMIT):
    return pltpu.CompilerParams(dimension_semantics=sem, vmem_limit_bytes=vmem)


def _const_spec(shape):
    nd = len(shape)
    return pl.BlockSpec(shape, lambda *_: (0,) * nd, pipeline_mode=pl.Buffered(1))


def _layer_spec(stacked, layer):
    nd = stacked.ndim - 1
    return pl.BlockSpec((None,) + stacked.shape[1:], lambda *_: (layer,) + (0,) * nd, pipeline_mode=pl.Buffered(1))


def _rms(y, g):
    return y * lax.rsqrt(jnp.mean(y * y, axis=-1, keepdims=True) + EPS) * g


def _dot01(mat01, x, pieces):
    m = mat01.astype(BF16)
    acc = None
    rem = x
    for _ in range(pieces):
        part = rem.astype(BF16)
        rem = rem - part.astype(F32)
        d = jnp.dot(m, part, preferred_element_type=F32)
        acc = d if acc is None else acc + d
    return acc


def _chunk_rows(j, n):
    return pl.ds(j, n, stride=ROW_CHUNKS)


def _head_rows(h, n):
    return pl.ds(h, n, stride=N_HEADS)


def _proj_kernel(xp_ref, xs_ref, g1_ref, wqkv_ref, wup_ref, wft_ref, fb_ref, qg_ref, kg_ref,
                 q_ref, kp_ref, vp_ref, ks_ref, vs_ref, kb_ref, vb_ref, lf_ref, u_ref, p_ref, *, sample_tiles):
    x = jnp.where(pl.program_id(0) < sample_tiles, xs_ref[...], xp_ref[...])
    hb = _rms(x, g1_ref[...]).astype(BF16)
    tm = hb.shape[0]
    qg = qg_ref[...]
    kg = kg_ref[...]
    k_heads, v_heads = [], []
    for c in range(D_ATTN // 256):
        lo = c * 256
        qc = jnp.dot(hb, wqkv_ref[:, lo:lo + 256], preferred_element_type=F32)
        kc = jnp.dot(hb, wqkv_ref[:, D_ATTN + lo:D_ATTN + lo + 256], preferred_element_type=F32)
        vc = jnp.dot(hb, wqkv_ref[:, 2 * D_ATTN + lo:2 * D_ATTN + lo + 256], preferred_element_type=F32)
        for j in range(2):
            a = lo + j * HEAD_DIM
            q_ref[:, a:a + HEAD_DIM] = _rms(qc[:, j * HEAD_DIM:(j + 1) * HEAD_DIM], qg)
            kn = _rms(kc[:, j * HEAD_DIM:(j + 1) * HEAD_DIM], kg)
            kb_ref[:, a:a + HEAD_DIM] = kn.astype(BF16)
            k_heads.append(kn)
            v_heads.append(vc[:, j * HEAD_DIM:(j + 1) * HEAD_DIM])
        vb_ref[:, lo:lo + 256] = vc.astype(BF16)

    for h in range(N_HEADS):
        for k_ref, v_ref in ((kp_ref, vp_ref), (ks_ref, vs_ref)):
            k_ref[_head_rows(h, tm), :] = k_heads[h]
            v_ref[_head_rows(h, tm), :] = v_heads[h]
    u_ref[...] = jnp.dot(hb, wup_ref[:, :D_SSM], preferred_element_type=F32)
    p_ref[...] = jnp.dot(hb, wup_ref[:, D_SSM:], preferred_element_type=F32)
    ft = lax.dot_general(wft_ref[...], hb, NT_DIMS, preferred_element_type=F32)[:N_HEADS]
    z = ft + fb_ref[...]
    lf_ref[...] = jnp.minimum(z, 0.0) - jnp.log1p(jnp.exp(-jnp.abs(z)))


def _x_specs(x_pair, tm, pt, tile):
    xp, xs = x_pair
    off = 0 if xs is not None else pt
    st = (xs if xs is not None else xp).shape[0] // tm - off
    d = xp.shape[1]
    specs = [pl.BlockSpec((tm, d), lambda i, *_: (jnp.clip(tile(i), 0, pt - 1), 0)),
             pl.BlockSpec((tm, d), lambda i, *_: (off + jnp.clip(tile(i) - pt, 0, st - 1), 0))]
    return specs, (xp, xs if xs is not None else xp)


def _proj(x_pair, t, m, layer, g1, wqkv, wup, wft, fb, qg, kg):
    d = x_pair[0].shape[1]
    tm = ROW_TILE
    pt = t // tm
    st = (m - t) // tm
    tile = lambda i: jnp.where(i < st, pt + i, i - st)
    row = lambda w: pl.BlockSpec((tm, w), lambda i: (tile(i), 0))
    heads_p = pl.BlockSpec((tm * N_HEADS, HEAD_DIM), lambda i: (jnp.maximum(i - st, 0), 0))
    heads_s = pl.BlockSpec((tm * N_HEADS, HEAD_DIM), lambda i: (jnp.minimum(i, st), 0))
    kv_p = jax.ShapeDtypeStruct((t * N_HEADS, HEAD_DIM), F32)
    kv_s = jax.ShapeDtypeStruct(((m - t + tm) * N_HEADS, HEAD_DIM), F32)
    outs = (
        jax.ShapeDtypeStruct((m, D_ATTN), F32), kv_p, kv_p, kv_s, kv_s, jax.ShapeDtypeStruct((m, D_ATTN), BF16),
        jax.ShapeDtypeStruct((m, D_ATTN), BF16), jax.ShapeDtypeStruct((N_HEADS, m), F32),
        jax.ShapeDtypeStruct((m, D_SSM), F32), jax.ShapeDtypeStruct((m, D_POOL), F32))
    x_specs, x_args = _x_specs(x_pair, tm, pt, tile)
    return pl.pallas_call(
        functools.partial(_proj_kernel, sample_tiles=st),
        grid=(m // tm,),
        in_specs=[*x_specs, _const_spec((1, d)), _layer_spec(wqkv, layer), _layer_spec(wup, layer),
                  _layer_spec(wft, layer), _const_spec((N_HEADS, 1)), _const_spec((1, HEAD_DIM)),
                  _const_spec((1, HEAD_DIM))],
        out_specs=(row(D_ATTN), heads_p, heads_p, heads_s, heads_s, row(D_ATTN), row(D_ATTN),
                   pl.BlockSpec((N_HEADS, tm), lambda i: (0, tile(i))), row(D_SSM), row(D_POOL)),
        out_shape=outs,
        compiler_params=_params(("arbitrary",)),
        name="proj",
    )(*x_args, g1, wqkv, wup, wft, fb, qg, kg)


def _cumsum_kernel(lf_ref, c_ref):
    t = c_ref.shape[1]
    w = ATTN_TILE
    r = lax.broadcasted_iota(jnp.int32, (w, w), 0)
    c = lax.broadcasted_iota(jnp.int32, (w, w), 1)
    upper = jnp.where(r <= c, 1.0, 0.0).astype(F32)
    carry = jnp.zeros((N_HEADS, 1), F32)
    for s in range(t // w):
        blk = jnp.dot(lf_ref[:, s * w:(s + 1) * w], upper, precision=HIGHEST,
                      preferred_element_type=F32) + carry
        c_ref[:, s * w:(s + 1) * w] = blk
        carry = blk[:, w - 1:w]


def _prompt_cumsum(lft, t):
    return pl.pallas_call(
        _cumsum_kernel,
        grid=(1,),
        in_specs=[pl.BlockSpec((N_HEADS, t), lambda i: (0, 0))],
        out_specs=pl.BlockSpec((N_HEADS, t), lambda i: (0, 0)),
        out_shape=jax.ShapeDtypeStruct((N_HEADS, t), F32),
        compiler_params=_params(("arbitrary",)),
        name="prompt_cumsum",
    )(lft)


def _fox_prompt_kernel(kstart_ref, q_ref, k_ref, v_ref, c_ref, o_ref):
    qi = pl.program_id(1)
    first_tile = kstart_ref[pl.program_id(0), qi]
    tq = q_ref.shape[0]
    q = (q_ref[...] * (HEAD_DIM ** -0.5)).astype(BF16)

    def tile(ki, carry, masked):
        m, l, acc = carry
        off = pl.multiple_of(ki * tq, tq)
        kt = k_ref[pl.ds(off, tq), :]
        vt = v_ref[pl.ds(off, tq), :]
        s = lax.dot_general(q, kt, NT_DIMS, preferred_element_type=F32) - c_ref[ki]
        if masked:
            r = lax.broadcasted_iota(jnp.int32, s.shape, 0)
            c = lax.broadcasted_iota(jnp.int32, s.shape, 1)
            s = jnp.where(c <= r, s, NEG_BIG)
        m_new = jnp.maximum(m, jnp.max(s, axis=-1, keepdims=True))
        alpha = jnp.exp(m - m_new)
        p = jnp.exp(s - m_new)
        l = alpha * l + jnp.sum(p, axis=-1, keepdims=True)
        acc = alpha * acc + jnp.dot(p.astype(BF16), vt, preferred_element_type=F32)
        return m_new, l, acc

    init = (jnp.full((tq, 1), NEG_BIG, F32), jnp.zeros((tq, 1), F32), jnp.zeros((tq, HEAD_DIM), F32))
    carry = lax.fori_loop(first_tile, qi, lambda ki, cr: tile(ki, cr, False), init)
    _, l, acc = tile(qi, carry, True)
    o_ref[...] = acc / l


def _negligible_key_tiles(c, qk_bound, tq):
    c_first = c[:, ::tq]
    c_last = c[:, tq - 1::tq]
    nq = c_first.shape[1]
    gap = c_first[:, :, None] - c_last[:, None, :]
    earlier = jnp.arange(nq)[None, :] < jnp.arange(nq)[:, None]
    skip = (gap < -(2.0 * qk_bound + SKIP_MARGIN)) & earlier[None]
    return jnp.sum(skip.astype(jnp.int32), axis=2)


def _fox_prompt(q, kb, vb, c, t, qk_bound):
    tq = ATTN_TILE
    nq = t // tq
    c4 = c.reshape(N_HEADS, nq, 1, tq)
    kstart = _negligible_key_tiles(c, qk_bound, tq)
    grid_spec = pltpu.PrefetchScalarGridSpec(
        num_scalar_prefetch=1,
        grid=(N_HEADS, nq),
        in_specs=[pl.BlockSpec((tq, HEAD_DIM), lambda h, i, ks: (i, h)),
                  pl.BlockSpec((t, HEAD_DIM), lambda h, i, ks: (0, h)),
                  pl.BlockSpec((t, HEAD_DIM), lambda h, i, ks: (0, h)),
                  pl.BlockSpec((None, nq, 1, tq), lambda h, i, ks: (h, 0, 0, 0))],
        out_specs=pl.BlockSpec((tq, HEAD_DIM), lambda h, i, ks: (i, h)))
    return pl.pallas_call(
        _fox_prompt_kernel,
        grid_spec=grid_spec,
        out_shape=jax.ShapeDtypeStruct((t, D_ATTN), F32),
        compiler_params=_params(("parallel", "parallel")),
        name="fox_prompt",
    )(kstart, q, kb, vb, c4)


def _fox_sample_kernel(pt_ref, q_ref, kn_ref, vn_ref, lfn_ref, *rest, pg):
    del pt_ref
    k_refs = rest[:pg]
    v_refs = rest[pg:2 * pg]
    lf_refs = rest[2 * pg:3 * pg]
    o_ref = rest[3 * pg]
    m_scr, l_scr, acc_scr, c_scr = rest[3 * pg + 1:]
    g = pl.program_id(1)
    tn = q_ref.shape[0]
    scale = HEAD_DIM ** -0.5

    @pl.when(g == 0)
    def _():
        m_scr[...] = jnp.full(m_scr.shape, NEG_BIG, F32)
        l_scr[...] = jnp.zeros(l_scr.shape, F32)
        acc_scr[...] = jnp.zeros(acc_scr.shape, F32)
        c_scr[...] = jnp.zeros(c_scr.shape, F32)

    r = lax.broadcasted_iota(jnp.int32, (PAGE_SIZE, PAGE_SIZE), 0)
    c = lax.broadcasted_iota(jnp.int32, (PAGE_SIZE, PAGE_SIZE), 1)
    upper = jnp.where(r <= c, 1.0, 0.0).astype(BF16)
    lf = jnp.concatenate([lf_refs[j][...] for j in range(pg)], axis=0)
    hi = lf.astype(BF16).astype(F32)
    rem = lf - hi
    mid = rem.astype(BF16).astype(F32)
    pieces = jnp.concatenate([hi, mid, rem - mid], axis=0).astype(BF16)
    cs = jnp.dot(pieces, upper, preferred_element_type=F32)
    n = pg * N_HEADS
    c_in = cs[:n] + cs[n:2 * n] + cs[2 * n:]
    carry = c_scr[...]
    c_pages = []
    for j in range(pg):
        cp = c_in[j * N_HEADS:(j + 1) * N_HEADS] + carry
        carry = cp[:, PAGE_SIZE - 1:PAGE_SIZE]
        c_pages.append(cp)
    c_scr[...] = carry

    q = (q_ref[...] * scale).astype(BF16)
    s = jnp.concatenate(
        [jnp.concatenate(
            [lax.dot_general(q[:, h * HEAD_DIM:(h + 1) * HEAD_DIM], k_refs[j][_head_rows(h, PAGE_SIZE), :].astype(BF16), NT_DIMS,
                             preferred_element_type=F32) - c_pages[j][h:h + 1, :] for j in range(pg)], axis=1)
         for h in range(N_HEADS)], axis=0)
    m_old = m_scr[...]
    m_new = jnp.maximum(m_old, jnp.max(s, axis=-1, keepdims=True))
    alpha = jnp.exp(m_old - m_new)
    p = jnp.exp(s - m_new)
    l_scr[...] = alpha * l_scr[...] + jnp.sum(p, axis=-1, keepdims=True)
    pvs = []
    for h in range(N_HEADS):
        ph = p[h * tn:(h + 1) * tn].astype(BF16)
        pv = jnp.dot(ph[:, :PAGE_SIZE], v_refs[0][_head_rows(h, PAGE_SIZE), :].astype(BF16),
                     preferred_element_type=F32)
        for j in range(1, pg):
            pv = pv + jnp.dot(ph[:, j * PAGE_SIZE:(j + 1) * PAGE_SIZE],
                              v_refs[j][_head_rows(h, PAGE_SIZE), :].astype(BF16),
                              preferred_element_type=F32)
        pvs.append(pv)
    acc_scr[...] = alpha * acc_scr[...] + jnp.concatenate(pvs, axis=0)
    m_scr[...] = m_new

    @pl.when(g == pl.num_programs(1) - 1)
    def _():
        lfn = lfn_ref[...]
        run = c_scr[...]
        c_new = []
        for j in range(tn):
            run = run + lfn[:, j:j + 1]
            c_new.append(run)
        qf = q_ref[...] * scale
        row = lax.broadcasted_iota(jnp.int32, (tn, 1), 0)
        for h in range(N_HEADS):
            hs = slice(h * HEAD_DIM, (h + 1) * HEAD_DIM)
            rows = slice(h * tn, (h + 1) * tn)
            qh = qf[:, hs]
            kh = kn_ref[_head_rows(h, tn), :]
            vh = vn_ref[_head_rows(h, tn), :]
            cols = []
            for j in range(tn):
                sj = jnp.sum(qh * kh[j:j + 1, :], axis=-1, keepdims=True) - c_new[j][h:h + 1, :]
                cols.append(jnp.where(row >= j, sj, NEG_BIG))
            m_old = m_scr[rows, :]
            m_fin = m_old
            for sj in cols:
                m_fin = jnp.maximum(m_fin, sj)
            alpha = jnp.exp(m_old - m_fin)
            l = alpha * l_scr[rows, :]
            acc = alpha * acc_scr[rows, :]
            for j, sj in enumerate(cols):
                pj = jnp.exp(sj - m_fin)
                l = l + pj
                acc = acc + pj * vh[j:j + 1, :]
            o_ref[:, hs] = acc / l


def _fox_sample(q, k, v, lfn, cache_k, cache_v, cache_lft, page_table, layer, n_phys, t_rows):
    nb, n_pages = page_table.shape
    tn = (q.shape[0] - t_rows) // nb
    pg = min(PAGES_PER_STEP, n_pages)
    row0 = t_rows // tn
    base = layer * n_phys

    def kv_page(j):
        return pl.BlockSpec((None, PAGE_SIZE * N_HEADS, HEAD_DIM),
                            lambda b, g, pt: (base + pt[b, g * pg + j], 0, 0))

    def lf_page(j):
        return pl.BlockSpec((None, N_HEADS, PAGE_SIZE), lambda b, g, pt: (base + pt[b, g * pg + j], 0, 0))

    new_kv = pl.BlockSpec((tn * N_HEADS, HEAD_DIM), lambda b, g, pt: (b, 0))
    in_specs = [pl.BlockSpec((tn, D_ATTN), lambda b, g, pt: (row0 + b, 0)), new_kv, new_kv,
                pl.BlockSpec((None, N_HEADS, tn), lambda b, g, pt: (b, 0, 0))]
    in_specs += [kv_page(j) for j in range(pg)]
    in_specs += [kv_page(j) for j in range(pg)]
    in_specs += [lf_page(j) for j in range(pg)]
    grid_spec = pltpu.PrefetchScalarGridSpec(
        num_scalar_prefetch=1,
        grid=(nb, n_pages // pg),
        in_specs=in_specs,
        out_specs=pl.BlockSpec((tn, D_ATTN), lambda b, g, pt: (b, 0)),
        scratch_shapes=[pltpu.VMEM((N_HEADS * tn, 1), F32), pltpu.VMEM((N_HEADS * tn, 1), F32),
                        pltpu.VMEM((N_HEADS * tn, HEAD_DIM), F32), pltpu.VMEM((N_HEADS, 1), F32)])
    return pl.pallas_call(
        functools.partial(_fox_sample_kernel, pg=pg),
        grid_spec=grid_spec,
        out_shape=jax.ShapeDtypeStruct((nb * tn, D_ATTN), F32),
        compiler_params=_params(("parallel", "arbitrary")),
        name="fox_sample",
    )(page_table, q, k, v, lfn, *([cache_k] * pg), *([cache_v] * pg), *([cache_lft] * pg))


def _block_diag_wide(tall, rows_per_group, cols_per_group, n_cols):
    wide = jnp.concatenate([tall] * (n_cols // tall.shape[1]), axis=1)
    r = lax.broadcasted_iota(jnp.int32, wide.shape, 0)
    c = lax.broadcasted_iota(jnp.int32, wide.shape, 1)
    same = (r >> (rows_per_group.bit_length() - 1)) == (c >> (cols_per_group.bit_length() - 1))
    return jnp.where(same, wide, 0.0)


def _s5_kernel(u_ref, h0re_ref, h0im_ref, are_ref, aim_ref, ldt_ref, btre_ref, btim_ref,
               ctre_ref, ctim_ref, d_ref, wglu_ref, *rest, nb, nt, chained):
    o_ref, hre_out, him_out, bbre, bbim, cdre, cdim, bure, buim, hre, him, abre, abim, pwre, pwim = rest
    i = pl.program_id(0)
    half_in = D_SSM // 2
    half_st = D_STATE // 2

    @pl.when(i == 0)
    def _():
        bdre = _block_diag_wide(btre_ref[...], SSM_GROUP, SSM_STATE, D_STATE)
        bdim = _block_diag_wide(btim_ref[...], SSM_GROUP, SSM_STATE, D_STATE)
        cdre[...] = _block_diag_wide(ctre_ref[...], SSM_STATE, SSM_GROUP, D_SSM).astype(BF16)
        cdim[...] = _block_diag_wide(ctim_ref[...], SSM_STATE, SSM_GROUP, D_SSM).astype(BF16)
        a_re = are_ref[...]
        a_im = aim_ref[...]
        dt = jnp.exp(ldt_ref[...])
        mag = jnp.exp(dt * a_re)
        ab_re = mag * jnp.cos(dt * a_im)
        ab_im = mag * jnp.sin(dt * a_im)
        den = a_re * a_re + a_im * a_im
        nr = ab_re - 1.0
        coef_re = (nr * a_re + ab_im * a_im) / den
        coef_im = (ab_im * a_re - nr * a_im) / den
        bbre[...] = (coef_re * bdre - coef_im * bdim).astype(BF16)
        bbim[...] = (coef_re * bdim + coef_im * bdre).astype(BF16)
        abre[...] = ab_re
        abim[...] = ab_im
        hre[...] = h0re_ref[...]
        him[...] = h0im_ref[...]
        if chained:
            def power(j, carry):
                pr, pi = carry
                pwre[pl.ds(j, 1), :] = pr
                pwim[pl.ds(j, 1), :] = pi
                return ab_re * pr - ab_im * pi, ab_re * pi + ab_im * pr

            lax.fori_loop(0, nt, power, (ab_re, ab_im))

    n = nb * nt
    tsh = nt.bit_length() - 1
    bsh = nb.bit_length() - 1
    r = lax.broadcasted_iota(jnp.int32, (n, n), 0)
    c = lax.broadcasted_iota(jnp.int32, (n, n), 1)
    to_time_major = jnp.where(((r & (nb - 1)) == (c >> tsh)) & ((r >> bsh) == (c & (nt - 1))), 1.0, 0.0).astype(F32)
    to_seq_major = jnp.where(((c & (nb - 1)) == (r >> tsh)) & ((c >> bsh) == (r & (nt - 1))), 1.0, 0.0).astype(F32)
    u = _dot01(to_time_major, u_ref[...], 3)
    ub = u.astype(BF16)
    for k in range(2):
        rs = slice(k * half_in, (k + 1) * half_in)
        cs = slice(k * half_st, (k + 1) * half_st)
        bure[:, cs] = jnp.dot(ub[:, rs], bbre[rs, cs], preferred_element_type=F32)
        buim[:, cs] = jnp.dot(ub[:, rs], bbim[rs, cs], preferred_element_type=F32)

    for cidx in range(D_STATE // SCAN_COLS):
        cs = slice(cidx * SCAN_COLS, (cidx + 1) * SCAN_COLS)
        ar = jnp.broadcast_to(abre[:, cs], (nb, SCAN_COLS))
        ai = jnp.broadcast_to(abim[:, cs], (nb, SCAN_COLS))

        def step(t, carry, cs=cs, ar=ar, ai=ai):
            hr, hi = carry
            rows = pl.ds(pl.multiple_of(t * nb, nb), nb)
            nr = ar * hr - ai * hi + bure[rows, cs]
            ni = ar * hi + ai * hr + buim[rows, cs]
            bure[rows, cs] = nr
            buim[rows, cs] = ni
            return nr, ni

        if not chained:
            hr, hi = lax.fori_loop(0, nt, step, (hre[:, cs], him[:, cs]), unroll=8)
            hre[:, cs] = hr
            him[:, cs] = hi
            continue

        zeros = jnp.zeros((nb, SCAN_COLS), F32)
        fr, fi = lax.fori_loop(0, nt, step, (zeros, zeros), unroll=8)
        ptr = pwre[nt - 1:nt, cs]
        pti = pwim[nt - 1:nt, cs]
        cr = hre[:, cs]
        ci = him[:, cs]
        seg = lax.broadcasted_iota(jnp.int32, (nb, SCAN_COLS), 0)
        inr, ini = zeros, zeros
        for s in range(nb):
            inr = jnp.where(seg == s, cr, inr)
            ini = jnp.where(seg == s, ci, ini)
            cr, ci = fr[s:s + 1, :] + ptr * cr - pti * ci, fi[s:s + 1, :] + ptr * ci + pti * cr
        hre[:, cs] = cr
        him[:, cs] = ci

        def fix(t, carry, cs=cs, inr=inr, ini=ini):
            rows = pl.ds(pl.multiple_of(t * nb, nb), nb)
            pr = pwre[pl.ds(t, 1), cs]
            pi = pwim[pl.ds(t, 1), cs]
            bure[rows, cs] += pr * inr - pi * ini
            buim[rows, cs] += pr * ini + pi * inr
            return carry

        lax.fori_loop(0, nt, fix, 0, unroll=8)

    halves = []
    for k in range(2):
        rs = slice(k * half_st, (k + 1) * half_st)
        cs = slice(k * half_in, (k + 1) * half_in)
        halves.append(jnp.dot(bure[:, rs].astype(BF16), cdre[rs, cs], preferred_element_type=F32)
                      - jnp.dot(buim[:, rs].astype(BF16), cdim[rs, cs], preferred_element_type=F32))
    y = jnp.concatenate(halves, axis=1) + d_ref[...] * u
    gl = 0.5 * y * (1.0 + jnp.tanh(math.sqrt(2.0 / math.pi) * (y + 0.044715 * (y * y * y))))
    gate = jax.nn.sigmoid(jnp.dot(gl.astype(BF16), wglu_ref[...], preferred_element_type=F32))
    o_ref[...] = _dot01(to_seq_major, gl * gate, 3)
    hre_out[...] = hre[...]
    him_out[...] = him[...]


def _s5(u_all, row_block0, nb, nt, n_steps, h0re, h0im, consts, chained):
    rows = nb * nt
    ns = 1 if chained else nb
    are, aim, ldt, btre, btim, ctre, ctim, dvec, wglu = consts
    in_specs = [pl.BlockSpec((rows, D_SSM), lambda i: (row_block0 + i, 0)),
                _const_spec((ns, D_STATE)), _const_spec((ns, D_STATE)),
                _const_spec((1, D_STATE)), _const_spec((1, D_STATE)), _const_spec((1, D_STATE)),
                _const_spec(btre.shape), _const_spec(btim.shape),
                _const_spec(ctre.shape), _const_spec(ctim.shape),
                _const_spec((1, D_SSM)), _const_spec((D_SSM, D_SSM))]
    args = [u_all, h0re, h0im, are, aim, ldt, btre, btim, ctre, ctim, dvec, wglu]
    state = jax.ShapeDtypeStruct((ns, D_STATE), F32)
    return pl.pallas_call(
        functools.partial(_s5_kernel, nb=nb, nt=nt, chained=chained),
        grid=(n_steps,),
        in_specs=in_specs,
        out_specs=(pl.BlockSpec((rows, D_SSM), lambda i: (i, 0)),
                   pl.BlockSpec((ns, D_STATE), lambda i: (0, 0)),
                   pl.BlockSpec((ns, D_STATE), lambda i: (0, 0))),
        out_shape=(jax.ShapeDtypeStruct((n_steps * rows, D_SSM), F32), state, state),
        scratch_shapes=[pltpu.VMEM((D_SSM, D_STATE), BF16), pltpu.VMEM((D_SSM, D_STATE), BF16),
                        pltpu.VMEM((D_STATE, D_SSM), BF16), pltpu.VMEM((D_STATE, D_SSM), BF16),
                        pltpu.VMEM((rows, D_STATE), F32), pltpu.VMEM((rows, D_STATE), F32),
                        pltpu.VMEM((ns, D_STATE), F32), pltpu.VMEM((ns, D_STATE), F32),
                        pltpu.VMEM((1, D_STATE), F32), pltpu.VMEM((1, D_STATE), F32),
                        pltpu.VMEM((nt, D_STATE), F32), pltpu.VMEM((nt, D_STATE), F32)],
        compiler_params=_params(("arbitrary",)),
        name="s5_prompt" if chained else "s5_sample",
    )(*args)


def _pool_kernel(cur_ref, *rest, seq_len, prev_rows, pos0, prompt):
    if prompt:
        prev_ref, buf_ref, pw_ref, ps_ref, o_ref = rest
    else:
        buf_ref, pw_ref, ps_ref, o_ref = rest
    i = pl.program_id(0)
    cur = cur_ref[...]
    n = cur.shape[0]
    if prompt:
        prev = jnp.where(i == 0, buf_ref[...], prev_ref[...])
    else:
        prev = buf_ref[...]
    npv = prev.shape[0]
    lsh = seq_len.bit_length() - 1
    psh = prev_rows.bit_length() - 1
    r = lax.broadcasted_iota(jnp.int32, (n, n), 0)
    c = lax.broadcasted_iota(jnp.int32, (n, n), 1)
    same = (r >> lsh) == (c >> lsh)
    d = r - c
    r1 = lax.broadcasted_iota(jnp.int32, (n, npv), 0)
    c1 = lax.broadcasted_iota(jnp.int32, (n, npv), 1)
    same1 = (r1 >> lsh) == (c1 >> psh)
    d1 = (r1 & (seq_len - 1)) + prev_rows - (c1 & (prev_rows - 1))
    local = lax.broadcasted_iota(jnp.int32, (n, 1), 0) & (seq_len - 1)
    pos = pos0 + i * n + local
    outs = []
    for gi, w in enumerate(POOL_WINDOWS):
        gs = slice(gi * POOL_GROUP, (gi + 1) * POOL_GROUP)
        band = jnp.where(same & (d >= 0) & (d < w), 1.0, 0.0).astype(F32)
        band_prev = jnp.where(same1 & (d1 < w), 1.0, 0.0).astype(F32)
        xg = cur[:, gs]
        wsum = _dot01(band, xg, 3) + _dot01(band_prev, prev[:, gs], 3)
        cnt = jnp.minimum(w, pos + 1).astype(F32)
        pooled = wsum / cnt - xg
        outs.append(jnp.dot(pooled.astype(BF16), pw_ref[gi], preferred_element_type=F32))
    o_ref[...] = jnp.concatenate(outs, axis=1) * ps_ref[...]


def _pool_prompt(p_all, buf_pad, pw, ps, t):
    tt = ROW_TILE
    ratio = tt // POOL_GROUP
    return pl.pallas_call(
        functools.partial(_pool_kernel, seq_len=tt, prev_rows=POOL_GROUP, pos0=0, prompt=True),
        grid=(t // tt,),
        in_specs=[pl.BlockSpec((tt, D_POOL), lambda i: (i, 0)),
                  pl.BlockSpec((POOL_GROUP, D_POOL), lambda i: (jnp.maximum(i * ratio - 1, 0), 0)),
                  _const_spec((POOL_GROUP, D_POOL)), _const_spec(pw.shape), _const_spec((1, D_POOL))],
        out_specs=pl.BlockSpec((tt, D_POOL), lambda i: (i, 0)),
        out_shape=jax.ShapeDtypeStruct((t, D_POOL), F32),
        compiler_params=_params(("parallel",)),
        name="pool_prompt",
    )(p_all, p_all, buf_pad, pw, ps)


def _pool_sample(p_all, buf_pad, pw, ps, t, nb, tn, past):
    rows = nb * tn
    blk = t // rows
    return pl.pallas_call(
        functools.partial(_pool_kernel, seq_len=tn, prev_rows=16, pos0=past, prompt=False),
        grid=(1,),
        in_specs=[pl.BlockSpec((rows, D_POOL), lambda i: (blk, 0)),
                  _const_spec(buf_pad.shape), _const_spec(pw.shape), _const_spec((1, D_POOL))],
        out_specs=pl.BlockSpec((rows, D_POOL), lambda i: (0, 0)),
        out_shape=jax.ShapeDtypeStruct((rows, D_POOL), F32),
        compiler_params=_params(("arbitrary",)),
        name="pool_sample",
    )(p_all, buf_pad, pw, ps)


def _outproj_kernel(xp_ref, xs_ref, ap_ref, as_ref, sp_ref, ss_ref, pp_ref, ps_ref, ong_ref, wout_ref, n2_ref,
                    wrh_ref, wrl_ref, rb_ref, xm_ref, xn_ref, ids_ref, gates_ref, *, prompt_tiles):
    g = ong_ref[...]
    sample = pl.program_id(0) >= prompt_tiles
    ma = _rms(jnp.where(sample, as_ref[...], ap_ref[...]), g[:, :D_ATTN]).astype(BF16)
    ms = _rms(jnp.where(sample, ss_ref[...], sp_ref[...]), g[:, D_ATTN:D_ATTN + D_SSM]).astype(BF16)
    mp = _rms(jnp.where(sample, ps_ref[...], pp_ref[...]), g[:, D_ATTN + D_SSM:]).astype(BF16)
    xm = (jnp.where(sample, xs_ref[...], xp_ref[...])
          + jnp.dot(ma, wout_ref[:D_ATTN, :], preferred_element_type=F32)
          + jnp.dot(ms, wout_ref[D_ATTN:D_ATTN + D_SSM, :], preferred_element_type=F32)
          + jnp.dot(mp, wout_ref[D_ATTN + D_SSM:, :], preferred_element_type=F32))
    xm_ref[...] = xm
    xn = _rms(xm, n2_ref[...])
    tm = xn.shape[0]
    for j in range(xn.shape[1] // ROW_CHUNK):
        xn_ref[_chunk_rows(j, tm), :] = xn[:, j * ROW_CHUNK:(j + 1) * ROW_CHUNK]
    xh = xn.astype(BF16)
    xl = (xn - xh.astype(F32)).astype(BF16)
    logits = (jnp.dot(xh, wrh_ref[...], preferred_element_type=F32)
              + jnp.dot(xl, wrh_ref[...], preferred_element_type=F32)
              + jnp.dot(xh, wrl_ref[...], preferred_element_type=F32))
    lt = jnp.transpose(logits)[:rb_ref.shape[0]] + rb_ref[...]
    ng, ne = N_EXPERT_GROUPS, EXPERTS_PER_GROUP
    gl = [lt[j:j + 1, :] for j in range(ng)]
    gmax = functools.reduce(jnp.maximum, gl)
    gsel = jnp.full(gmax.shape, ng - 1, jnp.int32)
    for j in range(ng - 2, -1, -1):
        gsel = jnp.where(gl[j] == gmax, j, gsel)
    gw = 1.0 / functools.reduce(jnp.add, [jnp.exp(v - gmax) for v in gl])
    es = []
    for j in range(ne):
        v = lt[ng + (ng - 1) * ne + j:ng + (ng - 1) * ne + j + 1, :]
        for gi in range(ng - 2, -1, -1):
            v = jnp.where(gsel == gi, lt[ng + gi * ne + j:ng + gi * ne + j + 1, :], v)
        es.append(v)

    def first_argmax(vals):
        vmax = functools.reduce(jnp.maximum, vals)
        idx = jnp.full(vmax.shape, len(vals) - 1, jnp.int32)
        for j in range(len(vals) - 2, -1, -1):
            idx = jnp.where(vals[j] == vmax, j, idx)
        return vmax, idx

    v1, i1 = first_argmax(es)
    v2, i2 = first_argmax([jnp.where(i1 == j, -jnp.inf, es[j]) for j in range(ne)])
    t = jnp.exp(v2 - v1)
    p1 = 1.0 / (1.0 + t)
    p2 = t / (1.0 + t)
    row = lax.broadcasted_iota(jnp.int32, ids_ref.shape, 0)
    ids_ref[...] = jnp.where(row == 0, gsel * ne + i1, jnp.where(row == 1, gsel * ne + i2, 0))
    gates_ref[...] = jnp.where(row == 0, p1 * gw, jnp.where(row == 1, p2 * gw, 0.0))


def _outproj(x_pair, m, mix_prompt, mix_sample, layer, ong, wout, n2, wr_hi, wr_lo, rb):
    d = x_pair[0].shape[1]
    tm = ROW_TILE
    pt = mix_prompt[0].shape[0] // tm
    x_specs, x_args = _x_specs(x_pair, tm, pt, lambda i: i)
    row = lambda w: pl.BlockSpec((tm, w), lambda i: (i, 0))
    prow = lambda w: pl.BlockSpec((tm, w), lambda i: (jnp.minimum(i, pt - 1), 0))
    srow = lambda w: pl.BlockSpec((tm, w), lambda i: (jnp.maximum(i - pt, 0), 0))
    col = pl.BlockSpec((8, tm), lambda i: (0, i))
    return pl.pallas_call(
        functools.partial(_outproj_kernel, prompt_tiles=pt),
        grid=(m // tm,),
        in_specs=[*x_specs, prow(D_ATTN), srow(D_ATTN), prow(D_SSM), srow(D_SSM), prow(D_POOL), srow(D_POOL),
                  _const_spec((1, d)),
                  _layer_spec(wout, layer), _const_spec((1, d)), _const_spec(wr_hi.shape),
                  _const_spec(wr_lo.shape), _const_spec(rb.shape)],
        out_specs=(row(d), pl.BlockSpec((tm * ROW_CHUNKS, ROW_CHUNK), lambda i: (i, 0)), col, col),
        out_shape=(jax.ShapeDtypeStruct((m, d), F32), jax.ShapeDtypeStruct((m * ROW_CHUNKS, ROW_CHUNK), F32),
                   jax.ShapeDtypeStruct((8, m), jnp.int32), jax.ShapeDtypeStruct((8, m), F32)),
        compiler_params=_params(("parallel",)),
        name="outproj_router",
    )(*x_args, mix_prompt[0], mix_sample[0], mix_prompt[1], mix_sample[1], mix_prompt[2], mix_sample[2],
      ong, wout, n2, wr_hi, wr_lo, rb)


def _token_copy(src_hbm, tok, dst, r, sem):
    return pltpu.make_async_copy(src_hbm.at[pl.ds(pl.multiple_of(tok * ROW_CHUNKS, ROW_CHUNKS), ROW_CHUNKS), :],
                                 dst.at[pl.ds(pl.multiple_of(r * ROW_CHUNKS, ROW_CHUNKS), ROW_CHUNKS), :], sem)


def _gather_tokens(src_hbm, idx_ref, first_idx, dst, sem, n):
    def issue(r, carry):
        _token_copy(src_hbm, idx_ref[first_idx + r], dst, r, sem).start()
        return carry

    lax.fori_loop(0, n, issue, 0, unroll=8)


def _wait_tokens(src_hbm, dst, sem):
    pltpu.make_async_copy(src_hbm.at[pl.ds(0, dst.shape[0]), :], dst, sem).wait()


def _expert_kernel(vt_ref, ve_ref, vlo_ref, vhi_ref, tok_ref, x_hbm, gate_ref, wg_ref, wu_ref, wd_ref,
                   o_ref, xs, wgb, wub, wdb, sem, *, n_tiles):
    v = pl.program_id(0)
    tm = gate_ref.shape[0]
    lo = vlo_ref[v]
    hi = vhi_ref[v]
    tile = vt_ref[v]
    base = tile * tm
    slot = tile & 1
    nonempty = hi > lo
    first = jnp.logical_and(nonempty, lo == base)
    changed = jnp.logical_or(v == 0, ve_ref[v] != ve_ref[jnp.maximum(v - 1, 0)])

    @pl.when(changed)
    def _():
        wgb[...] = wg_ref[...].astype(BF16)
        wub[...] = wu_ref[...].astype(BF16)
        wdb[...] = wd_ref[...].astype(BF16)

    @pl.when(v == 0)
    def _():
        _gather_tokens(x_hbm, tok_ref, 0, xs.at[0], sem.at[0], tm)

    @pl.when(first)
    def _():
        _wait_tokens(x_hbm, xs.at[slot], sem.at[slot])

    @pl.when(jnp.logical_and(first, tile + 1 < n_tiles))
    def _():
        _gather_tokens(x_hbm, tok_ref, base + tm, xs.at[1 - slot], sem.at[1 - slot], tm)

    @pl.when(nonempty)
    def _():
        xt = xs.at[slot]
        xb = jnp.concatenate([xt[_chunk_rows(j, tm), :] for j in range(ROW_CHUNKS)], axis=1).astype(BF16)
        hg = jnp.dot(xb, wgb[...], preferred_element_type=F32)
        hu = jnp.dot(xb, wub[...], preferred_element_type=F32)
        row = base + lax.broadcasted_iota(jnp.int32, (tm, 1), 0)
        keep = jnp.logical_and(row >= lo, row < hi)
        act = jnp.where(keep, (hg * jax.nn.sigmoid(hg)) * hu * gate_ref[...], 0.0)
        y = jnp.dot(act.astype(BF16), wdb[...], preferred_element_type=F32)

        @pl.when(first)
        def _():
            for j in range(ROW_CHUNKS):
                o_ref[_chunk_rows(j, tm), :] = y[:, j * ROW_CHUNK:(j + 1) * ROW_CHUNK]

        @pl.when(jnp.logical_not(first))
        def _():
            for j in range(ROW_CHUNKS):
                o_ref[_chunk_rows(j, tm), :] += y[:, j * ROW_CHUNK:(j + 1) * ROW_CHUNK]


def _expert_ffn(plan, xn_rows, layer, wg, wu, wd):
    v_tile, v_exp, v_lo, v_hi, tok_sorted, gate_sorted = plan
    n_flat = tok_sorted.shape[0]
    d = wg.shape[2]
    tm = ROW_TILE
    grid_spec = pltpu.PrefetchScalarGridSpec(
        num_scalar_prefetch=5,
        grid=(v_tile.shape[0],),
        in_specs=[pl.BlockSpec(memory_space=pl.ANY),
                  pl.BlockSpec((tm, 1), lambda v, vt, ve, lo, hi, tok: (vt[v], 0)),
                  pl.BlockSpec((None, None, d, D_EXPERT), lambda v, vt, ve, lo, hi, tok: (layer, ve[v], 0, 0)),
                  pl.BlockSpec((None, None, d, D_EXPERT), lambda v, vt, ve, lo, hi, tok: (layer, ve[v], 0, 0)),
                  pl.BlockSpec((None, None, D_EXPERT, d), lambda v, vt, ve, lo, hi, tok: (layer, ve[v], 0, 0))],
        out_specs=pl.BlockSpec((tm * ROW_CHUNKS, ROW_CHUNK), lambda v, vt, ve, lo, hi, tok: (vt[v], 0)),
        scratch_shapes=[pltpu.VMEM((2, tm * ROW_CHUNKS, ROW_CHUNK), F32),
                        pltpu.VMEM((d, D_EXPERT), BF16), pltpu.VMEM((d, D_EXPERT), BF16),
                        pltpu.VMEM((D_EXPERT, d), BF16), pltpu.SemaphoreType.DMA((2,))])
    return pl.pallas_call(
        functools.partial(_expert_kernel, n_tiles=n_flat // tm),
        grid_spec=grid_spec,
        out_shape=jax.ShapeDtypeStruct((n_flat * ROW_CHUNKS, ROW_CHUNK), F32),
        compiler_params=_params(("arbitrary",)),
        name="expert_ffn",
    )(v_tile, v_exp, v_lo, v_hi, tok_sorted, xn_rows, gate_sorted, wg, wu, wd)


def _combine_kernel(d1_ref, d2_ref, xm_ref, ys_hbm, *rest, n_tiles, prompt_tiles):
    out_refs, (buf1, buf2, sem) = rest[:-3], rest[-3:]
    i = pl.program_id(0)
    tm = xm_ref.shape[0]
    slot = i & 1

    def start(tile, s):
        _gather_tokens(ys_hbm, d1_ref, tile * tm, buf1.at[s], sem.at[0, s], tm)
        _gather_tokens(ys_hbm, d2_ref, tile * tm, buf2.at[s], sem.at[1, s], tm)

    pl.when(i == 0)(lambda: start(0, 0))
    _wait_tokens(ys_hbm, buf1.at[slot], sem.at[0, slot])
    _wait_tokens(ys_hbm, buf2.at[slot], sem.at[1, slot])
    pl.when(i + 1 < n_tiles)(lambda: start(i + 1, 1 - slot))
    rows1 = buf1.at[slot]
    rows2 = buf2.at[slot]

    def write(o_ref):
        for j in range(ROW_CHUNKS):
            cs = slice(j * ROW_CHUNK, (j + 1) * ROW_CHUNK)
            o_ref[:, cs] = xm_ref[:, cs] + rows1[_chunk_rows(j, tm), :] + rows2[_chunk_rows(j, tm), :]

    if prompt_tiles is None:
        write(out_refs[0])
    else:
        pl.when(i < prompt_tiles)(lambda: write(out_refs[0]))
        pl.when(i >= prompt_tiles)(lambda: write(out_refs[1]))


def _combine(dest1, dest2, xm, ys_rows, split_rows=None):
    m, d = xm.shape
    tm = ROW_TILE
    row = pl.BlockSpec((tm, d), lambda i, a, b: (i, 0))
    if split_rows is None:
        pt = None
        out_specs = row
        out_shape = jax.ShapeDtypeStruct((m, d), F32)
    else:
        pt = split_rows // tm
        out_specs = (pl.BlockSpec((tm, d), lambda i, a, b: (jnp.minimum(i, pt - 1), 0)),
                     pl.BlockSpec((tm, d), lambda i, a, b: (jnp.maximum(i - pt, 0), 0)))
        out_shape = (jax.ShapeDtypeStruct((split_rows, d), F32), jax.ShapeDtypeStruct((m - split_rows, d), F32))
    buf = pltpu.VMEM((2, tm * ROW_CHUNKS, ROW_CHUNK), F32)
    grid_spec = pltpu.PrefetchScalarGridSpec(
        num_scalar_prefetch=2,
        grid=(m // tm,),
        in_specs=[row, pl.BlockSpec(memory_space=pl.ANY)],
        out_specs=out_specs,
        scratch_shapes=[buf, buf, pltpu.SemaphoreType.DMA((2, 2))])
    return pl.pallas_call(
        functools.partial(_combine_kernel, n_tiles=m // tm, prompt_tiles=pt),
        grid_spec=grid_spec,
        out_shape=out_shape,
        compiler_params=_params(("arbitrary",)),
        name="moe_combine",
    )(dest1, dest2, xm, ys_rows)


def _dispatch_plan(ids, gates, m):
    tm = ROW_TILE
    n_flat = 2 * m
    n_tiles = n_flat // tm
    e_flat = ids[:2].reshape(n_flat)
    g_flat = gates[:2].reshape(n_flat)
    tok = jnp.arange(n_flat, dtype=jnp.int32) % m
    _, tok_sorted, gate_sorted = lax.sort((e_flat, tok, g_flat), num_keys=1, is_stable=True)
    onehot = (e_flat[:, None] == jnp.arange(N_EXPERTS, dtype=jnp.int32)[None, :]).astype(jnp.int32)
    csum = jnp.cumsum(onehot, axis=0)
    counts = csum[-1]
    row_end = jnp.cumsum(counts)
    row_start = row_end - counts
    dest = jnp.sum(onehot * (row_start[None, :] + csum - 1), axis=1).astype(jnp.int32)
    cuts = jnp.sort(jnp.concatenate([jnp.arange(n_tiles, dtype=jnp.int32) * tm, row_start[1:].astype(jnp.int32)]))
    v_lo = cuts
    v_hi = jnp.concatenate([cuts[1:], jnp.full((1,), n_flat, jnp.int32)])
    v_tile = jnp.minimum(v_lo // tm, n_tiles - 1).astype(jnp.int32)
    v_exp = jnp.minimum(jnp.sum((v_lo[:, None] >= row_end[None, :]).astype(jnp.int32), axis=1), N_EXPERTS - 1)
    plan = (v_tile, v_exp.astype(jnp.int32), v_lo, v_hi, tok_sorted, gate_sorted.reshape(n_flat, 1))
    return plan, dest[:m], dest[m:]


def _tall_blocks(w):
    g, c, r = w.shape
    tall = jnp.swapaxes(w, 1, 2).reshape(g * r, c)
    return jnp.tile(tall, (1, 128 // c))


def kernel(x_prompt, x_sample, cache_k, cache_v, cache_logf, page_table, state_ssm_re, state_ssm_im, state_pool, norm1_g, w_in, f_bias, q_gain, k_gain, ssm_a_re, ssm_a_im, ssm_log_dt, ssm_b_re, ssm_b_im, ssm_c_re, ssm_c_im, ssm_d, ssm_w_glu, pool_w, pool_scale, out_norm_g, w_out, norm2_g, router_group_w, router_group_b, router_expert_w, router_expert_b, moe_w_gate, moe_w_up, moe_w_down):
    n_prompt, t, d = x_prompt.shape
    nb, tn, _ = x_sample.shape
    depth = w_in.shape[0]
    n_phys = cache_k.shape[1]
    past = page_table.shape[1] * cache_k.shape[2]
    assert n_prompt == 1
    ms = nb * tn
    m = t + ms
    x_pair = (x_prompt.reshape(t, d), x_sample.reshape(ms, d))

    ck = cache_k.reshape(depth * n_phys, PAGE_SIZE * N_HEADS, HEAD_DIM)
    cv = cache_v.reshape(depth * n_phys, PAGE_SIZE * N_HEADS, HEAD_DIM)
    clf = jnp.swapaxes(cache_logf, 2, 3).reshape(depth * n_phys, N_HEADS, PAGE_SIZE)
    zero_state = jnp.zeros((1, D_STATE), F32)
    zero_buf = jnp.zeros((POOL_GROUP, D_POOL), F32)

    wqkv = w_in[:, :, :3 * D_ATTN].astype(BF16)
    wup = w_in[:, :, 3 * D_ATTN + N_HEADS:].astype(BF16)
    wft = jnp.pad(jnp.swapaxes(w_in[:, :, 3 * D_ATTN:3 * D_ATTN + N_HEADS], 1, 2), ((0, 0), (0, 8), (0, 0))).astype(BF16)
    wout = w_out.astype(BF16)

    outs_p, outs_s = [], []
    for l in range(depth):
        q, k_p, v_p, k_s, v_s, kb, vb, lft, u, p_in = _proj(
            x_pair, t, m, l, norm1_g[l].reshape(1, d), wqkv, wup, wft, f_bias[l].reshape(N_HEADS, 1),
            q_gain[l].reshape(1, HEAD_DIM), k_gain[l].reshape(1, HEAD_DIM))

        c_prompt = _prompt_cumsum(lft, t)
        qk_bound = 1.02 * (HEAD_DIM ** 0.5) * jnp.max(jnp.abs(q_gain[l])) * jnp.max(jnp.abs(k_gain[l]))
        a_p = _fox_prompt(q, kb, vb, c_prompt, t, qk_bound)
        lfn = lft[:, t:].reshape(N_HEADS, nb, tn).transpose(1, 0, 2)
        a_s = _fox_sample(q, k_s, v_s, lfn, ck, cv, clf, page_table, l, n_phys, t)

        consts = (ssm_a_re[l].reshape(1, D_STATE), ssm_a_im[l].reshape(1, D_STATE),
                  jnp.repeat(ssm_log_dt[l], SSM_STATE).reshape(1, D_STATE),
                  _tall_blocks(ssm_b_re[l]), _tall_blocks(ssm_b_im[l]),
                  _tall_blocks(ssm_c_re[l]), _tall_blocks(ssm_c_im[l]),
                  ssm_d[l].reshape(1, D_SSM), ssm_w_glu[l].astype(BF16))
        s_p, hre_p, him_p = _s5(u, 0, SCAN_SEGMENTS, ROW_TILE // SCAN_SEGMENTS, t // ROW_TILE, zero_state,
                                zero_state, consts, True)
        s_s, hre_s, him_s = _s5(u, t // ms, nb, tn, 1, state_ssm_re[l].reshape(nb, D_STATE),
                                state_ssm_im[l].reshape(nb, D_STATE), consts, False)

        pw = pool_w[l].astype(BF16)
        ps = pool_scale[l].reshape(1, D_POOL)
        po_p = _pool_prompt(p_in, zero_buf, pw, ps, t)
        buf_s = jnp.pad(state_pool[l], ((0, 0), (1, 0), (0, 0))).reshape(nb * 16, D_POOL)
        po_s = _pool_sample(p_in, buf_s, pw, ps, t, nb, tn, past)

        n_route = N_EXPERT_GROUPS + N_EXPERTS
        wr = jnp.pad(jnp.concatenate([router_group_w[l], router_expert_w[l]], axis=1), ((0, 0), (0, 128 - n_route)))
        rb = jnp.pad(jnp.concatenate([router_group_b[l], router_expert_b[l]]), (0, 24 - n_route)).reshape(24, 1)
        wr_hi = wr.astype(BF16)
        wr_lo = (wr - wr_hi.astype(F32)).astype(BF16)
        xm, xn, ids, gates = _outproj(x_pair, m, (a_p, s_p, po_p), (a_s, s_s, po_s), l, out_norm_g[l].reshape(1, d),
                                      wout, norm2_g[l].reshape(1, d), wr_hi, wr_lo, rb)

        plan, dest1, dest2 = _dispatch_plan(ids, gates, m)
        ys = _expert_ffn(plan, xn, l, moe_w_gate, moe_w_up, moe_w_down)
        if l + 1 < depth:
            x_pair = (_combine(dest1, dest2, xm, ys), None)
        else:
            y_prompt, y_sample = _combine(dest1, dest2, xm, ys, split_rows=t)

        lf_rows = lft.T
        outs_p.append((k_p.reshape(1, t, N_HEADS, HEAD_DIM), v_p.reshape(1, t, N_HEADS, HEAD_DIM),
                       lf_rows[:t].reshape(1, t, N_HEADS),
                       hre_p.reshape(1, SSM_GROUPS, SSM_STATE), him_p.reshape(1, SSM_GROUPS, SSM_STATE),
                       p_in[t - POOL_BUF:t].reshape(1, POOL_BUF, D_POOL)))
        p_s = p_in[t:].reshape(nb, tn, D_POOL)
        msh = ms * N_HEADS
        outs_s.append((k_s[:msh].reshape(nb, tn, N_HEADS, HEAD_DIM), v_s[:msh].reshape(nb, tn, N_HEADS, HEAD_DIM),
                       lf_rows[t:].reshape(nb, tn, N_HEADS),
                       hre_s.reshape(nb, SSM_GROUPS, SSM_STATE), him_s.reshape(nb, SSM_GROUPS, SSM_STATE),
                       jnp.concatenate([state_pool[l], p_s], axis=1)[:, -POOL_BUF:]))

    stack = lambda outs, i: jnp.stack([o[i] for o in outs], axis=0)
    return (y_prompt.reshape(1, t, d), y_sample.reshape(nb, tn, d),
            *[stack(outs_p, i) for i in range(6)], *[stack(outs_s, i) for i in range(6)])
```

```python
import functools
import math

import jax
import jax.numpy as jnp
from jax import lax
from jax.experimental import pallas as pl
from jax.experimental.pallas import tpu as pltpu

F32 = jnp.float32
BF16 = jnp.bfloat16
HIGHEST = lax.Precision.HIGHEST
NT_DIMS = (((1,), (1,)), ((), ()))

EPS = 1e-6
HEAD_DIM = 128
N_HEADS = 8
D_ATTN = N_HEADS * HEAD_DIM
D_SSM = 512
SSM_GROUPS = 32
SSM_GROUP = 16
SSM_STATE = 64
D_STATE = SSM_GROUPS * SSM_STATE
D_POOL = 512
POOL_WINDOWS = (2, 4, 8, 16)
POOL_GROUP = 128
POOL_BUF = 15
N_EXPERT_GROUPS = 4
EXPERTS_PER_GROUP = 4
N_EXPERTS = 16
D_EXPERT = 512
PAGE_SIZE = 128
NEG_BIG = -1e30
SKIP_MARGIN = 30.0

ROW_TILE = 256
ROW_CHUNK = 128
ROW_CHUNKS = 2048 // ROW_CHUNK
ATTN_TILE = 512
PAGES_PER_STEP = 16
SCAN_COLS = 512
SCAN_SEGMENTS = 8
VMEM_LIMIT = 56 * 1024 * 1024


def _params(sem, vmem=VMEM_LIMIT):
    return pltpu.CompilerParams(dimension_semantics=sem, vmem_limit_bytes=vmem)


def _const_spec(shape):
    nd = len(shape)
    return pl.BlockSpec(shape, lambda *_: (0,) * nd, pipeline_mode=pl.Buffered(1))


def _layer_spec(stacked, layer):
    nd = stacked.ndim - 1
    return pl.BlockSpec((None,) + stacked.shape[1:], lambda *_: (layer,) + (0,) * nd, pipeline_mode=pl.Buffered(1))


def _rms(y, g):
    return y * lax.rsqrt(jnp.mean(y * y, axis=-1, keepdims=True) + EPS) * g


def _dot01(mat01, x, pieces):
    m = mat01.astype(BF16)
    acc = None
    rem = x
    for _ in range(pieces):
        part = rem.astype(BF16)
        rem = rem - part.astype(F32)
        d = jnp.dot(m, part, preferred_element_type=F32)
        acc = d if acc is None else acc + d
    return acc


def _chunk_rows(j, n):
    return pl.ds(j, n, stride=ROW_CHUNKS)


def _head_rows(h, n):
    return pl.ds(h, n, stride=N_HEADS)


def _proj_kernel(xp_ref, xs_ref, g1_ref, wqkv_ref, wup_ref, wft_ref, fb_ref, qg_ref, kg_ref,
                 q_ref, kp_ref, vp_ref, ks_ref, vs_ref, kb_ref, vb_ref, lf_ref, u_ref, p_ref, *, sample_tiles):
    x = jnp.where(pl.program_id(0) < sample_tiles, xs_ref[...], xp_ref[...])
    hb = _rms(x, g1_ref[...]).astype(BF16)
    tm = hb.shape[0]
    qg = qg_ref[...]
    kg = kg_ref[...]
    k_heads, v_heads = [], []
    for c in range(D_ATTN // 256):
        lo = c * 256
        qc = jnp.dot(hb, wqkv_ref[:, lo:lo + 256], preferred_element_type=F32)
        kc = jnp.dot(hb, wqkv_ref[:, D_ATTN + lo:D_ATTN + lo + 256], preferred_element_type=F32)
        vc = jnp.dot(hb, wqkv_ref[:, 2 * D_ATTN + lo:2 * D_ATTN + lo + 256], preferred_element_type=F32)
        for j in range(2):
            a = lo + j * HEAD_DIM
            q_ref[:, a:a + HEAD_DIM] = _rms(qc[:, j * HEAD_DIM:(j + 1) * HEAD_DIM], qg)
            kn = _rms(kc[:, j * HEAD_DIM:(j + 1) * HEAD_DIM], kg)
            kb_ref[:, a:a + HEAD_DIM] = kn.astype(BF16)
            k_heads.append(kn)
            v_heads.append(vc[:, j * HEAD_DIM:(j + 1) * HEAD_DIM])
        vb_ref[:, lo:lo + 256] = vc.astype(BF16)

    for h in range(N_HEADS):
        for k_ref, v_ref in ((kp_ref, vp_ref), (ks_ref, vs_ref)):
            k_ref[_head_rows(h, tm), :] = k_heads[h]
            v_ref[_head_rows(h, tm), :] = v_heads[h]
    u_ref[...] = jnp.dot(hb, wup_ref[:, :D_SSM], preferred_element_type=F32)
    p_ref[...] = jnp.dot(hb, wup_ref[:, D_SSM:], preferred_element_type=F32)
    ft = lax.dot_general(wft_ref[...], hb, NT_DIMS, preferred_element_type=F32)[:N_HEADS]
    z = ft + fb_ref[...]
    lf_ref[...] = jnp.minimum(z, 0.0) - jnp.log1p(jnp.exp(-jnp.abs(z)))


def _x_specs(x_pair, tm, pt, tile):
    xp, xs = x_pair
    off = 0 if xs is not None else pt
    st = (xs if xs is not None else xp).shape[0] // tm - off
    d = xp.shape[1]
    specs = [pl.BlockSpec((tm, d), lambda i, *_: (jnp.clip(tile(i), 0, pt - 1), 0)),
             pl.BlockSpec((tm, d), lambda i, *_: (off + jnp.clip(tile(i) - pt, 0, st - 1), 0))]
    return specs, (xp, xs if xs is not None else xp)


def _proj(x_pair, t, m, layer, g1, wqkv, wup, wft, fb, qg, kg):
    d = x_pair[0].shape[1]
    tm = ROW_TILE
    pt = t // tm
    st = (m - t) // tm
    tile = lambda i: jnp.where(i < st, pt + i, i - st)
    row = lambda w: pl.BlockSpec((tm, w), lambda i: (tile(i), 0))
    heads_p = pl.BlockSpec((tm * N_HEADS, HEAD_DIM), lambda i: (jnp.maximum(i - st, 0), 0))
    heads_s = pl.BlockSpec((tm * N_HEADS, HEAD_DIM), lambda i: (jnp.minimum(i, st), 0))
    kv_p = jax.ShapeDtypeStruct((t * N_HEADS, HEAD_DIM), F32)
    kv_s = jax.ShapeDtypeStruct(((m - t + tm) * N_HEADS, HEAD_DIM), F32)
    outs = (
        jax.ShapeDtypeStruct((m, D_ATTN), F32), kv_p, kv_p, kv_s, kv_s, jax.ShapeDtypeStruct((m, D_ATTN), BF16),
        jax.ShapeDtypeStruct((m, D_ATTN), BF16), jax.ShapeDtypeStruct((N_HEADS, m), F32),
        jax.ShapeDtypeStruct((m, D_SSM), F32), jax.ShapeDtypeStruct((m, D_POOL), F32))
    x_specs, x_args = _x_specs(x_pair, tm, pt, tile)
    return pl.pallas_call(
        functools.partial(_proj_kernel, sample_tiles=st),
        grid=(m // tm,),
        in_specs=[*x_specs, _const_spec((1, d)), _layer_spec(wqkv, layer), _layer_spec(wup, layer),
                  _layer_spec(wft, layer), _const_spec((N_HEADS, 1)), _const_spec((1, HEAD_DIM)),
                  _const_spec((1, HEAD_DIM))],
        out_specs=(row(D_ATTN), heads_p, heads_p, heads_s, heads_s, row(D_ATTN), row(D_ATTN),
                   pl.BlockSpec((N_HEADS, tm), lambda i: (0, tile(i))), row(D_SSM), row(D_POOL)),
        out_shape=outs,
        compiler_params=_params(("arbitrary",)),
        name="proj",
    )(*x_args, g1, wqkv, wup, wft, fb, qg, kg)


def _cumsum_kernel(lf_ref, c_ref):
    t = c_ref.shape[1]
    w = ATTN_TILE
    r = lax.broadcasted_iota(jnp.int32, (w, w), 0)
    c = lax.broadcasted_iota(jnp.int32, (w, w), 1)
    upper = jnp.where(r <= c, 1.0, 0.0).astype(F32)
    carry = jnp.zeros((N_HEADS, 1), F32)
    for s in range(t // w):
        blk = jnp.dot(lf_ref[:, s * w:(s + 1) * w], upper, precision=HIGHEST,
                      preferred_element_type=F32) + carry
        c_ref[:, s * w:(s + 1) * w] = blk
        carry = blk[:, w - 1:w]


def _prompt_cumsum(lft, t):
    return pl.pallas_call(
        _cumsum_kernel,
        grid=(1,),
        in_specs=[pl.BlockSpec((N_HEADS, t), lambda i: (0, 0))],
        out_specs=pl.BlockSpec((N_HEADS, t), lambda i: (0, 0)),
        out_shape=jax.ShapeDtypeStruct((N_HEADS, t), F32),
        compiler_params=_params(("arbitrary",)),
        name="prompt_cumsum",
    )(lft)


def _fox_prompt_kernel(kstart_ref, q_ref, k_ref, v_ref, c_ref, o_ref):
    qi = pl.program_id(1)
    first_tile = kstart_ref[pl.program_id(0), qi]
    tq = q_ref.shape[0]
    q = (q_ref[...] * (HEAD_DIM ** -0.5)).astype(BF16)

    def tile(ki, carry, masked):
        m, l, acc = carry
        off = pl.multiple_of(ki * tq, tq)
        kt = k_ref[pl.ds(off, tq), :]
        vt = v_ref[pl.ds(off, tq), :]
        s = lax.dot_general(q, kt, NT_DIMS, preferred_element_type=F32) - c_ref[ki]
        if masked:
            r = lax.broadcasted_iota(jnp.int32, s.shape, 0)
            c = lax.broadcasted_iota(jnp.int32, s.shape, 1)
            s = jnp.where(c <= r, s, NEG_BIG)
        m_new = jnp.maximum(m, jnp.max(s, axis=-1, keepdims=True))
        alpha = jnp.exp(m - m_new)
        p = jnp.exp(s - m_new)
        l = alpha * l + jnp.sum(p, axis=-1, keepdims=True)
        acc = alpha * acc + jnp.dot(p.astype(BF16), vt, preferred_element_type=F32)
        return m_new, l, acc

    init = (jnp.full((tq, 1), NEG_BIG, F32), jnp.zeros((tq, 1), F32), jnp.zeros((tq, HEAD_DIM), F32))
    carry = lax.fori_loop(first_tile, qi, lambda ki, cr: tile(ki, cr, False), init)
    _, l, acc = tile(qi, carry, True)
    o_ref[...] = acc / l


def _negligible_key_tiles(c, qk_bound, tq):
    c_first = c[:, ::tq]
    c_last = c[:, tq - 1::tq]
    nq = c_first.shape[1]
    gap = c_first[:, :, None] - c_last[:, None, :]
    earlier = jnp.arange(nq)[None, :] < jnp.arange(nq)[:, None]
    skip = (gap < -(2.0 * qk_bound + SKIP_MARGIN)) & earlier[None]
    return jnp.sum(skip.astype(jnp.int32), axis=2)


def _fox_prompt(q, kb, vb, c, t, qk_bound):
    tq = ATTN_TILE
    nq = t // tq
    c4 = c.reshape(N_HEADS, nq, 1, tq)
    kstart = _negligible_key_tiles(c, qk_bound, tq)
    grid_spec = pltpu.PrefetchScalarGridSpec(
        num_scalar_prefetch=1,
        grid=(N_HEADS, nq),
        in_specs=[pl.BlockSpec((tq, HEAD_DIM), lambda h, i, ks: (i, h)),
                  pl.BlockSpec((t, HEAD_DIM), lambda h, i, ks: (0, h)),
                  pl.BlockSpec((t, HEAD_DIM), lambda h, i, ks: (0, h)),
                  pl.BlockSpec((None, nq, 1, tq), lambda h, i, ks: (h, 0, 0, 0))],
        out_specs=pl.BlockSpec((tq, HEAD_DIM), lambda h, i, ks: (i, h)))
    return pl.pallas_call(
        _fox_prompt_kernel,
        grid_spec=grid_spec,
        out_shape=jax.ShapeDtypeStruct((t, D_ATTN), F32),
        compiler_params=_params(("parallel", "parallel")),
        name="fox_prompt",
    )(kstart, q, kb, vb, c4)


def _fox_sample_kernel(pt_ref, q_ref, kn_ref, vn_ref, lfn_ref, *rest, pg):
    del pt_ref
    k_refs = rest[:pg]
    v_refs = rest[pg:2 * pg]
    lf_refs = rest[2 * pg:3 * pg]
    o_ref = rest[3 * pg]
    m_scr, l_scr, acc_scr, c_scr = rest[3 * pg + 1:]
    g = pl.program_id(1)
    tn = q_ref.shape[0]
    scale = HEAD_DIM ** -0.5

    @pl.when(g == 0)
    def _():
        m_scr[...] = jnp.full(m_scr.shape, NEG_BIG, F32)
        l_scr[...] = jnp.zeros(l_scr.shape, F32)
        acc_scr[...] = jnp.zeros(acc_scr.shape, F32)
        c_scr[...] = jnp.zeros(c_scr.shape, F32)

    r = lax.broadcasted_iota(jnp.int32, (PAGE_SIZE, PAGE_SIZE), 0)
    c = lax.broadcasted_iota(jnp.int32, (PAGE_SIZE, PAGE_SIZE), 1)
    upper = jnp.where(r <= c, 1.0, 0.0).astype(BF16)
    lf = jnp.concatenate([lf_refs[j][...] for j in range(pg)], axis=0)
    hi = lf.astype(BF16).astype(F32)
    rem = lf - hi
    mid = rem.astype(BF16).astype(F32)
    pieces = jnp.concatenate([hi, mid, rem - mid], axis=0).astype(BF16)
    cs = jnp.dot(pieces, upper, preferred_element_type=F32)
    n = pg * N_HEADS
    c_in = cs[:n] + cs[n:2 * n] + cs[2 * n:]
    carry = c_scr[...]
    c_pages = []
    for j in range(pg):
        cp = c_in[j * N_HEADS:(j + 1) * N_HEADS] + carry
        carry = cp[:, PAGE_SIZE - 1:PAGE_SIZE]
        c_pages.append(cp)
    c_scr[...] = carry

    q = (q_ref[...] * scale).astype(BF16)
    s = jnp.concatenate(
        [jnp.concatenate(
            [lax.dot_general(q[:, h * HEAD_DIM:(h + 1) * HEAD_DIM], k_refs[j][_head_rows(h, PAGE_SIZE), :].astype(BF16), NT_DIMS,
                             preferred_element_type=F32) - c_pages[j][h:h + 1, :] for j in range(pg)], axis=1)
         for h in range(N_HEADS)], axis=0)
    m_old = m_scr[...]
    m_new = jnp.maximum(m_old, jnp.max(s, axis=-1, keepdims=True))
    alpha = jnp.exp(m_old - m_new)
    p = jnp.exp(s - m_new)
    l_scr[...] = alpha * l_scr[...] + jnp.sum(p, axis=-1, keepdims=True)
    pvs = []
    for h in range(N_HEADS):
        ph = p[h * tn:(h + 1) * tn].astype(BF16)
        pv = jnp.dot(ph[:, :PAGE_SIZE], v_refs[0][_head_rows(h, PAGE_SIZE), :].astype(BF16),
                     preferred_element_type=F32)
        for j in range(1, pg):
            pv = pv + jnp.dot(ph[:, j * PAGE_SIZE:(j + 1) * PAGE_SIZE],
                              v_refs[j][_head_rows(h, PAGE_SIZE), :].astype(BF16),
                              preferred_element_type=F32)
        pvs.append(pv)
    acc_scr[...] = alpha * acc_scr[...] + jnp.concatenate(pvs, axis=0)
    m_scr[...] = m_new

    @pl.when(g == pl.num_programs(1) - 1)
    def _():
        lfn = lfn_ref[...]
        run = c_scr[...]
        c_new = []
        for j in range(tn):
            run = run + lfn[:, j:j + 1]
            c_new.append(run)
        qf = q_ref[...] * scale
        row = lax.broadcasted_iota(jnp.int32, (tn, 1), 0)
        for h in range(N_HEADS):
            hs = slice(h * HEAD_DIM, (h + 1) * HEAD_DIM)
            rows = slice(h * tn, (h + 1) * tn)
            qh = qf[:, hs]
            kh = kn_ref[_head_rows(h, tn), :]
            vh = vn_ref[_head_rows(h, tn), :]
            cols = []
            for j in range(tn):
                sj = jnp.sum(qh * kh[j:j + 1, :], axis=-1, keepdims=True) - c_new[j][h:h + 1, :]
                cols.append(jnp.where(row >= j, sj, NEG_BIG))
            m_old = m_scr[rows, :]
            m_fin = m_old
            for sj in cols:
                m_fin = jnp.maximum(m_fin, sj)
            alpha = jnp.exp(m_old - m_fin)
            l = alpha * l_scr[rows, :]
            acc = alpha * acc_scr[rows, :]
            for j, sj in enumerate(cols):
                pj = jnp.exp(sj - m_fin)
                l = l + pj
                acc = acc + pj * vh[j:j + 1, :]
            o_ref[:, hs] = acc / l


def _fox_sample(q, k, v, lfn, cache_k, cache_v, cache_lft, page_table, layer, n_phys, t_rows):
    nb, n_pages = page_table.shape
    tn = (q.shape[0] - t_rows) // nb
    pg = min(PAGES_PER_STEP, n_pages)
    row0 = t_rows // tn
    base = layer * n_phys

    def kv_page(j):
        return pl.BlockSpec((None, PAGE_SIZE * N_HEADS, HEAD_DIM),
                            lambda b, g, pt: (base + pt[b, g * pg + j], 0, 0))

    def lf_page(j):
        return pl.BlockSpec((None, N_HEADS, PAGE_SIZE), lambda b, g, pt: (base + pt[b, g * pg + j], 0, 0))

    new_kv = pl.BlockSpec((tn * N_HEADS, HEAD_DIM), lambda b, g, pt: (b, 0))
    in_specs = [pl.BlockSpec((tn, D_ATTN), lambda b, g, pt: (row0 + b, 0)), new_kv, new_kv,
                pl.BlockSpec((None, N_HEADS, tn), lambda b, g, pt: (b, 0, 0))]
    in_specs += [kv_page(j) for j in range(pg)]
    in_specs += [kv_page(j) for j in range(pg)]
    in_specs += [lf_page(j) for j in range(pg)]
    grid_spec = pltpu.PrefetchScalarGridSpec(
        num_scalar_prefetch=1,
        grid=(nb, n_pages // pg),
        in_specs=in_specs,
        out_specs=pl.BlockSpec((tn, D_ATTN), lambda b, g, pt: (b, 0)),
        scratch_shapes=[pltpu.VMEM((N_HEADS * tn, 1), F32), pltpu.VMEM((N_HEADS * tn, 1), F32),
                        pltpu.VMEM((N_HEADS * tn, HEAD_DIM), F32), pltpu.VMEM((N_HEADS, 1), F32)])
    return pl.pallas_call(
        functools.partial(_fox_sample_kernel, pg=pg),
        grid_spec=grid_spec,
        out_shape=jax.ShapeDtypeStruct((nb * tn, D_ATTN), F32),
        compiler_params=_params(("parallel", "arbitrary")),
        name="fox_sample",
    )(page_table, q, k, v, lfn, *([cache_k] * pg), *([cache_v] * pg), *([cache_lft] * pg))


def _block_diag_wide(tall, rows_per_group, cols_per_group, n_cols):
    wide = jnp.concatenate([tall] * (n_cols // tall.shape[1]), axis=1)
    r = lax.broadcasted_iota(jnp.int32, wide.shape, 0)
    c = lax.broadcasted_iota(jnp.int32, wide.shape, 1)
    same = (r >> (rows_per_group.bit_length() - 1)) == (c >> (cols_per_group.bit_length() - 1))
    return jnp.where(same, wide, 0.0)


def _s5_kernel(u_ref, h0re_ref, h0im_ref, are_ref, aim_ref, ldt_ref, btre_ref, btim_ref,
               ctre_ref, ctim_ref, d_ref, wglu_ref, *rest, nb, nt, chained):
    o_ref, hre_out, him_out, bbre, bbim, cdre, cdim, bure, buim, hre, him, abre, abim, pwre, pwim = rest
    i = pl.program_id(0)
    half_in = D_SSM // 2
    half_st = D_STATE // 2

    @pl.when(i == 0)
    def _():
        bdre = _block_diag_wide(btre_ref[...], SSM_GROUP, SSM_STATE, D_STATE)
        bdim = _block_diag_wide(btim_ref[...], SSM_GROUP, SSM_STATE, D_STATE)
        cdre[...] = _block_diag_wide(ctre_ref[...], SSM_STATE, SSM_GROUP, D_SSM).astype(BF16)
        cdim[...] = _block_diag_wide(ctim_ref[...], SSM_STATE, SSM_GROUP, D_SSM).astype(BF16)
        a_re = are_ref[...]
        a_im = aim_ref[...]
        dt = jnp.exp(ldt_ref[...])
        mag = jnp.exp(dt * a_re)
        ab_re = mag * jnp.cos(dt * a_im)
        ab_im = mag * jnp.sin(dt * a_im)
        den = a_re * a_re + a_im * a_im
        nr = ab_re - 1.0
        coef_re = (nr * a_re + ab_im * a_im) / den
        coef_im = (ab_im * a_re - nr * a_im) / den
        bbre[...] = (coef_re * bdre - coef_im * bdim).astype(BF16)
        bbim[...] = (coef_re * bdim + coef_im * bdre).astype(BF16)
        abre[...] = ab_re
        abim[...] = ab_im
        hre[...] = h0re_ref[...]
        him[...] = h0im_ref[...]
        if chained:
            def power(j, carry):
                pr, pi = carry
                pwre[pl.ds(j, 1), :] = pr
                pwim[pl.ds(j, 1), :] = pi
                return ab_re * pr - ab_im * pi, ab_re * pi + ab_im * pr

            lax.fori_loop(0, nt, power, (ab_re, ab_im))

    n = nb * nt
    tsh = nt.bit_length() - 1
    bsh = nb.bit_length() - 1
    r = lax.broadcasted_iota(jnp.int32, (n, n), 0)
    c = lax.broadcasted_iota(jnp.int32, (n, n), 1)
    to_time_major = jnp.where(((r & (nb - 1)) == (c >> tsh)) & ((r >> bsh) == (c & (nt - 1))), 1.0, 0.0).astype(F32)
    to_seq_major = jnp.where(((c & (nb - 1)) == (r >> tsh)) & ((c >> bsh) == (r & (nt - 1))), 1.0, 0.0).astype(F32)
    u = _dot01(to_time_major, u_ref[...], 3)
    ub = u.astype(BF16)
    for k in range(2):
        rs = slice(k * half_in, (k + 1) * half_in)
        cs = slice(k * half_st, (k + 1) * half_st)
        bure[:, cs] = jnp.dot(ub[:, rs], bbre[rs, cs], preferred_element_type=F32)
        buim[:, cs] = jnp.dot(ub[:, rs], bbim[rs, cs], preferred_element_type=F32)

    for cidx in range(D_STATE // SCAN_COLS):
        cs = slice(cidx * SCAN_COLS, (cidx + 1) * SCAN_COLS)
        ar = jnp.broadcast_to(abre[:, cs], (nb, SCAN_COLS))
        ai = jnp.broadcast_to(abim[:, cs], (nb, SCAN_COLS))

        def step(t, carry, cs=cs, ar=ar, ai=ai):
            hr, hi = carry
            rows = pl.ds(pl.multiple_of(t * nb, nb), nb)
            nr = ar * hr - ai * hi + bure[rows, cs]
            ni = ar * hi + ai * hr + buim[rows, cs]
            bure[rows, cs] = nr
            buim[rows, cs] = ni
            return nr, ni

        if not chained:
            hr, hi = lax.fori_loop(0, nt, step, (hre[:, cs], him[:, cs]), unroll=8)
            hre[:, cs] = hr
            him[:, cs] = hi
            continue

        zeros = jnp.zeros((nb, SCAN_COLS), F32)
        fr, fi = lax.fori_loop(0, nt, step, (zeros, zeros), unroll=8)
        ptr = pwre[nt - 1:nt, cs]
        pti = pwim[nt - 1:nt, cs]
        cr = hre[:, cs]
        ci = him[:, cs]
        seg = lax.broadcasted_iota(jnp.int32, (nb, SCAN_COLS), 0)
        inr, ini = zeros, zeros
        for s in range(nb):
            inr = jnp.where(seg == s, cr, inr)
            ini = jnp.where(seg == s, ci, ini)
            cr, ci = fr[s:s + 1, :] + ptr * cr - pti * ci, fi[s:s + 1, :] + ptr * ci + pti * cr
        hre[:, cs] = cr
        him[:, cs] = ci

        def fix(t, carry, cs=cs, inr=inr, ini=ini):
            rows = pl.ds(pl.multiple_of(t * nb, nb), nb)
            pr = pwre[pl.ds(t, 1), cs]
            pi = pwim[pl.ds(t, 1), cs]
            bure[rows, cs] += pr * inr - pi * ini
            buim[rows, cs] += pr * ini + pi * inr
            return carry

        lax.fori_loop(0, nt, fix, 0, unroll=8)

    halves = []
    for k in range(2):
        rs = slice(k * half_st, (k + 1) * half_st)
        cs = slice(k * half_in, (k + 1) * half_in)
        halves.append(jnp.dot(bure[:, rs].astype(BF16), cdre[rs, cs], preferred_element_type=F32)
                      - jnp.dot(buim[:, rs].astype(BF16), cdim[rs, cs], preferred_element_type=F32))
    y = jnp.concatenate(halves, axis=1) + d_ref[...] * u
    gl = 0.5 * y * (1.0 + jnp.tanh(math.sqrt(2.0 / math.pi) * (y + 0.044715 * (y * y * y))))
    gate = jax.nn.sigmoid(jnp.dot(gl.astype(BF16), wglu_ref[...], preferred_element_type=F32))
    o_ref[...] = _dot01(to_seq_major, gl * gate, 3)
    hre_out[...] = hre[...]
    him_out[...] = him[...]


def _s5(u_all, row_block0, nb, nt, n_steps, h0re, h0im, consts, chained):
    rows = nb * nt
    ns = 1 if chained else nb
    are, aim, ldt, btre, btim, ctre, ctim, dvec, wglu = consts
    in_specs = [pl.BlockSpec((rows, D_SSM), lambda i: (row_block0 + i, 0)),
                _const_spec((ns, D_STATE)), _const_spec((ns, D_STATE)),
                _const_spec((1, D_STATE)), _const_spec((1, D_STATE)), _const_spec((1, D_STATE)),
                _const_spec(btre.shape), _const_spec(btim.shape),
                _const_spec(ctre.shape), _const_spec(ctim.shape),
                _const_spec((1, D_SSM)), _const_spec((D_SSM, D_SSM))]
    args = [u_all, h0re, h0im, are, aim, ldt, btre, btim, ctre, ctim, dvec, wglu]
    state = jax.ShapeDtypeStruct((ns, D_STATE), F32)
    return pl.pallas_call(
        functools.partial(_s5_kernel, nb=nb, nt=nt, chained=chained),
        grid=(n_steps,),
        in_specs=in_specs,
        out_specs=(pl.BlockSpec((rows, D_SSM), lambda i: (i, 0)),
                   pl.BlockSpec((ns, D_STATE), lambda i: (0, 0)),
                   pl.BlockSpec((ns, D_STATE), lambda i: (0, 0))),
        out_shape=(jax.ShapeDtypeStruct((n_steps * rows, D_SSM), F32), state, state),
        scratch_shapes=[pltpu.VMEM((D_SSM, D_STATE), BF16), pltpu.VMEM((D_SSM, D_STATE), BF16),
                        pltpu.VMEM((D_STATE, D_SSM), BF16), pltpu.VMEM((D_STATE, D_SSM), BF16),
                        pltpu.VMEM((rows, D_STATE), F32), pltpu.VMEM((rows, D_STATE), F32),
                        pltpu.VMEM((ns, D_STATE), F32), pltpu.VMEM((ns, D_STATE), F32),
                        pltpu.VMEM((1, D_STATE), F32), pltpu.VMEM((1, D_STATE), F32),
                        pltpu.VMEM((nt, D_STATE), F32), pltpu.VMEM((nt, D_STATE), F32)],
        compiler_params=_params(("arbitrary",)),
        name="s5_prompt" if chained else "s5_sample",
    )(*args)


def _pool_kernel(cur_ref, *rest, seq_len, prev_rows, pos0, prompt):
    if prompt:
        prev_ref, buf_ref, pw_ref, ps_ref, o_ref = rest
    else:
        buf_ref, pw_ref, ps_ref, o_ref = rest
    i = pl.program_id(0)
    cur = cur_ref[...]
    n = cur.shape[0]
    if prompt:
        prev = jnp.where(i == 0, buf_ref[...], prev_ref[...])
    else:
        prev = buf_ref[...]
    npv = prev.shape[0]
    lsh = seq_len.bit_length() - 1
    psh = prev_rows.bit_length() - 1
    r = lax.broadcasted_iota(jnp.int32, (n, n), 0)
    c = lax.broadcasted_iota(jnp.int32, (n, n), 1)
    same = (r >> lsh) == (c >> lsh)
    d = r - c
    r1 = lax.broadcasted_iota(jnp.int32, (n, npv), 0)
    c1 = lax.broadcasted_iota(jnp.int32, (n, npv), 1)
    same1 = (r1 >> lsh) == (c1 >> psh)
    d1 = (r1 & (seq_len - 1)) + prev_rows - (c1 & (prev_rows - 1))
    local = lax.broadcasted_iota(jnp.int32, (n, 1), 0) & (seq_len - 1)
    pos = pos0 + i * n + local
    outs = []
    for gi, w in enumerate(POOL_WINDOWS):
        gs = slice(gi * POOL_GROUP, (gi + 1) * POOL_GROUP)
        band = jnp.where(same & (d >= 0) & (d < w), 1.0, 0.0).astype(F32)
        band_prev = jnp.where(same1 & (d1 < w), 1.0, 0.0).astype(F32)
        xg = cur[:, gs]
        wsum = _dot01(band, xg, 3) + _dot01(band_prev, prev[:, gs], 3)
        cnt = jnp.minimum(w, pos + 1).astype(F32)
        pooled = wsum / cnt - xg
        outs.append(jnp.dot(pooled.astype(BF16), pw_ref[gi], preferred_element_type=F32))
    o_ref[...] = jnp.concatenate(outs, axis=1) * ps_ref[...]


def _pool_prompt(p_all, buf_pad, pw, ps, t):
    tt = ROW_TILE
    ratio = tt // POOL_GROUP
    return pl.pallas_call(
        functools.partial(_pool_kernel, seq_len=tt, prev_rows=POOL_GROUP, pos0=0, prompt=True),
        grid=(t // tt,),
        in_specs=[pl.BlockSpec((tt, D_POOL), lambda i: (i, 0)),
                  pl.BlockSpec((POOL_GROUP, D_POOL), lambda i: (jnp.maximum(i * ratio - 1, 0), 0)),
                  _const_spec((POOL_GROUP, D_POOL)), _const_spec(pw.shape), _const_spec((1, D_POOL))],
        out_specs=pl.BlockSpec((tt, D_POOL), lambda i: (i, 0)),
        out_shape=jax.ShapeDtypeStruct((t, D_POOL), F32),
        compiler_params=_params(("parallel",)),
        name="pool_prompt",
    )(p_all, p_all, buf_pad, pw, ps)


def _pool_sample(p_all, buf_pad, pw, ps, t, nb, tn, past):
    rows = nb * tn
    blk = t // rows
    return pl.pallas_call(
        functools.partial(_pool_kernel, seq_len=tn, prev_rows=16, pos0=past, prompt=False),
        grid=(1,),
        in_specs=[pl.BlockSpec((rows, D_POOL), lambda i: (blk, 0)),
                  _const_spec(buf_pad.shape), _const_spec(pw.shape), _const_spec((1, D_POOL))],
        out_specs=pl.BlockSpec((rows, D_POOL), lambda i: (0, 0)),
        out_shape=jax.ShapeDtypeStruct((rows, D_POOL), F32),
        compiler_params=_params(("arbitrary",)),
        name="pool_sample",
    )(p_all, buf_pad, pw, ps)


def _outproj_kernel(xp_ref, xs_ref, ap_ref, as_ref, sp_ref, ss_ref, pp_ref, ps_ref, ong_ref, wout_ref, n2_ref,
                    wrh_ref, wrl_ref, rb_ref, xm_ref, xn_ref, ids_ref, gates_ref, *, prompt_tiles):
    g = ong_ref[...]
    sample = pl.program_id(0) >= prompt_tiles
    ma = _rms(jnp.where(sample, as_ref[...], ap_ref[...]), g[:, :D_ATTN]).astype(BF16)
    ms = _rms(jnp.where(sample, ss_ref[...], sp_ref[...]), g[:, D_ATTN:D_ATTN + D_SSM]).astype(BF16)
    mp = _rms(jnp.where(sample, ps_ref[...], pp_ref[...]), g[:, D_ATTN + D_SSM:]).astype(BF16)
    xm = (jnp.where(sample, xs_ref[...], xp_ref[...])
          + jnp.dot(ma, wout_ref[:D_ATTN, :], preferred_element_type=F32)
          + jnp.dot(ms, wout_ref[D_ATTN:D_ATTN + D_SSM, :], preferred_element_type=F32)
          + jnp.dot(mp, wout_ref[D_ATTN + D_SSM:, :], preferred_element_type=F32))
    xm_ref[...] = xm
    xn = _rms(xm, n2_ref[...])
    tm = xn.shape[0]
    for j in range(xn.shape[1] // ROW_CHUNK):
        xn_ref[_chunk_rows(j, tm), :] = xn[:, j * ROW_CHUNK:(j + 1) * ROW_CHUNK]
    xh = xn.astype(BF16)
    xl = (xn - xh.astype(F32)).astype(BF16)
    logits = (jnp.dot(xh, wrh_ref[...], preferred_element_type=F32)
              + jnp.dot(xl, wrh_ref[...], preferred_element_type=F32)
              + jnp.dot(xh, wrl_ref[...], preferred_element_type=F32))
    lt = jnp.transpose(logits)[:rb_ref.shape[0]] + rb_ref[...]
    ng, ne = N_EXPERT_GROUPS, EXPERTS_PER_GROUP
    gl = [lt[j:j + 1, :] for j in range(ng)]
    gmax = functools.reduce(jnp.maximum, gl)
    gsel = jnp.full(gmax.shape, ng - 1, jnp.int32)
    for j in range(ng - 2, -1, -1):
        gsel = jnp.where(gl[j] == gmax, j, gsel)
    gw = 1.0 / functools.reduce(jnp.add, [jnp.exp(v - gmax) for v in gl])
    es = []
    for j in range(ne):
        v = lt[ng + (ng - 1) * ne + j:ng + (ng - 1) * ne + j + 1, :]
        for gi in range(ng - 2, -1, -1):
            v = jnp.where(gsel == gi, lt[ng + gi * ne + j:ng + gi * ne + j + 1, :], v)
        es.append(v)

    def first_argmax(vals):
        vmax = functools.reduce(jnp.maximum, vals)
        idx = jnp.full(vmax.shape, len(vals) - 1, jnp.int32)
        for j in range(len(vals) - 2, -1, -1):
            idx = jnp.where(vals[j] == vmax, j, idx)
        return vmax, idx

    v1, i1 = first_argmax(es)
    v2, i2 = first_argmax([jnp.where(i1 == j, -jnp.inf, es[j]) for j in range(ne)])
    t = jnp.exp(v2 - v1)
    p1 = 1.0 / (1.0 + t)
    p2 = t / (1.0 + t)
    row = lax.broadcasted_iota(jnp.int32, ids_ref.shape, 0)
    ids_ref[...] = jnp.where(row == 0, gsel * ne + i1, jnp.where(row == 1, gsel * ne + i2, 0))
    gates_ref[...] = jnp.where(row == 0, p1 * gw, jnp.where(row == 1, p2 * gw, 0.0))


def _outproj(x_pair, m, mix_prompt, mix_sample, layer, ong, wout, n2, wr_hi, wr_lo, rb):
    d = x_pair[0].shape[1]
    tm = ROW_TILE
    pt = mix_prompt[0].shape[0] // tm
    x_specs, x_args = _x_specs(x_pair, tm, pt, lambda i: i)
    row = lambda w: pl.BlockSpec((tm, w), lambda i: (i, 0))
    prow = lambda w: pl.BlockSpec((tm, w), lambda i: (jnp.minimum(i, pt - 1), 0))
    srow = lambda w: pl.BlockSpec((tm, w), lambda i: (jnp.maximum(i - pt, 0), 0))
    col = pl.BlockSpec((8, tm), lambda i: (0, i))
    return pl.pallas_call(
        functools.partial(_outproj_kernel, prompt_tiles=pt),
        grid=(m // tm,),
        in_specs=[*x_specs, prow(D_ATTN), srow(D_ATTN), prow(D_SSM), srow(D_SSM), prow(D_POOL), srow(D_POOL),
                  _const_spec((1, d)),
                  _layer_spec(wout, layer), _const_spec((1, d)), _const_spec(wr_hi.shape),
                  _const_spec(wr_lo.shape), _const_spec(rb.shape)],
        out_specs=(row(d), pl.BlockSpec((tm * ROW_CHUNKS, ROW_CHUNK), lambda i: (i, 0)), col, col),
        out_shape=(jax.ShapeDtypeStruct((m, d), F32), jax.ShapeDtypeStruct((m * ROW_CHUNKS, ROW_CHUNK), F32),
                   jax.ShapeDtypeStruct((8, m), jnp.int32), jax.ShapeDtypeStruct((8, m), F32)),
        compiler_params=_params(("parallel",)),
        name="outproj_router",
    )(*x_args, mix_prompt[0], mix_sample[0], mix_prompt[1], mix_sample[1], mix_prompt[2], mix_sample[2],
      ong, wout, n2, wr_hi, wr_lo, rb)


def _token_copy(src_hbm, tok, dst, r, sem):
    return pltpu.make_async_copy(src_hbm.at[pl.ds(pl.multiple_of(tok * ROW_CHUNKS, ROW_CHUNKS), ROW_CHUNKS), :],
                                 dst.at[pl.ds(pl.multiple_of(r * ROW_CHUNKS, ROW_CHUNKS), ROW_CHUNKS), :], sem)


def _gather_tokens(src_hbm, idx_ref, first_idx, dst, sem, n, priority=0):
    def issue(r, carry):
        _token_copy(src_hbm, idx_ref[first_idx + r], dst, r, sem).start(priority=priority)
        return carry

    lax.fori_loop(0, n, issue, 0, unroll=8)


def _wait_tokens(src_hbm, dst, sem):
    pltpu.make_async_copy(src_hbm.at[pl.ds(0, dst.shape[0]), :], dst, sem).wait()


def _expert_kernel(vt_ref, ve_ref, vlo_ref, vhi_ref, tok_ref, x_hbm, gate_ref, wg_ref, wu_ref, wd_ref,
                   o_ref, xs, wgb, wub, wdb, sem, *, n_tiles):
    v = pl.program_id(0)
    tm = gate_ref.shape[0]
    lo = vlo_ref[v]
    hi = vhi_ref[v]
    tile = vt_ref[v]
    base = tile * tm
    slot = tile & 1
    nonempty = hi > lo
    first = jnp.logical_and(nonempty, lo == base)
    changed = jnp.logical_or(v == 0, ve_ref[v] != ve_ref[jnp.maximum(v - 1, 0)])

    @pl.when(changed)
    def _():
        wgb[...] = wg_ref[...].astype(BF16)
        wub[...] = wu_ref[...].astype(BF16)
        wdb[...] = wd_ref[...].astype(BF16)

    @pl.when(v == 0)
    def _():
        _gather_tokens(x_hbm, tok_ref, 0, xs.at[0], sem.at[0], tm)

    @pl.when(first)
    def _():
        _wait_tokens(x_hbm, xs.at[slot], sem.at[slot])

    @pl.when(jnp.logical_and(first, tile + 1 < n_tiles))
    def _():
        _gather_tokens(x_hbm, tok_ref, base + tm, xs.at[1 - slot], sem.at[1 - slot], tm)

    @pl.when(nonempty)
    def _():
        xt = xs.at[slot]
        xb = jnp.concatenate([xt[_chunk_rows(j, tm), :] for j in range(ROW_CHUNKS)], axis=1).astype(BF16)
        hg = jnp.dot(xb, wgb[...], preferred_element_type=F32)
        hu = jnp.dot(xb, wub[...], preferred_element_type=F32)
        row = base + lax.broadcasted_iota(jnp.int32, (tm, 1), 0)
        keep = jnp.logical_and(row >= lo, row < hi)
        act = jnp.where(keep, (hg * jax.nn.sigmoid(hg)) * hu * gate_ref[...], 0.0)
        y = jnp.dot(act.astype(BF16), wdb[...], preferred_element_type=F32)

        @pl.when(first)
        def _():
            for j in range(ROW_CHUNKS):
                o_ref[_chunk_rows(j, tm), :] = y[:, j * ROW_CHUNK:(j + 1) * ROW_CHUNK]

        @pl.when(jnp.logical_not(first))
        def _():
            for j in range(ROW_CHUNKS):
                o_ref[_chunk_rows(j, tm), :] += y[:, j * ROW_CHUNK:(j + 1) * ROW_CHUNK]


def _expert_ffn(plan, xn_rows, layer, wg, wu, wd):
    v_tile, v_exp, v_lo, v_hi, tok_sorted, gate_sorted = plan
    n_flat = tok_sorted.shape[0]
    d = wg.shape[2]
    tm = ROW_TILE
    grid_spec = pltpu.PrefetchScalarGridSpec(
        num_scalar_prefetch=5,
        grid=(v_tile.shape[0],),
        in_specs=[pl.BlockSpec(memory_space=pl.ANY),
                  pl.BlockSpec((tm, 1), lambda v, vt, ve, lo, hi, tok: (vt[v], 0)),
                  pl.BlockSpec((None, None, d, D_EXPERT), lambda v, vt, ve, lo, hi, tok: (layer, ve[v], 0, 0)),
                  pl.BlockSpec((None, None, d, D_EXPERT), lambda v, vt, ve, lo, hi, tok: (layer, ve[v], 0, 0)),
                  pl.BlockSpec((None, None, D_EXPERT, d), lambda v, vt, ve, lo, hi, tok: (layer, ve[v], 0, 0))],
        out_specs=pl.BlockSpec((tm * ROW_CHUNKS, ROW_CHUNK), lambda v, vt, ve, lo, hi, tok: (vt[v], 0)),
        scratch_shapes=[pltpu.VMEM((2, tm * ROW_CHUNKS, ROW_CHUNK), F32),
                        pltpu.VMEM((d, D_EXPERT), BF16), pltpu.VMEM((d, D_EXPERT), BF16),
                        pltpu.VMEM((D_EXPERT, d), BF16), pltpu.SemaphoreType.DMA((2,))])
    return pl.pallas_call(
        functools.partial(_expert_kernel, n_tiles=n_flat // tm),
        grid_spec=grid_spec,
        out_shape=jax.ShapeDtypeStruct((n_flat * ROW_CHUNKS, ROW_CHUNK), F32),
        compiler_params=_params(("arbitrary",)),
        name="expert_ffn",
    )(v_tile, v_exp, v_lo, v_hi, tok_sorted, xn_rows, gate_sorted, wg, wu, wd)


def _combine_kernel(d1_ref, d2_ref, xm_ref, ys_hbm, *rest, n_tiles, prompt_tiles):
    out_refs, (buf1, buf2, sem) = rest[:-3], rest[-3:]
    i = pl.program_id(0)
    tm = xm_ref.shape[0]
    slot = i & 1

    def start(tile, s):
        _gather_tokens(ys_hbm, d1_ref, tile * tm, buf1.at[s], sem.at[0, s], tm)
        _gather_tokens(ys_hbm, d2_ref, tile * tm, buf2.at[s], sem.at[1, s], tm, priority=1)

    pl.when(i == 0)(lambda: start(0, 0))
    _wait_tokens(ys_hbm, buf1.at[slot], sem.at[0, slot])
    _wait_tokens(ys_hbm, buf2.at[slot], sem.at[1, slot])
    pl.when(i + 1 < n_tiles)(lambda: start(i + 1, 1 - slot))
    rows1 = buf1.at[slot]
    rows2 = buf2.at[slot]

    def write(o_ref):
        for j in range(ROW_CHUNKS):
            cs = slice(j * ROW_CHUNK, (j + 1) * ROW_CHUNK)
            o_ref[:, cs] = xm_ref[:, cs] + rows1[_chunk_rows(j, tm), :] + rows2[_chunk_rows(j, tm), :]

    if prompt_tiles is None:
        write(out_refs[0])
    else:
        pl.when(i < prompt_tiles)(lambda: write(out_refs[0]))
        pl.when(i >= prompt_tiles)(lambda: write(out_refs[1]))


def _combine(dest1, dest2, xm, ys_rows, split_rows=None):
    m, d = xm.shape
    tm = ROW_TILE
    row = pl.BlockSpec((tm, d), lambda i, a, b: (i, 0))
    if split_rows is None:
        pt = None
        out_specs = row
        out_shape = jax.ShapeDtypeStruct((m, d), F32)
    else:
        pt = split_rows // tm
        out_specs = (pl.BlockSpec((tm, d), lambda i, a, b: (jnp.minimum(i, pt - 1), 0)),
                     pl.BlockSpec((tm, d), lambda i, a, b: (jnp.maximum(i - pt, 0), 0)))
        out_shape = (jax.ShapeDtypeStruct((split_rows, d), F32), jax.ShapeDtypeStruct((m - split_rows, d), F32))
    buf = pltpu.VMEM((2, tm * ROW_CHUNKS, ROW_CHUNK), F32)
    grid_spec = pltpu.PrefetchScalarGridSpec(
        num_scalar_prefetch=2,
        grid=(m // tm,),
        in_specs=[row, pl.BlockSpec(memory_space=pl.ANY)],
        out_specs=out_specs,
        scratch_shapes=[buf, buf, pltpu.SemaphoreType.DMA((2, 2))])
    return pl.pallas_call(
        functools.partial(_combine_kernel, n_tiles=m // tm, prompt_tiles=pt),
        grid_spec=grid_spec,
        out_shape=out_shape,
        compiler_params=_params(("arbitrary",)),
        name="moe_combine",
    )(dest1, dest2, xm, ys_rows)


def _dispatch_plan(ids, gates, m):
    tm = ROW_TILE
    n_flat = 2 * m
    n_tiles = n_flat // tm
    e_flat = ids[:2].reshape(n_flat)
    g_flat = gates[:2].reshape(n_flat)
    tok = jnp.arange(n_flat, dtype=jnp.int32) % m
    _, tok_sorted, gate_sorted = lax.sort((e_flat, tok, g_flat), num_keys=1, is_stable=True)
    onehot = (e_flat[:, None] == jnp.arange(N_EXPERTS, dtype=jnp.int32)[None, :]).astype(jnp.int32)
    csum = jnp.cumsum(onehot, axis=0)
    counts = csum[-1]
    row_end = jnp.cumsum(counts)
    row_start = row_end - counts
    dest = jnp.sum(onehot * (row_start[None, :] + csum - 1), axis=1).astype(jnp.int32)
    cuts = jnp.sort(jnp.concatenate([jnp.arange(n_tiles, dtype=jnp.int32) * tm, row_start[1:].astype(jnp.int32)]))
    v_lo = cuts
    v_hi = jnp.concatenate([cuts[1:], jnp.full((1,), n_flat, jnp.int32)])
    v_tile = jnp.minimum(v_lo // tm, n_tiles - 1).astype(jnp.int32)
    v_exp = jnp.minimum(jnp.sum((v_lo[:, None] >= row_end[None, :]).astype(jnp.int32), axis=1), N_EXPERTS - 1)
    plan = (v_tile, v_exp.astype(jnp.int32), v_lo, v_hi, tok_sorted, gate_sorted.reshape(n_flat, 1))
    return plan, dest[:m], dest[m:]


def _tall_blocks(w):
    g, c, r = w.shape
    tall = jnp.swapaxes(w, 1, 2).reshape(g * r, c)
    return jnp.tile(tall, (1, 128 // c))


def kernel(x_prompt, x_sample, cache_k, cache_v, cache_logf, page_table, state_ssm_re, state_ssm_im, state_pool, norm1_g, w_in, f_bias, q_gain, k_gain, ssm_a_re, ssm_a_im, ssm_log_dt, ssm_b_re, ssm_b_im, ssm_c_re, ssm_c_im, ssm_d, ssm_w_glu, pool_w, pool_scale, out_norm_g, w_out, norm2_g, router_group_w, router_group_b, router_expert_w, router_expert_b, moe_w_gate, moe_w_up, moe_w_down):
    n_prompt, t, d = x_prompt.shape
    nb, tn, _ = x_sample.shape
    depth = w_in.shape[0]
    n_phys = cache_k.shape[1]
    past = page_table.shape[1] * cache_k.shape[2]
    assert n_prompt == 1
    ms = nb * tn
    m = t + ms
    x_pair = (x_prompt.reshape(t, d), x_sample.reshape(ms, d))

    ck = cache_k.reshape(depth * n_phys, PAGE_SIZE * N_HEADS, HEAD_DIM)
    cv = cache_v.reshape(depth * n_phys, PAGE_SIZE * N_HEADS, HEAD_DIM)
    clf = jnp.swapaxes(cache_logf, 2, 3).reshape(depth * n_phys, N_HEADS, PAGE_SIZE)
    zero_state = jnp.zeros((1, D_STATE), F32)
    zero_buf = jnp.zeros((POOL_GROUP, D_POOL), F32)

    wqkv = w_in[:, :, :3 * D_ATTN].astype(BF16)
    wup = w_in[:, :, 3 * D_ATTN + N_HEADS:].astype(BF16)
    wft = jnp.pad(jnp.swapaxes(w_in[:, :, 3 * D_ATTN:3 * D_ATTN + N_HEADS], 1, 2), ((0, 0), (0, 8), (0, 0))).astype(BF16)
    wout = w_out.astype(BF16)

    outs_p, outs_s = [], []
    for l in range(depth):
        q, k_p, v_p, k_s, v_s, kb, vb, lft, u, p_in = _proj(
            x_pair, t, m, l, norm1_g[l].reshape(1, d), wqkv, wup, wft, f_bias[l].reshape(N_HEADS, 1),
            q_gain[l].reshape(1, HEAD_DIM), k_gain[l].reshape(1, HEAD_DIM))

        c_prompt = _prompt_cumsum(lft, t)
        qk_bound = 1.02 * (HEAD_DIM ** 0.5) * jnp.max(jnp.abs(q_gain[l])) * jnp.max(jnp.abs(k_gain[l]))
        a_p = _fox_prompt(q, kb, vb, c_prompt, t, qk_bound)
        lfn = lft[:, t:].reshape(N_HEADS, nb, tn).transpose(1, 0, 2)
        a_s = _fox_sample(q, k_s, v_s, lfn, ck, cv, clf, page_table, l, n_phys, t)

        consts = (ssm_a_re[l].reshape(1, D_STATE), ssm_a_im[l].reshape(1, D_STATE),
                  jnp.repeat(ssm_log_dt[l], SSM_STATE).reshape(1, D_STATE),
                  _tall_blocks(ssm_b_re[l]), _tall_blocks(ssm_b_im[l]),
                  _tall_blocks(ssm_c_re[l]), _tall_blocks(ssm_c_im[l]),
                  ssm_d[l].reshape(1, D_SSM), ssm_w_glu[l].astype(BF16))
        s_p, hre_p, him_p = _s5(u, 0, SCAN_SEGMENTS, ROW_TILE // SCAN_SEGMENTS, t // ROW_TILE, zero_state,
                                zero_state, consts, True)
        s_s, hre_s, him_s = _s5(u, t // ms, nb, tn, 1, state_ssm_re[l].reshape(nb, D_STATE),
                                state_ssm_im[l].reshape(nb, D_STATE), consts, False)

        pw = pool_w[l].astype(BF16)
        ps = pool_scale[l].reshape(1, D_POOL)
        po_p = _pool_prompt(p_in, zero_buf, pw, ps, t)
        buf_s = jnp.pad(state_pool[l], ((0, 0), (1, 0), (0, 0))).reshape(nb * 16, D_POOL)
        po_s = _pool_sample(p_in, buf_s, pw, ps, t, nb, tn, past)

        n_route = N_EXPERT_GROUPS + N_EXPERTS
        wr = jnp.pad(jnp.concatenate([router_group_w[l], router_expert_w[l]], axis=1), ((0, 0), (0, 128 - n_route)))
        rb = jnp.pad(jnp.concatenate([router_group_b[l], router_expert_b[l]]), (0, 24 - n_route)).reshape(24, 1)
        wr_hi = wr.astype(BF16)
        wr_lo = (wr - wr_hi.astype(F32)).astype(BF16)
        xm, xn, ids, gates = _outproj(x_pair, m, (a_p, s_p, po_p), (a_s, s_s, po_s), l, out_norm_g[l].reshape(1, d),
                                      wout, norm2_g[l].reshape(1, d), wr_hi, wr_lo, rb)

        plan, dest1, dest2 = _dispatch_plan(ids, gates, m)
        ys = _expert_ffn(plan, xn, l, moe_w_gate, moe_w_up, moe_w_down)
        if l + 1 < depth:
            x_pair = (_combine(dest1, dest2, xm, ys), None)
        else:
            y_prompt, y_sample = _combine(dest1, dest2, xm, ys, split_rows=t)

        lf_rows = lft.T
        outs_p.append((k_p.reshape(1, t, N_HEADS, HEAD_DIM), v_p.reshape(1, t, N_HEADS, HEAD_DIM),
                       lf_rows[:t].reshape(1, t, N_HEADS),
                       hre_p.reshape(1, SSM_GROUPS, SSM_STATE), him_p.reshape(1, SSM_GROUPS, SSM_STATE),
                       p_in[t - POOL_BUF:t].reshape(1, POOL_BUF, D_POOL)))
        p_s = p_in[t:].reshape(nb, tn, D_POOL)
        msh = ms * N_HEADS
        outs_s.append((k_s[:msh].reshape(nb, tn, N_HEADS, HEAD_DIM), v_s[:msh].reshape(nb, tn, N_HEADS, HEAD_DIM),
                       lf_rows[t:].reshape(nb, tn, N_HEADS),
                       hre_s.reshape(nb, SSM_GROUPS, SSM_STATE), him_s.reshape(nb, SSM_GROUPS, SSM_STATE),
                       jnp.concatenate([state_pool[l], p_s], axis=1)[:, -POOL_BUF:]))

    stack = lambda outs, i: jnp.stack([o[i] for o in outs], axis=0)
    return (y_prompt.reshape(1, t, d), y_sample.reshape(nb, tn, d),
            *[stack(outs_p, i) for i in range(6)], *[stack(outs_s, i) for i in range(6)])
```
